```python
import math
import jax, jax.numpy as jnp
from jax import lax
import numpy as np

D_MODEL = 1024
BATCH = 16
SEQ = 2048
DEPTH = 4

CTX_LEN = 256
GRID_W = 64
N_MIXERS = 4
EPS = 1e-6
F32 = jnp.float32

SSM_INNER = 2 * D_MODEL
SSM_HEAD_DIM = 64
SSM_HEADS = SSM_INNER // SSM_HEAD_DIM
SSM_GROUPS = 4
SSM_STATE = 128
SSM_CONV = 4
SSM_CHUNK = 128
SSM_CONV_DIM = SSM_INNER + 2 * SSM_GROUPS * SSM_STATE
SSM_IN_DIM = SSM_INNER + SSM_CONV_DIM + 2 * SSM_HEADS

ATT_HEAD_DIM = 64
ATT_Q_HEADS = D_MODEL // ATT_HEAD_DIM
ATT_KV_HEADS = 4
ATT_BLOCK = 128
ROPE_THETA = 10000.0

POOL_WINDOWS = (2, 4, 8, 16)
POOL_GROUP = D_MODEL // len(POOL_WINDOWS)

LRU_WIDTH = 1280
LRU_BLOCKS = 10
LRU_BLOCK_W = LRU_WIDTH // LRU_BLOCKS
LRU_CONV = 4
LRU_C = 8.0

FFN_HIDDEN = 3584
N_EXPERTS = 8
TOP_K = 2
MOE_BLOCK = 512

kernel_name = "hybrid_interleaved_diffusion_trunk"


def rmsnorm(u, g):
    uf = u.astype(F32)
    y = uf * lax.rsqrt(jnp.mean(uf * uf, axis=-1, keepdims=True) + EPS)
    return (y * g.astype(F32)).astype(u.dtype)


def modulate(u, g, shift, scale):
    return rmsnorm(u, g) * (1 + scale) + shift


def dwconv_centred(u, w, b):
    K, C = w.shape
    left = K // 2
    out = lax.conv_general_dilated(u, w[:, None, :].astype(u.dtype), window_strides=(1,),
                                   padding=[(left, K - 1 - left)],
                                   dimension_numbers=('NWC', 'WIO', 'NWC'), feature_group_count=C)
    return out + b


def axial_rope_tables(seq_len):
    rows = seq_len // GRID_W
    row = jnp.repeat(jnp.arange(rows), GRID_W).astype(F32)
    col = jnp.tile(jnp.arange(GRID_W), rows).astype(F32)
    n_freq = ATT_HEAD_DIM // 4
    inv_freq = ROPE_THETA ** (-jnp.arange(n_freq, dtype=F32) / n_freq)
    ang = jnp.concatenate([row[:, None] * inv_freq, col[:, None] * inv_freq], axis=-1)
    return jnp.cos(ang), jnp.sin(ang)


def apply_rope(u, cos, sin):
    shape = (1, u.shape[1]) + (1,) * (u.ndim - 3) + (cos.shape[-1],)
    cos = cos.reshape(shape)
    sin = sin.reshape(shape)
    u1, u2 = jnp.split(u.astype(F32), 2, axis=-1)
    return jnp.concatenate([u1 * cos - u2 * sin, u1 * sin + u2 * cos], axis=-1).astype(u.dtype)


def ssd_scan(xdt, la, bm, cm, h0):
    b, n, G, R, P = xdt.shape
    N = bm.shape[-1]
    Q = SSM_CHUNK
    nc = n // Q
    xc = xdt.reshape(b, nc, Q, G, R, P)
    ac = jnp.cumsum(la.reshape(b, nc, Q, G, R), axis=2)
    bc = bm.reshape(b, nc, Q, G, N)
    cc = cm.reshape(b, nc, Q, G, N)
    lower = jnp.tril(jnp.ones((Q, Q), bool))[:, :, None, None]
    decay = jnp.exp(jnp.where(lower, ac[:, :, :, None] - ac[:, :, None, :], -jnp.inf))
    cb = jnp.einsum('bclgn,bcsgn->bclsg', cc, bc)
    y_diag = jnp.einsum('bclsgr,bcsgrp->bclgrp', cb[..., None] * decay, xc)
    x_to_end = xc * jnp.exp(ac[:, :, -1:] - ac)[..., None]
    states = jnp.einsum('bcsgn,bcsgrp->bcgrpn', bc, x_to_end)
    chunk_decay = jnp.exp(ac[:, :, -1])

    def step(h, inp):
        d, s = inp
        return h * d[..., None, None] + s, h

    h_last, h_in = lax.scan(step, h0, (jnp.moveaxis(chunk_decay, 1, 0), jnp.moveaxis(states, 1, 0)))
    h_in = jnp.moveaxis(h_in, 0, 1)
    y_off = jnp.einsum('bclgn,bcgrpn->bclgrp', cc, h_in) * jnp.exp(ac)[..., None]
    return (y_diag + y_off).reshape(b, n, G, R, P), h_last


def mamba2_mixer(ul, uc, w_in, conv_w, conv_b, dt_bias, a_log, d_skip, norm_g, w_out, ctx_out):
    G, R, P, N, H = SSM_GROUPS, SSM_HEADS // SSM_GROUPS, SSM_HEAD_DIM, SSM_STATE, SSM_HEADS
    neg_a = -jnp.exp(a_log.astype(F32))

    def project(u):
        b, n, _ = u.shape
        zxbcdt = u @ w_in
        z = zxbcdt[..., :SSM_INNER]
        xbc = jax.nn.silu(dwconv_centred(zxbcdt[..., SSM_INNER:SSM_INNER + SSM_CONV_DIM], conv_w, conv_b))
        xs = xbc[..., :SSM_INNER].reshape(b, n, G, R, P).astype(F32)
        bm = xbc[..., SSM_INNER:SSM_INNER + G * N].reshape(b, n, G, N).astype(F32)
        cm = xbc[..., SSM_INNER + G * N:].reshape(b, n, G, N).astype(F32)
        dt = jax.nn.softplus(zxbcdt[..., SSM_INNER + SSM_CONV_DIM:].reshape(b, n, 2, H).astype(F32)
                             + dt_bias.astype(F32))
        return z, xs, bm, cm, dt

    def scan_dir(d, xs, bm, cm, dt, h0):
        b, n = xs.shape[:2]
        dtd = dt[:, :, d].reshape(b, n, G, R)
        la = dtd * neg_a[d].reshape(G, R)
        xdt = xs * dtd[..., None]
        if d == 1:
            xdt, la, bm, cm = (jnp.flip(t, 1) for t in (xdt, la, bm, cm))
        y, h_last = ssd_scan(xdt, la, bm, cm, h0)
        if d == 1:
            y = jnp.flip(y, 1)
        return y, h_last

    def finish(z, xs, y, dtype):
        b, n = xs.shape[:2]
        y = y + xs * d_skip.astype(F32).reshape(G, R)[:, :, None]
        y = y.reshape(b, n, SSM_INNER) * jax.nn.silu(z.astype(F32))
        return rmsnorm(y, norm_g).astype(dtype) @ w_out

    zc, xsc, bmc, cmc, dtc = project(uc)
    h0 = jnp.zeros((uc.shape[0], G, R, P, N), F32)
    yc_f, hc_f = scan_dir(0, xsc, bmc, cmc, dtc, h0)
    yc_b, hc_b = scan_dir(1, xsc, bmc, cmc, dtc, h0)
    zl, xsl, bml, cml, dtl = project(ul)
    yl_f, _ = scan_dir(0, xsl, bml, cml, dtl, hc_f)
    yl_b, _ = scan_dir(1, xsl, bml, cml, dtl, hc_b)
    yl = finish(zl, xsl, yl_f + yl_b, ul.dtype)
    yc = finish(zc, xsc, yc_f + yc_b, uc.dtype) if ctx_out else None
    return yl, yc


def attend(q, k, v):
    s = jnp.einsum('bqgrd,bkgd->bgrqk', q, k, preferred_element_type=F32) * (ATT_HEAD_DIM ** -0.5)
    p = jax.nn.softmax(s, axis=-1).astype(v.dtype)
    return jnp.einsum('bgrqk,bkgd->bqgrd', p, v)


def gqa_mixer(ul, uc, w_qkv, q_g, k_g, w_out, rope_cos, rope_sin, ctx_out):
    R = ATT_Q_HEADS // ATT_KV_HEADS
    nq = ATT_Q_HEADS * ATT_HEAD_DIM
    nk = ATT_KV_HEADS * ATT_HEAD_DIM

    def project(u):
        b, n, _ = u.shape
        qkv = u @ w_qkv
        q = rmsnorm(qkv[..., :nq].reshape(b, n, ATT_KV_HEADS, R, ATT_HEAD_DIM), q_g)
        k = rmsnorm(qkv[..., nq:nq + nk].reshape(b, n, ATT_KV_HEADS, ATT_HEAD_DIM), k_g)
        v = qkv[..., nq + nk:].reshape(b, n, ATT_KV_HEADS, ATT_HEAD_DIM)
        return q, k, v

    ql, kl, vl = project(ul)
    qc, kc, vc = project(uc)
    ql = apply_rope(ql, rope_cos, rope_sin)
    kl = apply_rope(kl, rope_cos, rope_sin)
    keys = jnp.concatenate([kc, kl], axis=1)
    vals = jnp.concatenate([vc, vl], axis=1)
    b, n, _ = ul.shape
    nb = n // ATT_BLOCK
    q_blocks = jnp.moveaxis(ql.reshape(b, nb, ATT_BLOCK, ATT_KV_HEADS, R, ATT_HEAD_DIM), 1, 0)
    o = lax.map(lambda qb: attend(qb, keys, vals), q_blocks)
    yl = jnp.moveaxis(o, 0, 1).reshape(b, n, nq) @ w_out
    yc = attend(qc, kc, vc).reshape(uc.shape[0], uc.shape[1], nq) @ w_out if ctx_out else None
    return yl, yc


def centred_pool_minus_self(v, window):
    b, n, ch = v.shape
    csum = jnp.concatenate([jnp.zeros((b, 1, ch), v.dtype), jnp.cumsum(v, axis=1)], axis=1)
    t = jnp.arange(n)
    hi = jnp.minimum(t + window // 2, n)
    lo = jnp.maximum(t - window // 2, 0)
    count = (hi - lo).astype(v.dtype)
    return (csum[:, hi] - csum[:, lo]) / count[None, :, None] - v


def pool_mixer(ul, uc, w_in, w_grp, scale, ctx_out):
    def mix(u):
        b, n, _ = u.shape
        v = (u @ w_in).astype(F32).reshape(b, n, len(POOL_WINDOWS), POOL_GROUP)
        pooled = jnp.stack([centred_pool_minus_self(v[:, :, g], w) for g, w in enumerate(POOL_WINDOWS)], axis=2)
        out = jnp.einsum('bngi,gio->bngo', pooled.astype(u.dtype), w_grp).reshape(b, n, D_MODEL)
        return out * scale

    yl = mix(ul)
    yc = mix(uc) if ctx_out else None
    return yl, yc


def linear_recurrence(a, bx, h0, reverse):
    first = -1 if reverse else 0
    bx = bx.at[:, first].add(a[:, first] * h0)

    def combine(e1, e2):
        a1, b1 = e1
        a2, b2 = e2
        return a1 * a2, a2 * b1 + b2

    _, h = lax.associative_scan(combine, (a, bx), axis=1, reverse=reverse)
    return h


def rglru_mixer(ul, uc, w_in, conv_w, conv_b, gate_w, gate_b, lam, w_out, ctx_out):
    def project(u):
        gu = u @ w_in
        xr = dwconv_centred(gu[..., LRU_WIDTH:], conv_w, conv_b).astype(F32)
        return gu[..., :LRU_WIDTH], xr

    def coeffs(xr, d):
        b, n, _ = xr.shape
        xb = xr.reshape(b, n, LRU_BLOCKS, LRU_BLOCK_W)
        pre = jnp.einsum('bnki,jkio->bnjko', xb, gate_w[d].astype(F32)).reshape(b, n, 2, LRU_WIDTH)
        gates = jax.nn.sigmoid(pre + gate_b[d].astype(F32))
        r, i = gates[:, :, 0], gates[:, :, 1]
        log_a = -LRU_C * r * jax.nn.softplus(-lam[d].astype(F32))
        bx = jnp.sqrt(-jnp.expm1(2.0 * log_a)) * (i * xr)
        return jnp.exp(log_a), bx

    gc_pre, xrc = project(uc)
    gl_pre, xrl = project(ul)
    zero = jnp.zeros((uc.shape[0], LRU_WIDTH), F32)
    hc_f = linear_recurrence(*coeffs(xrc, 0), zero, False)
    hc_b = linear_recurrence(*coeffs(xrc, 1), zero, True)
    hl_f = linear_recurrence(*coeffs(xrl, 0), hc_f[:, -1], False)
    hl_b = linear_recurrence(*coeffs(xrl, 1), hc_b[:, 0], True)
    yl = ((hl_f + hl_b).astype(ul.dtype) * jax.nn.gelu(gl_pre)) @ w_out
    yc = ((hc_f + hc_b).astype(uc.dtype) * jax.nn.gelu(gc_pre)) @ w_out if ctx_out else None
    return yl, yc


def swiglu(u, w_gate, w_up, w_down):
    return (jax.nn.silu(u @ w_gate) * (u @ w_up)) @ w_down


def moe_swiglu(u, w_router, w_gate, w_up, w_down):
    shape = u.shape
    xt = u.reshape(-1, shape[-1])
    T = xt.shape[0]
    logits = jnp.matmul(xt, w_router, preferred_element_type=F32)
    top_val, top_idx = lax.top_k(logits, TOP_K)
    gates = jax.nn.softmax(top_val, axis=-1)
    A = T * TOP_K
    e_flat = top_idx.reshape(-1)
    tok_flat = jnp.repeat(jnp.arange(T, dtype=jnp.int32), TOP_K)
    g_flat = gates.reshape(-1)
    order = jnp.argsort(e_flat)
    e_s, tok_s, g_s = e_flat[order], tok_flat[order], g_flat[order]
    counts = jnp.bincount(e_flat, length=N_EXPERTS)
    padded = (counts + MOE_BLOCK - 1) // MOE_BLOCK * MOE_BLOCK
    start = jnp.cumsum(counts) - counts
    pend = jnp.cumsum(padded)
    pstart = pend - padded
    dest = pstart[e_s] + jnp.arange(A) - start[e_s]
    n_blocks = -(-(A + N_EXPERTS * (MOE_BLOCK - 1)) // MOE_BLOCK)
    P = n_blocks * MOE_BLOCK
    buf_tok = jnp.full((P,), T, jnp.int32).at[dest].set(tok_s)
    buf_gate = jnp.zeros((P,), F32).at[dest].set(g_s)
    block_expert = jnp.minimum(jnp.searchsorted(pend, jnp.arange(n_blocks) * MOE_BLOCK, side='right'),
                               N_EXPERTS - 1)
    x_pad = jnp.concatenate([xt, jnp.zeros((1, xt.shape[1]), xt.dtype)], axis=0)
    xb = x_pad[buf_tok].reshape(n_blocks, MOE_BLOCK, -1)
    yb = lax.map(lambda a: swiglu(a[0], w_gate[a[1]], w_up[a[1]], w_down[a[1]]), (xb, block_expert))
    yb = yb.reshape(P, -1) * buf_gate[:, None].astype(yb.dtype)
    y = jax.ops.segment_sum(yb, buf_tok, num_segments=T + 1)[:T]
    return y.reshape(shape)


def setup_inputs(seed: int = 0) -> dict:
    key = jax.random.key(seed)
    keys = iter(jax.random.split(key, 48))
    D = D_MODEL
    n_ssm = len(range(0, DEPTH, N_MIXERS))
    n_att = len(range(1, DEPTH, N_MIXERS))
    n_pool = len(range(2, DEPTH, N_MIXERS))
    n_lru = len(range(3, DEPTH, N_MIXERS))
    n_dense = len(range(0, DEPTH, 2))
    n_moe = len(range(1, DEPTH, 2))

    def normal(shape, scale):
        return jax.random.normal(next(keys), shape, F32) * scale

    def gain(shape):
        return 1.0 + 0.02 * jax.random.normal(next(keys), shape, F32)

    x = normal((BATCH, SEQ, D), 1.0)
    c = normal((BATCH, D), 1.0)
    ctx = normal((BATCH, CTX_LEN, D), 1.0)
    c_ctx = normal((D,), 1.0)
    w_mod = normal((DEPTH, D, 6 * D), 0.5 * D ** -0.5)
    b_mod = normal((DEPTH, 6 * D), 0.02)
    norm_g = gain((DEPTH, 2, D))
    ssm_w_in = normal((n_ssm, D, SSM_IN_DIM), D ** -0.5)
    ssm_conv_w = normal((n_ssm, SSM_CONV, SSM_CONV_DIM), SSM_CONV ** -0.5)
    ssm_conv_b = normal((n_ssm, SSM_CONV_DIM), 0.02)
    dt0 = jnp.exp(jax.random.uniform(next(keys), (n_ssm, 2, SSM_HEADS), F32, math.log(1e-3), math.log(1e-1)))
    ssm_dt_bias = dt0 + jnp.log(-jnp.expm1(-dt0))
    ssm_a_log = jnp.log(jax.random.uniform(next(keys), (n_ssm, 2, SSM_HEADS), F32, 1.0, 16.0))
    ssm_d = gain((n_ssm, SSM_HEADS))
    ssm_norm_g = gain((n_ssm, SSM_INNER))
    ssm_w_out = normal((n_ssm, SSM_INNER, D), SSM_INNER ** -0.5)
    att_w_qkv = normal((n_att, D, (ATT_Q_HEADS + 2 * ATT_KV_HEADS) * ATT_HEAD_DIM), D ** -0.5)
    att_q_norm = gain((n_att, ATT_HEAD_DIM))
    att_k_norm = gain((n_att, ATT_HEAD_DIM))
    att_w_out = normal((n_att, ATT_Q_HEADS * ATT_HEAD_DIM, D), (ATT_Q_HEADS * ATT_HEAD_DIM) ** -0.5)
    pool_w_in = normal((n_pool, D, D), D ** -0.5)
    pool_w_grp = normal((n_pool, len(POOL_WINDOWS), POOL_GROUP, POOL_GROUP), POOL_GROUP ** -0.5)
    pool_scale = gain((n_pool, D))
    lru_w_in = normal((n_lru, D, 2 * LRU_WIDTH), D ** -0.5)
    lru_conv_w = normal((n_lru, LRU_CONV, LRU_WIDTH), LRU_CONV ** -0.5)
    lru_conv_b = normal((n_lru, LRU_WIDTH), 0.02)
    lru_gate_w = normal((n_lru, 2, 2, LRU_BLOCKS, LRU_BLOCK_W, LRU_BLOCK_W), LRU_BLOCK_W ** -0.5)
    lru_gate_b = normal((n_lru, 2, 2, LRU_WIDTH), 0.02)
    a0 = jax.random.uniform(next(keys), (n_lru, 2, LRU_WIDTH), F32, 0.9, 0.999) ** (1.0 / LRU_C)
    lru_lambda = jnp.log(a0) - jnp.log1p(-a0)
    lru_w_out = normal((n_lru, LRU_WIDTH, D), LRU_WIDTH ** -0.5)
    ffn_w_gate = normal((n_dense, D, FFN_HIDDEN), D ** -0.5)
    ffn_w_up = normal((n_dense, D, FFN_HIDDEN), D ** -0.5)
    ffn_w_down = normal((n_dense, FFN_HIDDEN, D), FFN_HIDDEN ** -0.5)
    moe_w_router = normal((n_moe, D, N_EXPERTS), D ** -0.5)
    moe_w_gate = normal((n_moe, N_EXPERTS, D, FFN_HIDDEN), D ** -0.5)
    moe_w_up = normal((n_moe, N_EXPERTS, D, FFN_HIDDEN), D ** -0.5)
    moe_w_down = normal((n_moe, N_EXPERTS, FFN_HIDDEN, D), FFN_HIDDEN ** -0.5)
    return {"x": x, "c": c, "ctx": ctx, "c_ctx": c_ctx, "w_mod": w_mod, "b_mod": b_mod, "norm_g": norm_g,
            "ssm_w_in": ssm_w_in, "ssm_conv_w": ssm_conv_w, "ssm_conv_b": ssm_conv_b,
            "ssm_dt_bias": ssm_dt_bias, "ssm_a_log": ssm_a_log, "ssm_d": ssm_d, "ssm_norm_g": ssm_norm_g,
            "ssm_w_out": ssm_w_out, "att_w_qkv": att_w_qkv, "att_q_norm": att_q_norm,
            "att_k_norm": att_k_norm, "att_w_out": att_w_out, "pool_w_in": pool_w_in,
            "pool_w_grp": pool_w_grp, "pool_scale": pool_scale, "lru_w_in": lru_w_in,
            "lru_conv_w": lru_conv_w, "lru_conv_b": lru_conv_b, "lru_gate_w": lru_gate_w,
            "lru_gate_b": lru_gate_b, "lru_lambda": lru_lambda, "lru_w_out": lru_w_out,
            "ffn_w_gate": ffn_w_gate, "ffn_w_up": ffn_w_up, "ffn_w_down": ffn_w_down,
            "moe_w_router": moe_w_router, "moe_w_gate": moe_w_gate, "moe_w_up": moe_w_up,
            "moe_w_down": moe_w_down}


def reference(x, c, ctx, c_ctx, w_mod, b_mod, norm_g, ssm_w_in, ssm_conv_w, ssm_conv_b, ssm_dt_bias,
              ssm_a_log, ssm_d, ssm_norm_g, ssm_w_out, att_w_qkv, att_q_norm, att_k_norm, att_w_out,
              pool_w_in, pool_w_grp, pool_scale, lru_w_in, lru_conv_w, lru_conv_b, lru_gate_w, lru_gate_b,
              lru_lambda, lru_w_out, ffn_w_gate, ffn_w_up, ffn_w_down, moe_w_router, moe_w_gate, moe_w_up,
              moe_w_down):
    rope_cos, rope_sin = axial_rope_tables(x.shape[1])
    s_lat = jax.nn.silu(c)
    s_ctx = jax.nn.silu(c_ctx)
    h, hc = x, ctx
    for i in range(DEPTH):
        last = i == DEPTH - 1
        mod_l = jnp.split((s_lat @ w_mod[i] + b_mod[i])[:, None, :], 6, axis=-1)
        mod_c = jnp.split((s_ctx @ w_mod[i] + b_mod[i])[None, None, :], 6, axis=-1)
        ul = modulate(h, norm_g[i, 0], mod_l[0], mod_l[1])
        uc = modulate(hc, norm_g[i, 0], mod_c[0], mod_c[1])
        kind, j = i % N_MIXERS, i // N_MIXERS
        if kind == 0:
            yl, yc = mamba2_mixer(ul, uc, ssm_w_in[j], ssm_conv_w[j], ssm_conv_b[j], ssm_dt_bias[j],
                                  ssm_a_log[j], ssm_d[j], ssm_norm_g[j], ssm_w_out[j], not last)
        elif kind == 1:
            yl, yc = gqa_mixer(ul, uc, att_w_qkv[j], att_q_norm[j], att_k_norm[j], att_w_out[j],
                               rope_cos, rope_sin, not last)
        elif kind == 2:
            yl, yc = pool_mixer(ul, uc, pool_w_in[j], pool_w_grp[j], pool_scale[j], not last)
        else:
            yl, yc = rglru_mixer(ul, uc, lru_w_in[j], lru_conv_w[j], lru_conv_b[j], lru_gate_w[j],
                                 lru_gate_b[j], lru_lambda[j], lru_w_out[j], not last)
        h = h + mod_l[2] * yl
        if not last:
            hc = hc + mod_c[2] * yc
        if i % 2 == 0:
            k = i // 2
            ffn = lambda u: swiglu(u, ffn_w_gate[k], ffn_w_up[k], ffn_w_down[k])
        else:
            k = i // 2
            ffn = lambda u: moe_swiglu(u, moe_w_router[k], moe_w_gate[k], moe_w_up[k], moe_w_down[k])
        h = h + mod_l[5] * ffn(modulate(h, norm_g[i, 1], mod_l[3], mod_l[4]))
        if not last:
            hc = hc + mod_c[5] * ffn(modulate(hc, norm_g[i, 1], mod_c[3], mod_c[4]))
    return h
```

```python
import functools
import math

import jax
import jax.numpy as jnp
from jax import lax
from jax.experimental import pallas as pl
from jax.experimental.pallas import tpu as pltpu

F32 = jnp.float32
BF16 = jnp.bfloat16
EPS = 1e-6

GRID_W = 64
SSM_HEAD_DIM = 64
SSM_GROUPS = 4
SSM_STATE = 128
SSM_CHUNK = 128
ATT_HEAD_DIM = 64
ATT_KV_HEADS = 4
ROPE_THETA = 10000.0
POOL_WINDOWS = (2, 4, 8, 16)
LRU_BLOCKS = 10
LRU_C = 8.0
TOP_K = 2
MOE_BLOCK = 512

LANE = 128
POOL_HALO = 64
VMEM_LIMIT = 56 * 1024 * 1024


def _cparams(*sem):
    return pltpu.CompilerParams(dimension_semantics=sem, vmem_limit_bytes=VMEM_LIMIT)


def _pick(n, candidates):
    for c in candidates:
        if n % c == 0:
            return c
    raise ValueError(f"no tile in {candidates} divides {n}")


def _sigmoid(x):
    return 1.0 / (1.0 + jnp.exp(-x))


def _silu(x):
    return x * _sigmoid(x)


def _split2(x):
    hi = x.astype(BF16)
    lo = (x - hi.astype(F32)).astype(BF16)
    return hi, lo


def _split3(x):
    x1 = x.astype(BF16)
    r = x - x1.astype(F32)
    x2 = r.astype(BF16)
    x3 = (r - x2.astype(F32)).astype(BF16)
    return x1, x2, x3


def _dot(a, b):
    return jnp.dot(a, b, preferred_element_type=F32)


def _dot_nt(a, b):
    return lax.dot_general(a, b, (((1,), (1,)), ((), ())), preferred_element_type=F32)


def _dot_tn(a, b):
    return lax.dot_general(a, b, (((0,), (0,)), ((), ())), preferred_element_type=F32)


def _dot_exact_lhs(a_bf16, x_f32):
    x1, x2, x3 = _split3(x_f32)
    return _dot(a_bf16, x1) + _dot(a_bf16, x2) + _dot(a_bf16, x3)


def _modulated(h, modl_ref, modc_ref, g_ref, k_shift, k_scale, row0, seq):
    tm = h.shape[0]
    row = row0 + lax.broadcasted_iota(jnp.int32, (tm, 1), 0)
    is_ctx = row >= seq
    shift = jnp.where(is_ctx, modc_ref[k_shift:k_shift + 1, :], modl_ref[k_shift:k_shift + 1, :])
    scale = jnp.where(is_ctx, modc_ref[k_scale:k_scale + 1, :], modl_ref[k_scale:k_scale + 1, :])
    ms = jnp.mean(h * h, axis=-1, keepdims=True)
    y = h * lax.rsqrt(ms + EPS) * g_ref[...]
    return y * (1.0 + scale) + shift


def _gate_rows(modl_ref, modc_ref, k_gate, tm, row0, seq):
    row = row0 + lax.broadcasted_iota(jnp.int32, (tm, 1), 0)
    return jnp.where(row >= seq, modc_ref[k_gate:k_gate + 1, :], modl_ref[k_gate:k_gate + 1, :])


def _mod_kernel(s_ref, w_ref, b_ref, o_ref):
    s = _silu(s_ref[...])
    o_ref[...] = _dot(s.astype(BF16), w_ref[...].astype(BF16)) + b_ref[...]


def _mod_vectors(s_rows, w_mod, b_mod):
    depth, d, n6 = w_mod.shape
    bp = s_rows.shape[0]
    tn = _pick(n6, (1536, 1024, 512, 256, 128))
    return pl.pallas_call(
        _mod_kernel,
        grid=(depth, n6 // tn),
        in_specs=[pl.BlockSpec((bp, d), lambda l, j: (0, 0)),
                  pl.BlockSpec((None, d, tn), lambda l, j: (l, 0, j)),
                  pl.BlockSpec((None, 1, tn), lambda l, j: (l, 0, j))],
        out_specs=pl.BlockSpec((None, bp, tn), lambda l, j: (l, 0, j)),
        out_shape=jax.ShapeDtypeStruct((depth, bp, n6), F32),
        compiler_params=_cparams("parallel", "parallel"),
        name="mod_vectors",
    )(s_rows, w_mod, b_mod.reshape(depth, 1, n6))


def _seg_conv(x, w_ref, seq):
    nt = x.shape[0]
    row = lax.broadcasted_iota(jnp.int32, (nt, 1), 0)
    in_ctx = row >= seq
    pos = jnp.where(in_ctx, row - seq, row)
    seglen = jnp.where(in_ctx, nt - seq, seq)
    out = x * w_ref[2:3, :]
    for k, off in ((0, -2), (1, -1), (3, 1)):
        shifted = pltpu.roll(x, (-off) % nt, axis=0)
        valid = (pos + off >= 0) & (pos + off < seglen)
        out = out + jnp.where(valid, shifted, 0.0) * w_ref[k:k + 1, :]
    return out


def _epi_plain(acc, extra, row0, seq):
    return acc


def _epi_softplus(acc, extra, row0, seq):
    (b_ref,) = extra
    x = acc + b_ref[...]
    return jnp.maximum(x, 0.0) + jnp.log1p(jnp.exp(-jnp.abs(x)))


def _epi_conv_silu(acc, extra, row0, seq):
    w_ref, b_ref = extra
    return _silu(_seg_conv(acc, w_ref, seq) + b_ref[...])


def _epi_conv_bias(acc, extra, row0, seq):
    w_ref, b_ref = extra
    return _seg_conv(acc, w_ref, seq) + b_ref[...]


def _epi_gelu(acc, extra, row0, seq):
    return jax.nn.gelu(acc)


def _epi_qk(acc, extra, row0, seq, *, out_scale):
    g_ref, gain_ref, cos_ref, sin_ref = extra
    tn = acc.shape[1]
    ms = _dot((acc * acc).astype(BF16), g_ref[...])
    xn = acc * lax.rsqrt(ms + EPS) * gain_ref[...]
    lane = lax.broadcasted_iota(jnp.int32, (1, tn), 1)
    half = ATT_HEAD_DIM // 2
    second = (lane % ATT_HEAD_DIM) >= half
    partner = jnp.where(second, pltpu.roll(xn, half, axis=1), pltpu.roll(xn, tn - half, axis=1))
    reps = tn // LANE
    cos = jnp.concatenate([cos_ref[...]] * reps, axis=1) if reps > 1 else cos_ref[...]
    sin = jnp.concatenate([sin_ref[...]] * reps, axis=1) if reps > 1 else sin_ref[...]
    return (xn * cos + partner * sin) * out_scale


def _proj_kernel(*refs, epi, n_extra, k_shift, k_scale, seq):
    h_ref, modl_ref, modc_ref, g_ref, w_ref = refs[:5]
    extra = refs[5:5 + n_extra]
    o_ref = refs[5 + n_extra]
    u_scr = refs[6 + n_extra]
    tm = h_ref.shape[0]
    row0 = pl.program_id(1) * tm

    @pl.when(pl.program_id(2) == 0)
    def _():
        u_scr[...] = _modulated(h_ref[...], modl_ref, modc_ref, g_ref, k_shift, k_scale, row0, seq).astype(BF16)

    acc = _dot(u_scr[...], w_ref[...])
    o_ref[...] = epi(acc, extra, row0, seq).astype(o_ref.dtype)


def _proj(hh, modl, modc, g, w, *, epi, extra=(), extra_specs=(), tm, tn, out_dtype, seq, k_shift, k_scale,
          name):
    b, nt, d = hh.shape
    n = w.shape[1]
    grid = (b, nt // tm, n // tn)
    in_specs = [pl.BlockSpec((None, tm, d), lambda bb, i, j: (bb, i, 0)),
                pl.BlockSpec((None, 6, d), lambda bb, i, j: (bb, 0, 0)),
                pl.BlockSpec((None, 6, d), lambda bb, i, j: (0, 0, 0)),
                pl.BlockSpec((1, d), lambda bb, i, j: (0, 0)),
                pl.BlockSpec((d, tn), lambda bb, i, j: (0, j))] + list(extra_specs)
    kern = functools.partial(_proj_kernel, epi=epi, n_extra=len(extra), k_shift=k_shift, k_scale=k_scale, seq=seq)
    return pl.pallas_call(
        kern, grid=grid, in_specs=in_specs,
        out_specs=pl.BlockSpec((None, tm, tn), lambda bb, i, j: (bb, i, j)),
        out_shape=jax.ShapeDtypeStruct((b, nt, n), out_dtype),
        scratch_shapes=[pltpu.VMEM((tm, d), BF16)],
        compiler_params=_cparams("parallel", "parallel", "arbitrary"),
        name=name,
    )(hh, modl, modc, g, w, *extra)


def _col_spec(rows, tn):
    return pl.BlockSpec((rows, tn), lambda bb, i, j: (0, j))


def _pro_plain(ins):
    (a_ref,) = ins
    return a_ref[...]


def _pro_ssd(ins):
    yf_ref, yb_ref, z_ref, ng_ref = ins
    y = (yf_ref[...].astype(F32) + yb_ref[...].astype(F32)) * _silu(z_ref[...].astype(F32))
    ms = jnp.mean(y * y, axis=-1, keepdims=True)
    return (y * lax.rsqrt(ms + EPS) * ng_ref[...]).astype(BF16)


def _pro_lru(ins):
    hf_ref, hb_ref, gg_ref = ins
    hsum = (hf_ref[...].astype(F32) + hb_ref[...].astype(F32))
    return (hsum * gg_ref[...].astype(F32)).astype(BF16)


def _mmres_kernel(*refs, pro, n_in, k_gate, seq):
    ins = refs[:n_in]
    w_ref, h_ref, modl_ref, modc_ref, o_ref = refs[n_in:n_in + 5]
    tm = h_ref.shape[0]
    row0 = pl.program_id(1) * tm
    y = _dot(pro(ins), w_ref[...])
    gate = _gate_rows(modl_ref, modc_ref, k_gate, tm, row0, seq)
    o_ref[...] = h_ref[...] + gate * y


def _mmres(ins, in_widths, w, hh, modl, modc, *, pro, tm, k_gate, seq, name, rows=None, row_ins=()):
    b, nt, d = hh.shape
    rows = nt if rows is None else rows
    k = w.shape[0]
    grid = (b, rows // tm)
    in_specs = [pl.BlockSpec((None, tm, wd), lambda bb, i: (bb, i, 0)) for wd in in_widths]
    in_specs += [pl.BlockSpec((1, r.shape[1]), lambda bb, i: (0, 0)) for r in row_ins]
    in_specs += [pl.BlockSpec((k, d), lambda bb, i: (0, 0)),
                 pl.BlockSpec((None, tm, d), lambda bb, i: (bb, i, 0)),
                 pl.BlockSpec((None, 6, d), lambda bb, i: (bb, 0, 0)),
                 pl.BlockSpec((None, 6, d), lambda bb, i: (0, 0, 0))]
    kern = functools.partial(_mmres_kernel, pro=pro, n_in=len(ins) + len(row_ins), k_gate=k_gate, seq=seq)
    return pl.pallas_call(
        kern, grid=grid, in_specs=in_specs,
        out_specs=pl.BlockSpec((None, tm, d), lambda bb, i: (bb, i, 0)),
        out_shape=jax.ShapeDtypeStruct((b, rows, d), F32),
        compiler_params=_cparams("parallel", "parallel"),
        name=name,
    )(*ins, *row_ins, w, hh, modl, modc)


def _ffn_kernel(h_ref, modl_ref, modc_ref, g_ref, wg_ref, wu_ref, wd_ref, o_ref, u_scr, acc_scr, *, seq):
    tm = h_ref.shape[0]
    row0 = pl.program_id(1) * tm
    j = pl.program_id(2)

    @pl.when(j == 0)
    def _():
        u_scr[...] = _modulated(h_ref[...], modl_ref, modc_ref, g_ref, 3, 4, row0, seq).astype(BF16)
        acc_scr[...] = jnp.zeros_like(acc_scr)

    u = u_scr[...]
    a = (_silu(_dot(u, wg_ref[...])) * _dot(u, wu_ref[...])).astype(BF16)
    acc_scr[...] += _dot(a, wd_ref[...])

    @pl.when(j == pl.num_programs(2) - 1)
    def _():
        gate = _gate_rows(modl_ref, modc_ref, 5, tm, row0, seq)
        o_ref[...] = h_ref[...] + gate * acc_scr[...]


def _dense_ffn(hh, modl, modc, g, wg, wu, wd, *, tm, th, seq):
    b, nt, d = hh.shape
    hid = wg.shape[1]
    grid = (b, nt // tm, hid // th)
    return pl.pallas_call(
        functools.partial(_ffn_kernel, seq=seq),
        grid=grid,
        in_specs=[pl.BlockSpec((None, tm, d), lambda bb, i, j: (bb, i, 0)),
                  pl.BlockSpec((None, 6, d), lambda bb, i, j: (bb, 0, 0)),
                  pl.BlockSpec((None, 6, d), lambda bb, i, j: (0, 0, 0)),
                  pl.BlockSpec((1, d), lambda bb, i, j: (0, 0)),
                  pl.BlockSpec((d, th), lambda bb, i, j: (0, j)),
                  pl.BlockSpec((d, th), lambda bb, i, j: (0, j)),
                  pl.BlockSpec((th, d), lambda bb, i, j: (j, 0))],
        out_specs=pl.BlockSpec((None, tm, d), lambda bb, i, j: (bb, i, 0)),
        out_shape=jax.ShapeDtypeStruct((b, nt, d), F32),
        scratch_shapes=[pltpu.VMEM((tm, d), BF16), pltpu.VMEM((tm, d), F32)],
        compiler_params=_cparams("parallel", "parallel", "arbitrary"),
        name="dense_ffn",
    )(hh, modl, modc, g, wg, wu, wd)


def _sel01(x_f32, e_bf16):
    x1, x2, x3 = _split3(x_f32)
    return _dot(x1, e_bf16) + _dot(x2, e_bf16) + _dot(x3, e_bf16)


def _ssd_chunk(xbc, dt, nega, dskip, expand, lane0, state_ref, o_ref, *, reverse, add_skip):
    q = xbc.shape[0]
    g_n = SSM_GROUPS * SSM_STATE
    inner = xbc.shape[1] - 2 * g_n
    heads = inner // SSM_HEAD_DIM
    rp = inner // SSM_GROUPS
    r_heads = heads // SSM_GROUPS

    rows = lax.broadcasted_iota(jnp.int32, (q, q), 0)
    cols = lax.broadcasted_iota(jnp.int32, (q, q), 1)
    keep = (rows <= cols) if reverse else (rows >= cols)
    tri = keep.astype(BF16)

    la = dt * nega
    ac = _dot_exact_lhs(tri, la)
    ac_t = jnp.transpose(ac)
    last = 0 if reverse else q - 1
    ac_end = ac[last:last + 1, :]

    x = xbc[:, :inner].astype(F32)
    xdt = x * _sel01(dt, expand)
    to_end = jnp.exp(_sel01(ac_end - ac, expand))
    from_start = jnp.exp(_sel01(ac, expand))
    chunk_decay = from_start[last:last + 1, :]
    xdt_b = xdt.astype(BF16)
    xend_b = (xdt * to_end).astype(BF16)

    for g in range(SSM_GROUPS):
        bg = xbc[:, inner + g * SSM_STATE: inner + (g + 1) * SSM_STATE]
        cg = xbc[:, inner + g_n + g * SSM_STATE: inner + g_n + (g + 1) * SSM_STATE]
        cb = _dot_nt(cg, bg)
        pieces = []
        for r in range(r_heads):
            hd = g * r_heads + r
            ln = lane0 + hd
            diff = ac[:, ln:ln + 1] - ac_t[ln:ln + 1, :]
            dec = jnp.where(keep, jnp.exp(jnp.minimum(diff, 0.0)), 0.0)
            m = (cb * dec).astype(BF16)
            pieces.append(_dot(m, xdt_b[:, hd * SSM_HEAD_DIM:(hd + 1) * SSM_HEAD_DIM]))
        y_diag = jnp.concatenate(pieces, axis=1)
        sl = slice(g * rp, (g + 1) * rp)
        h_in = state_ref[g]
        y_off = _dot(cg, h_in.astype(BF16)) * from_start[:, sl]
        states = _dot_tn(bg, xend_b[:, sl])
        state_ref[g] = h_in * chunk_decay[:, sl] + states
        y = y_diag + y_off
        if add_skip:
            y = y + x[:, sl] * dskip[:, sl]
        o_ref[:, sl] = y.astype(o_ref.dtype)


def _ssd_kernel(xf_ref, xb_ref, dtf_ref, dtb_ref, nega_ref, dskip_ref, exp_ref, yf_ref, yb_ref, st_ref, *, heads):
    @pl.when(pl.program_id(1) == 0)
    def _():
        st_ref[...] = jnp.zeros_like(st_ref)

    _ssd_chunk(xf_ref[...], dtf_ref[...], nega_ref[...], dskip_ref[...], exp_ref[0], 0, st_ref.at[0], yf_ref,
               reverse=False, add_skip=True)
    _ssd_chunk(xb_ref[...], dtb_ref[...], nega_ref[...], dskip_ref[...], exp_ref[1], heads, st_ref.at[1], yb_ref,
               reverse=True, add_skip=False)


def _ssd(xbc, dt, nega, dskip, expand, *, seq, heads):
    b, nt, width = xbc.shape
    q = SSM_CHUNK
    inner = heads * SSM_HEAD_DIM
    nch = nt // q
    ncl = seq // q

    def fwd(bb, j):
        return (bb, (j + ncl) % nch, 0)

    def bwd(bb, j):
        return (bb, nch - 1 - j, 0)

    return pl.pallas_call(
        functools.partial(_ssd_kernel, heads=heads),
        grid=(b, nch),
        in_specs=[pl.BlockSpec((None, q, width), fwd),
                  pl.BlockSpec((None, q, width), bwd),
                  pl.BlockSpec((None, q, LANE), fwd),
                  pl.BlockSpec((None, q, LANE), bwd),
                  pl.BlockSpec((1, LANE), lambda bb, j: (0, 0)),
                  pl.BlockSpec((1, inner), lambda bb, j: (0, 0)),
                  pl.BlockSpec((2, LANE, inner), lambda bb, j: (0, 0, 0))],
        out_specs=[pl.BlockSpec((None, q, inner), fwd),
                   pl.BlockSpec((None, q, inner), bwd)],
        out_shape=[jax.ShapeDtypeStruct((b, nt, inner), BF16)] * 2,
        scratch_shapes=[pltpu.VMEM((2, SSM_GROUPS, SSM_STATE, inner // SSM_GROUPS), F32)],
        compiler_params=_cparams("parallel", "arbitrary"),
        name="ssd_scan",
    )(xbc, xbc, dt, dt, nega, dskip, expand)


def _attn_heads(q, k, v, o_ref):
    hd = ATT_HEAD_DIM
    n_q = q.shape[1] // hd
    rep = n_q // ATT_KV_HEADS
    tq = q.shape[0]
    for g in range(ATT_KV_HEADS):
        kg = k[:, g * hd:(g + 1) * hd]
        vg = v[:, g * hd:(g + 1) * hd]
        qg = jnp.concatenate([q[:, (g * rep + r) * hd:(g * rep + r + 1) * hd] for r in range(rep)], axis=0)
        s = _dot_nt(qg, kg)
        m = jnp.max(s, axis=-1, keepdims=True)
        p = jnp.exp(s - m)
        l = jnp.sum(p, axis=-1, keepdims=True)
        o = _dot(p.astype(BF16), vg) / l
        og = jnp.concatenate([o[r * tq:(r + 1) * tq, :] for r in range(rep)], axis=1)
        o_ref[:, g * rep * hd:(g + 1) * rep * hd] = og.astype(o_ref.dtype)


def _attn_kernel(q_ref, k_ref, v_ref, o_ref, *, seq):
    tq = q_ref.shape[0]
    is_lat = pl.program_id(1) * tq < seq

    @pl.when(is_lat)
    def _():
        _attn_heads(q_ref[...], k_ref[...], v_ref[...], o_ref)

    @pl.when(jnp.logical_not(is_lat))
    def _():
        _attn_heads(q_ref[...], k_ref[seq:, :], v_ref[seq:, :], o_ref)


def _attention(q, k, v, *, seq, tq):
    b, nt, dq = q.shape
    dk = k.shape[2]
    return pl.pallas_call(
        functools.partial(_attn_kernel, seq=seq),
        grid=(b, nt // tq),
        in_specs=[pl.BlockSpec((None, tq, dq), lambda bb, i: (bb, i, 0)),
                  pl.BlockSpec((None, nt, dk), lambda bb, i: (bb, 0, 0)),
                  pl.BlockSpec((None, nt, dk), lambda bb, i: (bb, 0, 0))],
        out_specs=pl.BlockSpec((None, tq, dq), lambda bb, i: (bb, i, 0)),
        out_shape=jax.ShapeDtypeStruct((b, nt, dq), BF16),
        compiler_params=_cparams("parallel", "parallel"),
        name="gqa_attention",
    )(q, k, v)


def _pool_kernel(v_ref, vp_ref, vn_ref, h_ref, modl_ref, modc_ref, wg_ref, sc_ref, o_ref, *, seq, nt):
    tm = v_ref.shape[0]
    halo = vp_ref.shape[0]
    i = pl.program_id(1)
    row0 = i * tm
    v = v_ref[...]
    vext = jnp.concatenate([vp_ref[...], v, vn_ref[...]], axis=0)
    vhi, vlo = _split2(vext)
    t = row0 + lax.broadcasted_iota(jnp.int32, (tm, 1), 0)
    s = row0 - halo + lax.broadcasted_iota(jnp.int32, (1, tm + 2 * halo), 1)
    in_ctx = t >= seq
    seg_lo = jnp.where(in_ctx, seq, 0)
    seg_hi = jnp.where(in_ctx, nt, seq)
    grp = v.shape[1] // len(POOL_WINDOWS)
    outs = []
    for gi, win in enumerate(POOL_WINDOWS):
        lo = jnp.maximum(t - win // 2, seg_lo)
        hi = jnp.minimum(t + win // 2, seg_hi)
        band = ((s >= lo) & (s < hi)).astype(BF16)
        sl = slice(gi * grp, (gi + 1) * grp)
        summed = _dot(band, vhi[:, sl]) + _dot(band, vlo[:, sl])
        pooled = summed / (hi - lo).astype(F32) - v[:, sl]
        outs.append(_dot(pooled.astype(BF16), wg_ref[gi]))
    y = jnp.concatenate(outs, axis=1) * sc_ref[...]
    gate = _gate_rows(modl_ref, modc_ref, 2, tm, row0, seq)
    o_ref[...] = h_ref[...] + gate * y


def _pool(v, hh, modl, modc, w_grp, scale, *, seq, tm):
    b, nt, d = hh.shape
    halo = POOL_HALO
    per = tm // halo
    nhb = nt // halo
    grp = w_grp.shape[1]
    return pl.pallas_call(
        functools.partial(_pool_kernel, seq=seq, nt=nt),
        grid=(b, nt // tm),
        in_specs=[pl.BlockSpec((None, tm, d), lambda bb, i: (bb, i, 0)),
                  pl.BlockSpec((None, halo, d), lambda bb, i: (bb, jnp.maximum(i * per - 1, 0), 0)),
                  pl.BlockSpec((None, halo, d), lambda bb, i: (bb, jnp.minimum((i + 1) * per, nhb - 1), 0)),
                  pl.BlockSpec((None, tm, d), lambda bb, i: (bb, i, 0)),
                  pl.BlockSpec((None, 6, d), lambda bb, i: (bb, 0, 0)),
                  pl.BlockSpec((None, 6, d), lambda bb, i: (0, 0, 0)),
                  pl.BlockSpec((len(POOL_WINDOWS), grp, grp), lambda bb, i: (0, 0, 0)),
                  pl.BlockSpec((1, d), lambda bb, i: (0, 0))],
        out_specs=pl.BlockSpec((None, tm, d), lambda bb, i: (bb, i, 0)),
        out_shape=jax.ShapeDtypeStruct((b, nt, d), F32),
        compiler_params=_cparams("parallel", "parallel"),
        name="pool_mixer",
    )(v, v, v, hh, modl, modc, w_grp, scale)


def _lru_kernel(x_ref, w_ref, b_ref, lam_ref, o_ref, a_scr, bx_scr, h_scr, carry_scr):
    d_dir = pl.program_id(1)
    tc, width = x_ref.shape
    bw = width // LRU_BLOCKS

    @pl.when(pl.program_id(2) == 0)
    def _():
        carry_scr[...] = jnp.zeros_like(carry_scr)

    xb = x_ref[...]
    xr = xb.astype(F32)
    r_parts, i_parts = [], []
    for k in range(LRU_BLOCKS):
        pre = _dot(xb[:, k * bw:(k + 1) * bw], w_ref[k])
        r_parts.append(pre[:, :bw])
        i_parts.append(pre[:, bw:])
    r_gate = _sigmoid(jnp.concatenate(r_parts, axis=1) + b_ref[0:1, :])
    i_gate = _sigmoid(jnp.concatenate(i_parts, axis=1) + b_ref[1:2, :])
    neg_lam = -lam_ref[...]
    softplus = jnp.maximum(neg_lam, 0.0) + jnp.log1p(jnp.exp(-jnp.abs(neg_lam)))
    log_a = (-LRU_C) * r_gate * softplus
    a = jnp.exp(log_a)
    a_scr[...] = a
    bx_scr[...] = jnp.sqrt(-jnp.tanh(log_a) * (a * a + 1.0)) * (i_gate * xr)

    def step(t, h):
        h = a_scr[pl.ds(t, 1), :] * h + bx_scr[pl.ds(t, 1), :]
        h_scr[pl.ds(t, 1), :] = h
        return h

    @pl.when(d_dir == 0)
    def _():
        carry_scr[...] = lax.fori_loop(0, tc, step, carry_scr[...], unroll=8)

    @pl.when(d_dir == 1)
    def _():
        carry_scr[...] = lax.fori_loop(0, tc, lambda t, h: step(tc - 1 - t, h), carry_scr[...], unroll=8)

    o_ref[...] = h_scr[...].astype(o_ref.dtype)


def _lru(xr, w_cat, bias, lam, *, seq, tc):
    b, nt, width = xr.shape
    nc = nt // tc
    nl = seq // tc

    def chunk(d, j):
        return d * (nc - 1 - j) + (1 - d) * ((j + nl) % nc)

    return pl.pallas_call(
        _lru_kernel,
        grid=(b, 2, nc),
        in_specs=[pl.BlockSpec((None, tc, width), lambda bb, d, j: (bb, chunk(d, j), 0)),
                  pl.BlockSpec((None,) + w_cat.shape[1:], lambda bb, d, j: (d, 0, 0, 0)),
                  pl.BlockSpec((None, 2, width), lambda bb, d, j: (d, 0, 0)),
                  pl.BlockSpec((None, 1, width), lambda bb, d, j: (d, 0, 0))],
        out_specs=pl.BlockSpec((None, None, tc, width), lambda bb, d, j: (d, bb, chunk(d, j), 0)),
        out_shape=jax.ShapeDtypeStruct((2, b, nt, width), BF16),
        scratch_shapes=[pltpu.VMEM((tc, width), F32), pltpu.VMEM((tc, width), F32), pltpu.VMEM((tc, width), F32),
                        pltpu.VMEM((1, width), F32)],
        compiler_params=_cparams("parallel", "arbitrary", "arbitrary"),
        name="rglru_scan",
    )(xr, w_cat, bias, lam)


def _router_kernel(h_ref, modl_ref, modc_ref, g_ref, whi_ref, wlo_ref, u_ref, meta_ref, cnt_ref, carry_scr, *,
                   seq, n_exp):
    tm = h_ref.shape[0]
    row0 = pl.program_id(1) * tm
    first = (pl.program_id(0) == 0) & (pl.program_id(1) == 0)

    @pl.when(first)
    def _():
        carry_scr[...] = jnp.zeros_like(carry_scr)

    u = _modulated(h_ref[...], modl_ref, modc_ref, g_ref, 3, 4, row0, seq)
    u_ref[...] = u
    uhi, ulo = _split2(u)
    logits = _dot(uhi, whi_ref[...]) + _dot(ulo, whi_ref[...]) + _dot(uhi, wlo_ref[...])
    lane = lax.broadcasted_iota(jnp.int32, (tm, LANE), 1)
    neg = jnp.float32(-jnp.inf)
    logits = jnp.where(lane < n_exp, logits, neg)
    v0 = jnp.max(logits, axis=-1, keepdims=True)
    lane_f = lane.astype(F32)
    i0 = jnp.min(jnp.where(logits == v0, lane_f, float(LANE)), axis=-1, keepdims=True)
    rest = jnp.where(lane_f == i0, neg, logits)
    v1 = jnp.max(rest, axis=-1, keepdims=True)
    i1 = jnp.min(jnp.where(rest == v1, lane_f, float(LANE)), axis=-1, keepdims=True)
    g0 = 1.0 / (1.0 + jnp.exp(v1 - v0))
    g1 = 1.0 - g0
    sel0 = lane_f == i0
    sel1 = lane_f == i1
    onehot = (sel0 | sel1).astype(BF16)
    rr = lax.broadcasted_iota(jnp.int32, (tm, tm), 0)
    cc = lax.broadcasted_iota(jnp.int32, (tm, tm), 1)
    before = (cc < rr).astype(BF16)
    prefix = _dot(before, onehot) + carry_scr[...]
    r0 = jnp.sum(jnp.where(sel0, prefix, 0.0), axis=-1, keepdims=True)
    r1 = jnp.sum(jnp.where(sel1, prefix, 0.0), axis=-1, keepdims=True)
    carry_scr[...] += jnp.sum(onehot.astype(F32), axis=0, keepdims=True)
    cnt_ref[...] = carry_scr[...]
    meta = jnp.where(lane == 0, i0, 0.0)
    meta = jnp.where(lane == 1, i1, meta)
    meta = jnp.where(lane == 2, g0, meta)
    meta = jnp.where(lane == 3, g1, meta)
    meta = jnp.where(lane == 4, r0, meta)
    meta = jnp.where(lane == 5, r1, meta)
    meta_ref[...] = meta


def _router(hh, modl, modc, g, whi, wlo, *, rows, tm, seq, n_exp):
    b, nt, d = hh.shape
    return pl.pallas_call(
        functools.partial(_router_kernel, seq=seq, n_exp=n_exp),
        grid=(b, rows // tm),
        in_specs=[pl.BlockSpec((None, tm, d), lambda bb, i: (bb, i, 0)),
                  pl.BlockSpec((None, 6, d), lambda bb, i: (bb, 0, 0)),
                  pl.BlockSpec((None, 6, d), lambda bb, i: (0, 0, 0)),
                  pl.BlockSpec((1, d), lambda bb, i: (0, 0)),
                  pl.BlockSpec((d, LANE), lambda bb, i: (0, 0)),
                  pl.BlockSpec((d, LANE), lambda bb, i: (0, 0))],
        out_specs=[pl.BlockSpec((None, tm, d), lambda bb, i: (bb, i, 0)),
                   pl.BlockSpec((None, tm, LANE), lambda bb, i: (bb, i, 0)),
                   pl.BlockSpec((1, LANE), lambda bb, i: (0, 0))],
        out_shape=[jax.ShapeDtypeStruct((b, rows, d), F32),
                   jax.ShapeDtypeStruct((b, rows, LANE), F32),
                   jax.ShapeDtypeStruct((1, LANE), F32)],
        scratch_shapes=[pltpu.VMEM((1, LANE), F32)],
        compiler_params=_cparams("arbitrary", "arbitrary"),
        name="moe_router",
    )(hh, modl, modc, g, whi, wlo)


def _dispatch_kernel(dest_ref, u_hbm, xg_in, xg_hbm, sem, *, tm, rows_per_b):
    del xg_in
    bb = pl.program_id(0)
    i = pl.program_id(1)
    base = (bb * rows_per_b + i * tm) * TOP_K

    def copy(r, k):
        src = u_hbm.at[bb, pl.ds(i * tm + r, 1)]
        dst = xg_hbm.at[pl.ds(dest_ref[base + r * TOP_K + k], 1)]
        return pltpu.make_async_copy(src, dst, sem)

    def issue(r, c):
        for k in range(TOP_K):
            copy(r, k).start()
        return c

    def drain(r, c):
        for k in range(TOP_K):
            copy(r, k).wait()
        return c

    lax.fori_loop(0, tm, issue, 0)
    lax.fori_loop(0, tm, drain, 0)


def _dispatch(dest, u, xg0, *, tm):
    b, rows, d = u.shape
    grid_spec = pltpu.PrefetchScalarGridSpec(
        num_scalar_prefetch=1, grid=(b, rows // tm),
        in_specs=[pl.BlockSpec(memory_space=pl.ANY), pl.BlockSpec(memory_space=pl.ANY)],
        out_specs=pl.BlockSpec(memory_space=pl.ANY),
        scratch_shapes=[pltpu.SemaphoreType.DMA(())])
    return pl.pallas_call(
        functools.partial(_dispatch_kernel, tm=tm, rows_per_b=rows),
        grid_spec=grid_spec,
        out_shape=jax.ShapeDtypeStruct(xg0.shape, xg0.dtype),
        input_output_aliases={2: 0},
        compiler_params=_cparams("arbitrary", "arbitrary"),
        name="moe_dispatch",
    )(dest, u, xg0)


def _expert_kernel(be_ref, nu_ref, x_ref, wg_ref, wu_ref, wd_ref, y_ref, xb_scr, acc_scr):
    blk = pl.program_id(0)
    j = pl.program_id(1)
    used = blk < nu_ref[0]

    @pl.when(used & (j == 0))
    def _():
        xb_scr[...] = x_ref[...].astype(BF16)
        acc_scr[...] = jnp.zeros_like(acc_scr)

    @pl.when(used)
    def _():
        x = xb_scr[...]
        a = (_silu(_dot(x, wg_ref[...])) * _dot(x, wu_ref[...])).astype(BF16)
        acc_scr[...] += _dot(a, wd_ref[...])

    @pl.when(used & (j == pl.num_programs(1) - 1))
    def _():
        y_ref[...] = acc_scr[...]

    @pl.when(jnp.logical_not(used) & (j == pl.num_programs(1) - 1))
    def _():
        y_ref[...] = jnp.zeros_like(y_ref)


def _experts(block_expert, n_used, xg, wg, wu, wd, *, th):
    p, d = xg.shape
    hid = wg.shape[2]
    nb = p // MOE_BLOCK
    nj = hid // th

    def blk_eff(blk, nu):
        return jnp.minimum(blk, nu[0] - 1)

    def j_eff(blk, j, nu):
        return jnp.where(blk < nu[0], j, nj - 1)

    grid_spec = pltpu.PrefetchScalarGridSpec(
        num_scalar_prefetch=2, grid=(nb, nj),
        in_specs=[pl.BlockSpec((MOE_BLOCK, d), lambda blk, j, be, nu: (blk_eff(blk, nu), 0)),
                  pl.BlockSpec((None, d, th), lambda blk, j, be, nu: (be[blk_eff(blk, nu)], 0, j_eff(blk, j, nu))),
                  pl.BlockSpec((None, d, th), lambda blk, j, be, nu: (be[blk_eff(blk, nu)], 0, j_eff(blk, j, nu))),
                  pl.BlockSpec((None, th, d), lambda blk, j, be, nu: (be[blk_eff(blk, nu)], j_eff(blk, j, nu), 0))],
        out_specs=pl.BlockSpec((MOE_BLOCK, d), lambda blk, j, be, nu: (blk, 0)),
        scratch_shapes=[pltpu.VMEM((MOE_BLOCK, d), BF16), pltpu.VMEM((MOE_BLOCK, d), F32)])
    return pl.pallas_call(
        _expert_kernel, grid_spec=grid_spec,
        out_shape=jax.ShapeDtypeStruct((p, d), F32),
        compiler_params=_cparams("arbitrary", "arbitrary"),
        name="moe_experts",
    )(block_expert, n_used, xg, wg, wu, wd)


def _combine_kernel(dest_ref, yg_hbm, meta_ref, h_ref, modl_ref, modc_ref, o_ref, y0_scr, y1_scr, sem, *, tm,
                    rows_per_b, seq):
    bb = pl.program_id(0)
    i = pl.program_id(1)
    base = (bb * rows_per_b + i * tm) * TOP_K
    bufs = (y0_scr, y1_scr)

    def copy(r, k):
        src = yg_hbm.at[pl.ds(dest_ref[base + r * TOP_K + k], 1)]
        return pltpu.make_async_copy(src, bufs[k].at[pl.ds(r, 1)], sem)

    def issue(r, c):
        for k in range(TOP_K):
            copy(r, k).start()
        return c

    def drain(r, c):
        for k in range(TOP_K):
            copy(r, k).wait()
        return c

    lax.fori_loop(0, tm, issue, 0)
    lax.fori_loop(0, tm, drain, 0)
    meta = meta_ref[...]
    y = meta[:, 2:3] * y0_scr[...] + meta[:, 3:4] * y1_scr[...]
    gate = _gate_rows(modl_ref, modc_ref, 5, tm, i * tm, seq)
    o_ref[...] = h_ref[...] + gate * y


def _combine(dest, yg, meta, hh, modl, modc, *, tm, seq):
    b, rows, _ = meta.shape
    d = hh.shape[2]
    grid_spec = pltpu.PrefetchScalarGridSpec(
        num_scalar_prefetch=1, grid=(b, rows // tm),
        in_specs=[pl.BlockSpec(memory_space=pl.ANY),
                  pl.BlockSpec((None, tm, LANE), lambda bb, i, ds: (bb, i, 0)),
                  pl.BlockSpec((None, tm, d), lambda bb, i, ds: (bb, i, 0)),
                  pl.BlockSpec((None, 6, d), lambda bb, i, ds: (bb, 0, 0)),
                  pl.BlockSpec((None, 6, d), lambda bb, i, ds: (0, 0, 0))],
        out_specs=pl.BlockSpec((None, tm, d), lambda bb, i, ds: (bb, i, 0)),
        scratch_shapes=[pltpu.VMEM((tm, d), F32), pltpu.VMEM((tm, d), F32), pltpu.SemaphoreType.DMA(())])
    return pl.pallas_call(
        functools.partial(_combine_kernel, tm=tm, rows_per_b=rows, seq=seq),
        grid_spec=grid_spec,
        out_shape=jax.ShapeDtypeStruct((b, rows, d), F32),
        compiler_params=_cparams("arbitrary", "arbitrary"),
        name="moe_combine",
    )(dest, yg, meta, hh, modl, modc)


def _moe(hh, modl, modc, g, w_router, wg, wu, wd, *, rows, seq):
    b, nt, d = hh.shape
    n_exp = w_router.shape[1]
    wr = jnp.pad(w_router, ((0, 0), (0, LANE - n_exp)))
    whi, wlo = _split2(wr)
    tm = _pick(rows, (768, 512, 256))
    u, meta, counts = _router(hh, modl, modc, g, whi, wlo, rows=rows, tm=tm, seq=seq, n_exp=n_exp)

    cnt = counts[0, :n_exp].astype(jnp.int32)
    padded = (cnt + MOE_BLOCK - 1) // MOE_BLOCK * MOE_BLOCK
    pend = jnp.cumsum(padded)
    pstart = pend - padded
    n_assign = b * rows * TOP_K
    nb = -(-(n_assign + n_exp * (MOE_BLOCK - 1)) // MOE_BLOCK)
    idx = meta[:, :, 0:TOP_K].astype(jnp.int32)
    rank = meta[:, :, 4:4 + TOP_K].astype(jnp.int32)
    dest = (jnp.take(pstart, idx) + rank).reshape(-1)
    blk_start = jnp.arange(nb, dtype=jnp.int32) * MOE_BLOCK
    block_expert = jnp.minimum(jnp.sum(pend[None, :] <= blk_start[:, None], axis=1), n_exp - 1).astype(jnp.int32)
    n_used = (pend[-1:] // MOE_BLOCK).astype(jnp.int32)

    xg0 = jnp.zeros((nb * MOE_BLOCK, d), F32)
    td = _pick(rows, (256,))
    xg = _dispatch(dest, u, xg0, tm=td)
    yg = _experts(block_expert, n_used, xg, wg, wu, wd, th=_pick(wg.shape[2], (512, 256, 128)))
    return _combine(dest, yg, meta, hh, modl, modc, tm=td, seq=seq)


def _rope_tables(seq, nt):
    rows = seq // GRID_W
    row = jnp.repeat(jnp.arange(rows), GRID_W).astype(F32)
    col = jnp.tile(jnp.arange(GRID_W), rows).astype(F32)
    n_freq = ATT_HEAD_DIM // 4
    inv_freq = ROPE_THETA ** (-jnp.arange(n_freq, dtype=F32) / n_freq)
    ang = jnp.concatenate([row[:, None] * inv_freq, col[:, None] * inv_freq], axis=-1)
    cos, sin = jnp.cos(ang), jnp.sin(ang)
    cos_h = jnp.concatenate([cos, cos], axis=-1)
    sin_h = jnp.concatenate([-sin, sin], axis=-1)
    reps = LANE // ATT_HEAD_DIM
    cos_t = jnp.tile(cos_h, (1, reps))
    sin_t = jnp.tile(sin_h, (1, reps))
    pad = nt - seq
    cos_t = jnp.concatenate([cos_t, jnp.ones((pad, LANE), F32)], axis=0)
    sin_t = jnp.concatenate([sin_t, jnp.zeros((pad, LANE), F32)], axis=0)
    return cos_t, sin_t


def _head_mean_matrix(n):
    r = jnp.arange(n) // ATT_HEAD_DIM
    return (r[:, None] == r[None, :]).astype(F32).astype(BF16) * jnp.asarray(1.0 / ATT_HEAD_DIM, BF16)


def kernel(x, c, ctx, c_ctx, w_mod, b_mod, norm_g, ssm_w_in, ssm_conv_w, ssm_conv_b, ssm_dt_bias, ssm_a_log, ssm_d, ssm_norm_g, ssm_w_out, att_w_qkv, att_q_norm, att_k_norm, att_w_out, pool_w_in, pool_w_grp, pool_scale, lru_w_in, lru_conv_w, lru_conv_b, lru_gate_w, lru_gate_b, lru_lambda, lru_w_out, ffn_w_gate, ffn_w_up, ffn_w_down, moe_w_router, moe_w_gate, moe_w_up, moe_w_down):
    b, seq, d = x.shape
    nctx = ctx.shape[1]
    nt = seq + nctx
    depth = w_mod.shape[0]
    n_mixers = 4

    bp = -(-(b + 1) // 8) * 8
    s_rows = jnp.concatenate([c, c_ctx[None, :], jnp.zeros((bp - b - 1, d), F32)], axis=0)
    mods = _mod_vectors(s_rows, w_mod, b_mod)
    modl_all = mods[:, :b].reshape(depth, b, 6, d)
    modc_all = mods[:, b:b + 1].reshape(depth, 1, 6, d)

    hh = jnp.concatenate([x, ctx], axis=1)
    tm_row = _pick(nt, (768, 512, 256))
    cos_t, sin_t = _rope_tables(seq, nt)
    out = None

    for i in range(depth):
        last = i == depth - 1
        kind, j = i % n_mixers, i // n_mixers
        modl, modc = modl_all[i], modc_all[i]
        g1 = norm_g[i, 0][None, :]
        g2 = norm_g[i, 1][None, :]
        proj = functools.partial(_proj, hh, modl, modc, g1, seq=seq, k_shift=0, k_scale=1)

        if kind == 0:
            inner = ssm_d.shape[1] * SSM_HEAD_DIM
            heads = ssm_d.shape[1]
            conv_dim = ssm_conv_w.shape[2]
            w_in = ssm_w_in[j].astype(BF16)
            w_z = w_in[:, :inner]
            w_xbc = w_in[:, inner:inner + conv_dim]
            w_dt = jnp.pad(w_in[:, inner + conv_dim:], ((0, 0), (0, LANE - 2 * heads)))
            dt_bias = jnp.pad(ssm_dt_bias[j].reshape(1, 2 * heads), ((0, 0), (0, LANE - 2 * heads)))
            z = proj(w_z, epi=_epi_plain, tm=tm_row, tn=512, out_dtype=BF16, name="ssd_proj_z")
            xbc = proj(w_xbc, epi=_epi_conv_silu, extra=(ssm_conv_w[j], ssm_conv_b[j][None, :]),
                       extra_specs=(_col_spec(ssm_conv_w.shape[1], 512), _col_spec(1, 512)),
                       tm=nt, tn=512, out_dtype=BF16, name="ssd_proj_xbc")
            dt = proj(w_dt, epi=_epi_softplus, extra=(dt_bias,), extra_specs=(_col_spec(1, LANE),),
                      tm=tm_row, tn=LANE, out_dtype=F32, name="ssd_proj_dt")
            nega = jnp.pad(-jnp.exp(ssm_a_log[j].astype(F32)).reshape(1, 2 * heads), ((0, 0), (0, LANE - 2 * heads)))
            head_of = jnp.arange(inner)[None, :] // SSM_HEAD_DIM
            lanes = jnp.arange(LANE)[:, None]
            expand = jnp.stack([(lanes == head_of + dd * heads) for dd in range(2)]).astype(F32).astype(BF16)
            dskip = jnp.repeat(ssm_d[j].astype(F32), SSM_HEAD_DIM)[None, :]
            yf, yb = _ssd(xbc, dt, nega, dskip, expand, seq=seq, heads=heads)
            hh = _mmres((yf, yb, z), (inner, inner, inner), ssm_w_out[j].astype(BF16), hh, modl, modc,
                        pro=_pro_ssd, tm=tm_row, k_gate=2, seq=seq, name="ssd_out",
                        row_ins=(ssm_norm_g[j][None, :],))
        elif kind == 1:
            nq = att_w_out.shape[1]
            nk = ATT_KV_HEADS * ATT_HEAD_DIM
            w_qkv = att_w_qkv[j].astype(BF16)
            tab_specs = (pl.BlockSpec((tm_row, LANE), lambda bb, ii, jj: (ii, 0)),) * 2

            def qk_extra(n, gain):
                return ((_head_mean_matrix(n), jnp.tile(gain, n // ATT_HEAD_DIM)[None, :], cos_t, sin_t),
                        (pl.BlockSpec((n, n), lambda bb, ii, jj: (0, 0)),
                         pl.BlockSpec((1, n), lambda bb, ii, jj: (0, 0))) + tab_specs)

            ex, sp = qk_extra(nq, att_q_norm[j])
            q = proj(w_qkv[:, :nq], epi=functools.partial(_epi_qk, out_scale=ATT_HEAD_DIM ** -0.5), extra=ex,
                     extra_specs=sp, tm=tm_row, tn=nq, out_dtype=BF16, name="att_proj_q")
            ex, sp = qk_extra(nk, att_k_norm[j])
            k = proj(w_qkv[:, nq:nq + nk], epi=functools.partial(_epi_qk, out_scale=1.0), extra=ex,
                     extra_specs=sp, tm=tm_row, tn=nk, out_dtype=BF16, name="att_proj_k")
            v = proj(w_qkv[:, nq + nk:], epi=_epi_plain, tm=tm_row, tn=nk, out_dtype=BF16, name="att_proj_v")
            o = _attention(q, k, v, seq=seq, tq=_pick(nctx, (128,)))
            hh = _mmres((o,), (nq,), att_w_out[j].astype(BF16), hh, modl, modc, pro=_pro_plain, tm=tm_row,
                        k_gate=2, seq=seq, name="att_out")
        elif kind == 2:
            v = proj(pool_w_in[j].astype(BF16), epi=_epi_plain, tm=tm_row, tn=512, out_dtype=F32,
                     name="pool_proj")
            hh = _pool(v, hh, modl, modc, pool_w_grp[j].astype(BF16), pool_scale[j][None, :], seq=seq,
                       tm=_pick(nctx, (256, 128)))
        else:
            width = lru_w_out.shape[1]
            w_in = lru_w_in[j].astype(BF16)
            gg = proj(w_in[:, :width], epi=_epi_gelu, tm=tm_row, tn=256, out_dtype=BF16, name="lru_proj_gate")
            xr = proj(w_in[:, width:], epi=_epi_conv_bias, extra=(lru_conv_w[j], lru_conv_b[j][None, :]),
                      extra_specs=(_col_spec(lru_conv_w.shape[1], 256), _col_spec(1, 256)),
                      tm=nt, tn=256, out_dtype=BF16, name="lru_proj_x")
            gw = lru_gate_w[j]
            w_cat = jnp.concatenate([gw[:, 0], gw[:, 1]], axis=-1).astype(BF16)
            hs = _lru(xr, w_cat, lru_gate_b[j], lru_lambda[j][:, None, :], seq=seq, tc=_pick(nctx, (256, 128)))
            rows = seq if last else nt
            hh = _mmres((hs[0], hs[1], gg), (width, width, width), lru_w_out[j].astype(BF16), hh, modl, modc,
                        pro=_pro_lru, tm=_pick(rows, (1024, 768, 512, 256)), k_gate=2, seq=seq, name="lru_out",
                        rows=rows)

        kf = i // 2
        if i % 2 == 0:
            hh = _dense_ffn(hh, modl, modc, g2, ffn_w_gate[kf].astype(BF16), ffn_w_up[kf].astype(BF16),
                            ffn_w_down[kf].astype(BF16), tm=tm_row, th=512, seq=seq)
        else:
            rows = seq if last else nt
            hh = _moe(hh, modl, modc, g2, moe_w_router[kf], moe_w_gate[kf].astype(BF16), moe_w_up[kf].astype(BF16),
                      moe_w_down[kf].astype(BF16), rows=rows, seq=seq)
        if last:
            out = hh[:, :seq] if hh.shape[1] != seq else hh
    return out
```

```python
import functools
import math

import jax
import jax.numpy as jnp
from jax import lax
from jax.experimental import pallas as pl
from jax.experimental.pallas import tpu as pltpu

F32 = jnp.float32
BF16 = jnp.bfloat16
EPS = 1e-6

GRID_W = 64
SSM_HEAD_DIM = 64
SSM_GROUPS = 4
SSM_STATE = 128
SSM_CHUNK = 128
ATT_HEAD_DIM = 64
ATT_KV_HEADS = 4
ROPE_THETA = 10000.0
POOL_WINDOWS = (2, 4, 8, 16)
LRU_BLOCKS = 10
LRU_C = 8.0
TOP_K = 2
MOE_BLOCK = 1024

LANE = 128
SUBLANES = 8
POOL_HALO = 64
VMEM_LIMIT = 56 * 1024 * 1024


def _cparams(*sem):
    return pltpu.CompilerParams(dimension_semantics=sem, vmem_limit_bytes=VMEM_LIMIT)


def _pick(n, candidates):
    for c in candidates:
        if n % c == 0:
            return c
    raise ValueError(f"no tile in {candidates} divides {n}")


def _sigmoid(x):
    return 1.0 / (1.0 + jnp.exp(-x))


def _silu(x):
    return x * _sigmoid(x)


def _split2(x):
    hi = x.astype(BF16)
    lo = (x - hi.astype(F32)).astype(BF16)
    return hi, lo


def _split3(x):
    x1 = x.astype(BF16)
    r = x - x1.astype(F32)
    x2 = r.astype(BF16)
    x3 = (r - x2.astype(F32)).astype(BF16)
    return x1, x2, x3


def _dot(a, b):
    return jnp.dot(a, b, preferred_element_type=F32)


def _dot_nt(a, b):
    return lax.dot_general(a, b, (((1,), (1,)), ((), ())), preferred_element_type=F32)


def _dot_tn(a, b):
    return lax.dot_general(a, b, (((0,), (0,)), ((), ())), preferred_element_type=F32)


def _dot_exact_lhs(a_bf16, x_f32):
    x1, x2, x3 = _split3(x_f32)
    return _dot(a_bf16, x1) + _dot(a_bf16, x2) + _dot(a_bf16, x3)


def _modulated(h, modl_ref, modc_ref, g_ref, k_shift, k_scale, row0, seq):
    tm = h.shape[0]
    row = row0 + lax.broadcasted_iota(jnp.int32, (tm, 1), 0)
    is_ctx = row >= seq
    shift = jnp.where(is_ctx, modc_ref[k_shift:k_shift + 1, :], modl_ref[k_shift:k_shift + 1, :])
    scale = jnp.where(is_ctx, modc_ref[k_scale:k_scale + 1, :], modl_ref[k_scale:k_scale + 1, :])
    ms = jnp.mean(h * h, axis=-1, keepdims=True)
    y = h * lax.rsqrt(ms + EPS) * g_ref[...]
    return y * (1.0 + scale) + shift


def _gate_rows(modl_ref, modc_ref, k_gate, tm, row0, seq):
    row = row0 + lax.broadcasted_iota(jnp.int32, (tm, 1), 0)
    return jnp.where(row >= seq, modc_ref[k_gate:k_gate + 1, :], modl_ref[k_gate:k_gate + 1, :])


def _mod_kernel(s_ref, w_ref, b_ref, o_ref):
    s = _silu(s_ref[...])
    o_ref[...] = _dot(s.astype(BF16), w_ref[...].astype(BF16)) + b_ref[...]


def _mod_vectors(s_rows, w_mod, b_mod):
    depth, d, n6 = w_mod.shape
    bp = s_rows.shape[0]
    tn = _pick(n6, (1536, 1024, 512, 256, 128))
    return pl.pallas_call(
        _mod_kernel,
        grid=(depth, n6 // tn),
        in_specs=[pl.BlockSpec((bp, d), lambda l, j: (0, 0)),
                  pl.BlockSpec((None, d, tn), lambda l, j: (l, 0, j)),
                  pl.BlockSpec((None, 1, tn), lambda l, j: (l, 0, j))],
        out_specs=pl.BlockSpec((None, bp, tn), lambda l, j: (l, 0, j)),
        out_shape=jax.ShapeDtypeStruct((depth, bp, n6), F32),
        compiler_params=_cparams("parallel", "parallel"),
        name="mod_vectors",
    )(s_rows, w_mod, b_mod.reshape(depth, 1, n6))


def _seg_conv(x, w_ref, seq):
    nt = x.shape[0]
    row = lax.broadcasted_iota(jnp.int32, (nt, 1), 0)
    in_ctx = row >= seq
    pos = jnp.where(in_ctx, row - seq, row)
    seglen = jnp.where(in_ctx, nt - seq, seq)
    out = x * w_ref[2:3, :]
    for k, off in ((0, -2), (1, -1), (3, 1)):
        shifted = pltpu.roll(x, (-off) % nt, axis=0)
        valid = (pos + off >= 0) & (pos + off < seglen)
        out = out + jnp.where(valid, shifted, 0.0) * w_ref[k:k + 1, :]
    return out


def _epi_plain(acc, extra, row0, seq):
    return acc


def _epi_softplus(acc, extra, row0, seq):
    (b_ref,) = extra
    x = acc + b_ref[...]
    return jnp.maximum(x, 0.0) + jnp.log1p(jnp.exp(-jnp.abs(x)))


def _epi_conv_silu(acc, extra, row0, seq):
    w_ref, b_ref = extra
    return _silu(_seg_conv(acc, w_ref, seq) + b_ref[...])


def _epi_conv_bias(acc, extra, row0, seq):
    w_ref, b_ref = extra
    return _seg_conv(acc, w_ref, seq) + b_ref[...]


def _epi_gelu(acc, extra, row0, seq):
    return jax.nn.gelu(acc)


def _epi_qk(acc, extra, row0, seq, *, out_scale):
    g_ref, gain_ref, cos_ref, sin_ref = extra
    tn = acc.shape[1]
    ms = _dot((acc * acc).astype(BF16), g_ref[...])
    xn = acc * lax.rsqrt(ms + EPS) * gain_ref[...]
    lane = lax.broadcasted_iota(jnp.int32, (1, tn), 1)
    half = ATT_HEAD_DIM // 2
    second = (lane % ATT_HEAD_DIM) >= half
    partner = jnp.where(second, pltpu.roll(xn, half, axis=1), pltpu.roll(xn, tn - half, axis=1))
    reps = tn // LANE
    cos = jnp.concatenate([cos_ref[...]] * reps, axis=1) if reps > 1 else cos_ref[...]
    sin = jnp.concatenate([sin_ref[...]] * reps, axis=1) if reps > 1 else sin_ref[...]
    return (xn * cos + partner * sin) * out_scale


def _proj_kernel(*refs, epi, n_extra, k_shift, k_scale, seq):
    h_ref, modl_ref, modc_ref, g_ref, w_ref = refs[:5]
    extra = refs[5:5 + n_extra]
    o_ref = refs[5 + n_extra]
    u_scr = refs[6 + n_extra]
    tm = h_ref.shape[0]
    row0 = pl.program_id(1) * tm

    @pl.when(pl.program_id(2) == 0)
    def _():
        u_scr[...] = _modulated(h_ref[...], modl_ref, modc_ref, g_ref, k_shift, k_scale, row0, seq).astype(BF16)

    acc = _dot(u_scr[...], w_ref[...])
    o_ref[...] = epi(acc, extra, row0, seq).astype(o_ref.dtype)


def _proj(hh, modl, modc, g, w, *, epi, extra=(), extra_specs=(), tm, tn, out_dtype, seq, k_shift, k_scale,
          name):
    b, nt, d = hh.shape
    n = w.shape[1]
    grid = (b, nt // tm, n // tn)
    in_specs = [pl.BlockSpec((None, tm, d), lambda bb, i, j: (bb, i, 0)),
                pl.BlockSpec((None, 6, d), lambda bb, i, j: (bb, 0, 0)),
                pl.BlockSpec((None, 6, d), lambda bb, i, j: (0, 0, 0)),
                pl.BlockSpec((1, d), lambda bb, i, j: (0, 0)),
                pl.BlockSpec((d, tn), lambda bb, i, j: (0, j))] + list(extra_specs)
    kern = functools.partial(_proj_kernel, epi=epi, n_extra=len(extra), k_shift=k_shift, k_scale=k_scale, seq=seq)
    return pl.pallas_call(
        kern, grid=grid, in_specs=in_specs,
        out_specs=pl.BlockSpec((None, tm, tn), lambda bb, i, j: (bb, i, j)),
        out_shape=jax.ShapeDtypeStruct((b, nt, n), out_dtype),
        scratch_shapes=[pltpu.VMEM((tm, d), BF16)],
        compiler_params=_cparams("parallel", "parallel", "arbitrary"),
        name=name,
    )(hh, modl, modc, g, w, *extra)


def _col_spec(rows, tn):
    return pl.BlockSpec((rows, tn), lambda bb, i, j: (0, j))


def _pro_plain(ins):
    (a_ref,) = ins
    return a_ref[...]


def _pro_ssd(ins):
    yf_ref, yb_ref, z_ref, ng_ref = ins
    y = (yf_ref[...].astype(F32) + yb_ref[...].astype(F32)) * _silu(z_ref[...].astype(F32))
    ms = jnp.mean(y * y, axis=-1, keepdims=True)
    return (y * lax.rsqrt(ms + EPS) * ng_ref[...]).astype(BF16)


def _pro_lru(ins):
    hf_ref, hb_ref, gg_ref = ins
    hsum = (hf_ref[...].astype(F32) + hb_ref[...].astype(F32))
    return (hsum * gg_ref[...].astype(F32)).astype(BF16)


def _mmres_kernel(*refs, pro, n_in, k_gate, seq):
    ins = refs[:n_in]
    w_ref, h_ref, modl_ref, modc_ref, o_ref = refs[n_in:n_in + 5]
    tm = h_ref.shape[0]
    row0 = pl.program_id(1) * tm
    y = _dot(pro(ins), w_ref[...])
    gate = _gate_rows(modl_ref, modc_ref, k_gate, tm, row0, seq)
    o_ref[...] = h_ref[...] + gate * y


def _mmres(ins, in_widths, w, hh, modl, modc, *, pro, tm, k_gate, seq, name, rows=None, row_ins=()):
    b, nt, d = hh.shape
    rows = nt if rows is None else rows
    k = w.shape[0]
    grid = (b, rows // tm)
    in_specs = [pl.BlockSpec((None, tm, wd), lambda bb, i: (bb, i, 0)) for wd in in_widths]
    in_specs += [pl.BlockSpec((1, r.shape[1]), lambda bb, i: (0, 0)) for r in row_ins]
    in_specs += [pl.BlockSpec((k, d), lambda bb, i: (0, 0)),
                 pl.BlockSpec((None, tm, d), lambda bb, i: (bb, i, 0)),
                 pl.BlockSpec((None, 6, d), lambda bb, i: (bb, 0, 0)),
                 pl.BlockSpec((None, 6, d), lambda bb, i: (0, 0, 0))]
    kern = functools.partial(_mmres_kernel, pro=pro, n_in=len(ins) + len(row_ins), k_gate=k_gate, seq=seq)
    return pl.pallas_call(
        kern, grid=grid, in_specs=in_specs,
        out_specs=pl.BlockSpec((None, tm, d), lambda bb, i: (bb, i, 0)),
        out_shape=jax.ShapeDtypeStruct((b, rows, d), F32),
        compiler_params=_cparams("parallel", "parallel"),
        name=name,
    )(*ins, *row_ins, w, hh, modl, modc)


def _ffn_kernel(h_ref, modl_ref, modc_ref, g_ref, wg_ref, wu_ref, wd_ref, o_ref, u_scr, acc_scr, *, seq):
    tm = h_ref.shape[0]
    row0 = pl.program_id(1) * tm
    j = pl.program_id(2)

    @pl.when(j == 0)
    def _():
        u_scr[...] = _modulated(h_ref[...], modl_ref, modc_ref, g_ref, 3, 4, row0, seq).astype(BF16)
        acc_scr[...] = jnp.zeros_like(acc_scr)

    u = u_scr[...]
    a = (_silu(_dot(u, wg_ref[...])) * _dot(u, wu_ref[...])).astype(BF16)
    acc_scr[...] += _dot(a, wd_ref[...])

    @pl.when(j == pl.num_programs(2) - 1)
    def _():
        gate = _gate_rows(modl_ref, modc_ref, 5, tm, row0, seq)
        o_ref[...] = h_ref[...] + gate * acc_scr[...]


def _dense_ffn(hh, modl, modc, g, wg, wu, wd, *, tm, th, seq):
    b, nt, d = hh.shape
    hid = wg.shape[1]
    grid = (b, nt // tm, hid // th)
    return pl.pallas_call(
        functools.partial(_ffn_kernel, seq=seq),
        grid=grid,
        in_specs=[pl.BlockSpec((None, tm, d), lambda bb, i, j: (bb, i, 0)),
                  pl.BlockSpec((None, 6, d), lambda bb, i, j: (bb, 0, 0)),
                  pl.BlockSpec((None, 6, d), lambda bb, i, j: (0, 0, 0)),
                  pl.BlockSpec((1, d), lambda bb, i, j: (0, 0)),
                  pl.BlockSpec((d, th), lambda bb, i, j: (0, j)),
                  pl.BlockSpec((d, th), lambda bb, i, j: (0, j)),
                  pl.BlockSpec((th, d), lambda bb, i, j: (j, 0))],
        out_specs=pl.BlockSpec((None, tm, d), lambda bb, i, j: (bb, i, 0)),
        out_shape=jax.ShapeDtypeStruct((b, nt, d), F32),
        scratch_shapes=[pltpu.VMEM((tm, d), BF16), pltpu.VMEM((tm, d), F32)],
        compiler_params=_cparams("parallel", "parallel", "arbitrary"),
        name="dense_ffn",
    )(hh, modl, modc, g, wg, wu, wd)


def _sel01(x_f32, e_bf16):
    x1, x2, x3 = _split3(x_f32)
    return _dot(x1, e_bf16) + _dot(x2, e_bf16) + _dot(x3, e_bf16)


def _ssd_chunk(xbc, dt, nega, dskip, expand, lane0, state_ref, o_ref, *, reverse, add_skip):
    q = xbc.shape[0]
    g_n = SSM_GROUPS * SSM_STATE
    inner = xbc.shape[1] - 2 * g_n
    heads = inner // SSM_HEAD_DIM
    rp = inner // SSM_GROUPS
    r_heads = heads // SSM_GROUPS

    rows = lax.broadcasted_iota(jnp.int32, (q, q), 0)
    cols = lax.broadcasted_iota(jnp.int32, (q, q), 1)
    keep = (rows <= cols) if reverse else (rows >= cols)
    tri = keep.astype(BF16)

    la = dt * nega
    ac = _dot_exact_lhs(tri, la)
    ac_t = jnp.transpose(ac)
    last = 0 if reverse else q - 1
    ac_end = ac[last:last + 1, :]

    x = xbc[:, :inner].astype(F32)
    xdt = x * _sel01(dt, expand)
    to_end = jnp.exp(_sel01(ac_end - ac, expand))
    from_start = jnp.exp(_sel01(ac, expand))
    chunk_decay = from_start[last:last + 1, :]
    xdt_b = xdt.astype(BF16)
    xend_b = (xdt * to_end).astype(BF16)

    for g in range(SSM_GROUPS):
        bg = xbc[:, inner + g * SSM_STATE: inner + (g + 1) * SSM_STATE]
        cg = xbc[:, inner + g_n + g * SSM_STATE: inner + g_n + (g + 1) * SSM_STATE]
        cb = _dot_nt(cg, bg)
        pieces = []
        for r in range(r_heads):
            hd = g * r_heads + r
            ln = lane0 + hd
            diff = ac[:, ln:ln + 1] - ac_t[ln:ln + 1, :]
            dec = jnp.where(keep, jnp.exp(jnp.minimum(diff, 0.0)), 0.0)
            m = (cb * dec).astype(BF16)
            pieces.append(_dot(m, xdt_b[:, hd * SSM_HEAD_DIM:(hd + 1) * SSM_HEAD_DIM]))
        y_diag = jnp.concatenate(pieces, axis=1)
        sl = slice(g * rp, (g + 1) * rp)
        h_in = state_ref[g]
        y_off = _dot(cg, h_in.astype(BF16)) * from_start[:, sl]
        states = _dot_tn(bg, xend_b[:, sl])
        state_ref[g] = h_in * chunk_decay[:, sl] + states
        y = y_diag + y_off
        if add_skip:
            y = y + x[:, sl] * dskip[:, sl]
        o_ref[:, sl] = y.astype(o_ref.dtype)


def _ssd_kernel(xf_ref, xb_ref, dtf_ref, dtb_ref, nega_ref, dskip_ref, exp_ref, yf_ref, yb_ref, st_ref, *, heads):
    @pl.when(pl.program_id(1) == 0)
    def _():
        st_ref[...] = jnp.zeros_like(st_ref)

    _ssd_chunk(xf_ref[...], dtf_ref[...], nega_ref[...], dskip_ref[...], exp_ref[0], 0, st_ref.at[0], yf_ref,
               reverse=False, add_skip=True)
    _ssd_chunk(xb_ref[...], dtb_ref[...], nega_ref[...], dskip_ref[...], exp_ref[1], heads, st_ref.at[1], yb_ref,
               reverse=True, add_skip=False)


def _ssd(xbc, dt, nega, dskip, expand, *, seq, heads):
    b, nt, width = xbc.shape
    q = SSM_CHUNK
    inner = heads * SSM_HEAD_DIM
    nch = nt // q
    ncl = seq // q

    def fwd(bb, j):
        return (bb, (j + ncl) % nch, 0)

    def bwd(bb, j):
        return (bb, nch - 1 - j, 0)

    return pl.pallas_call(
        functools.partial(_ssd_kernel, heads=heads),
        grid=(b, nch),
        in_specs=[pl.BlockSpec((None, q, width), fwd),
                  pl.BlockSpec((None, q, width), bwd),
                  pl.BlockSpec((None, q, LANE), fwd),
                  pl.BlockSpec((None, q, LANE), bwd),
                  pl.BlockSpec((1, LANE), lambda bb, j: (0, 0)),
                  pl.BlockSpec((1, inner), lambda bb, j: (0, 0)),
                  pl.BlockSpec((2, LANE, inner), lambda bb, j: (0, 0, 0))],
        out_specs=[pl.BlockSpec((None, q, inner), fwd),
                   pl.BlockSpec((None, q, inner), bwd)],
        out_shape=[jax.ShapeDtypeStruct((b, nt, inner), BF16)] * 2,
        scratch_shapes=[pltpu.VMEM((2, SSM_GROUPS, SSM_STATE, inner // SSM_GROUPS), F32)],
        compiler_params=_cparams("parallel", "arbitrary"),
        name="ssd_scan",
    )(xbc, xbc, dt, dt, nega, dskip, expand)


def _attn_heads(q, k, v, o_ref):
    hd = ATT_HEAD_DIM
    n_q = q.shape[1] // hd
    rep = n_q // ATT_KV_HEADS
    tq = q.shape[0]
    for g in range(ATT_KV_HEADS):
        kg = k[:, g * hd:(g + 1) * hd]
        vg = v[:, g * hd:(g + 1) * hd]
        qg = jnp.concatenate([q[:, (g * rep + r) * hd:(g * rep + r + 1) * hd] for r in range(rep)], axis=0)
        s = _dot_nt(qg, kg)
        m = jnp.max(s, axis=-1, keepdims=True)
        p = jnp.exp2(s - m).astype(BF16)
        v_aug = jnp.concatenate([vg, jnp.ones_like(vg)], axis=1)
        o_aug = _dot(p, v_aug)
        o = o_aug[:, :hd] / o_aug[:, hd:hd + 1]
        og = jnp.concatenate([o[r * tq:(r + 1) * tq, :] for r in range(rep)], axis=1)
        o_ref[:, g * rep * hd:(g + 1) * rep * hd] = og.astype(o_ref.dtype)


def _attn_kernel(q_ref, k_ref, v_ref, o_ref, *, seq):
    tq = q_ref.shape[0]
    is_lat = pl.program_id(1) * tq < seq

    @pl.when(is_lat)
    def _():
        _attn_heads(q_ref[...], k_ref[...], v_ref[...], o_ref)

    @pl.when(jnp.logical_not(is_lat))
    def _():
        _attn_heads(q_ref[...], k_ref[seq:, :], v_ref[seq:, :], o_ref)


def _attention(q, k, v, *, seq, tq):
    b, nt, dq = q.shape
    dk = k.shape[2]
    return pl.pallas_call(
        functools.partial(_attn_kernel, seq=seq),
        grid=(b, nt // tq),
        in_specs=[pl.BlockSpec((None, tq, dq), lambda bb, i: (bb, i, 0)),
                  pl.BlockSpec((None, nt, dk), lambda bb, i: (bb, 0, 0)),
                  pl.BlockSpec((None, nt, dk), lambda bb, i: (bb, 0, 0))],
        out_specs=pl.BlockSpec((None, tq, dq), lambda bb, i: (bb, i, 0)),
        out_shape=jax.ShapeDtypeStruct((b, nt, dq), BF16),
        compiler_params=_cparams("parallel", "parallel"),
        name="gqa_attention",
    )(q, k, v)


def _pool_kernel(v_ref, vp_ref, vn_ref, h_ref, modl_ref, modc_ref, wg_ref, sc_ref, o_ref, *, seq, nt):
    tm = v_ref.shape[0]
    halo = vp_ref.shape[0]
    i = pl.program_id(1)
    row0 = i * tm
    v = v_ref[...]
    vext = jnp.concatenate([vp_ref[...], v, vn_ref[...]], axis=0)
    vhi, vlo = _split2(vext)
    t = row0 + lax.broadcasted_iota(jnp.int32, (tm, 1), 0)
    s = row0 - halo + lax.broadcasted_iota(jnp.int32, (1, tm + 2 * halo), 1)
    in_ctx = t >= seq
    seg_lo = jnp.where(in_ctx, seq, 0)
    seg_hi = jnp.where(in_ctx, nt, seq)
    grp = v.shape[1] // len(POOL_WINDOWS)
    outs = []
    for gi, win in enumerate(POOL_WINDOWS):
        lo = jnp.maximum(t - win // 2, seg_lo)
        hi = jnp.minimum(t + win // 2, seg_hi)
        band = ((s >= lo) & (s < hi)).astype(BF16)
        sl = slice(gi * grp, (gi + 1) * grp)
        summed = _dot(band, vhi[:, sl]) + _dot(band, vlo[:, sl])
        pooled = summed / (hi - lo).astype(F32) - v[:, sl]
        outs.append(_dot(pooled.astype(BF16), wg_ref[gi]))
    y = jnp.concatenate(outs, axis=1) * sc_ref[...]
    gate = _gate_rows(modl_ref, modc_ref, 2, tm, row0, seq)
    o_ref[...] = h_ref[...] + gate * y


def _pool(v, hh, modl, modc, w_grp, scale, *, seq, tm):
    b, nt, d = hh.shape
    halo = POOL_HALO
    per = tm // halo
    nhb = nt // halo
    grp = w_grp.shape[1]
    return pl.pallas_call(
        functools.partial(_pool_kernel, seq=seq, nt=nt),
        grid=(b, nt // tm),
        in_specs=[pl.BlockSpec((None, tm, d), lambda bb, i: (bb, i, 0)),
                  pl.BlockSpec((None, halo, d), lambda bb, i: (bb, jnp.maximum(i * per - 1, 0), 0)),
                  pl.BlockSpec((None, halo, d), lambda bb, i: (bb, jnp.minimum((i + 1) * per, nhb - 1), 0)),
                  pl.BlockSpec((None, tm, d), lambda bb, i: (bb, i, 0)),
                  pl.BlockSpec((None, 6, d), lambda bb, i: (bb, 0, 0)),
                  pl.BlockSpec((None, 6, d), lambda bb, i: (0, 0, 0)),
                  pl.BlockSpec((len(POOL_WINDOWS), grp, grp), lambda bb, i: (0, 0, 0)),
                  pl.BlockSpec((1, d), lambda bb, i: (0, 0))],
        out_specs=pl.BlockSpec((None, tm, d), lambda bb, i: (bb, i, 0)),
        out_shape=jax.ShapeDtypeStruct((b, nt, d), F32),
        compiler_params=_cparams("parallel", "parallel"),
        name="pool_mixer",
    )(v, v, v, hh, modl, modc, w_grp, scale)


def _lru_scan_chunk(a_scr, bx_scr, h_scr, carry_scr, *, reverse):
    nblk, tc, _ = a_scr.shape
    seg = tc // SUBLANES

    def rows(j):
        return pl.ds(j, SUBLANES, stride=seg)

    def local(jj, st):
        j = seg - 1 - jj if reverse else jj
        out = []
        for k in range(nblk):
            prod, h = st[k]
            a = a_scr[k, rows(j), :]
            prod = a * prod
            h = a * h + bx_scr[k, rows(j), :]
            a_scr[k, rows(j), :] = prod
            bx_scr[k, rows(j), :] = h
            out.append((prod, h))
        return tuple(out)

    init = tuple((jnp.ones((SUBLANES, LANE), F32), jnp.zeros((SUBLANES, LANE), F32)) for _ in range(nblk))
    ends = lax.fori_loop(0, seg, local, init, unroll=2)
    cmats = []
    for k in range(nblk):
        prod, h = ends[k]
        carry = carry_scr[k]
        incoming = [None] * SUBLANES
        for s in (range(SUBLANES - 1, -1, -1) if reverse else range(SUBLANES)):
            incoming[s] = carry
            carry = prod[s:s + 1, :] * carry + h[s:s + 1, :]
        carry_scr[k] = carry
        cmats.append(jnp.concatenate(incoming, axis=0))

    def fix(j, c):
        for k in range(nblk):
            h_scr[k, rows(j), :] = bx_scr[k, rows(j), :] + a_scr[k, rows(j), :] * cmats[k]
        return c

    lax.fori_loop(0, seg, fix, 0, unroll=2)


def _lru_kernel(x_ref, w_ref, b_ref, lam_ref, o_ref, a_scr, bx_scr, h_scr, carry_scr):
    d_dir = pl.program_id(1)
    tc, width = x_ref.shape
    bw = width // LRU_BLOCKS

    @pl.when(pl.program_id(2) == 0)
    def _():
        carry_scr[...] = jnp.zeros_like(carry_scr)

    xb = x_ref[...]
    for k in range(LRU_BLOCKS):
        sl = slice(k * bw, (k + 1) * bw)
        pre = _dot(xb[:, sl], w_ref[k])
        r_gate = _sigmoid(pre[:, :bw] + b_ref[0:1, sl])
        i_gate = _sigmoid(pre[:, bw:] + b_ref[1:2, sl])
        neg_lam = -lam_ref[:, sl]
        softplus = jnp.maximum(neg_lam, 0.0) + jnp.log1p(jnp.exp(-jnp.abs(neg_lam)))
        log_a = (-LRU_C) * r_gate * softplus
        a = jnp.exp(log_a)
        a_scr[k] = a
        bx_scr[k] = jnp.sqrt(-jnp.tanh(log_a) * (a * a + 1.0)) * (i_gate * xb[:, sl].astype(F32))

    @pl.when(d_dir == 0)
    def _():
        _lru_scan_chunk(a_scr, bx_scr, h_scr, carry_scr, reverse=False)

    @pl.when(d_dir == 1)
    def _():
        _lru_scan_chunk(a_scr, bx_scr, h_scr, carry_scr, reverse=True)

    for k in range(LRU_BLOCKS):
        o_ref[:, k * bw:(k + 1) * bw] = h_scr[k].astype(o_ref.dtype)


def _lru(xr, w_cat, bias, lam, *, seq, tc):
    b, nt, width = xr.shape
    nc = nt // tc
    nl = seq // tc

    def chunk(d, j):
        return d * (nc - 1 - j) + (1 - d) * ((j + nl) % nc)

    return pl.pallas_call(
        _lru_kernel,
        grid=(b, 2, nc),
        in_specs=[pl.BlockSpec((None, tc, width), lambda bb, d, j: (bb, chunk(d, j), 0)),
                  pl.BlockSpec((None,) + w_cat.shape[1:], lambda bb, d, j: (d, 0, 0, 0)),
                  pl.BlockSpec((None, 2, width), lambda bb, d, j: (d, 0, 0)),
                  pl.BlockSpec((None, 1, width), lambda bb, d, j: (d, 0, 0))],
        out_specs=pl.BlockSpec((None, None, tc, width), lambda bb, d, j: (d, bb, chunk(d, j), 0)),
        out_shape=jax.ShapeDtypeStruct((2, b, nt, width), BF16),
        scratch_shapes=[pltpu.VMEM((LRU_BLOCKS, tc, width // LRU_BLOCKS), F32)] * 3
        + [pltpu.VMEM((LRU_BLOCKS, 1, width // LRU_BLOCKS), F32)],
        compiler_params=_cparams("parallel", "arbitrary", "arbitrary"),
        name="rglru_scan",
    )(xr, w_cat, bias, lam)


def _router_kernel(h_ref, modl_ref, modc_ref, g_ref, whi_ref, wlo_ref, u_ref, meta_ref, cnt_ref, carry_scr, *,
                   seq, n_exp):
    tm = h_ref.shape[0]
    row0 = pl.program_id(1) * tm
    first = (pl.program_id(0) == 0) & (pl.program_id(1) == 0)

    @pl.when(first)
    def _():
        carry_scr[...] = jnp.zeros_like(carry_scr)

    u = _modulated(h_ref[...], modl_ref, modc_ref, g_ref, 3, 4, row0, seq)
    u_ref[...] = u
    uhi, ulo = _split2(u)
    logits = _dot(uhi, whi_ref[...]) + _dot(ulo, whi_ref[...]) + _dot(uhi, wlo_ref[...])
    lane = lax.broadcasted_iota(jnp.int32, (tm, LANE), 1)
    neg = jnp.float32(-jnp.inf)
    logits = jnp.where(lane < n_exp, logits, neg)
    v0 = jnp.max(logits, axis=-1, keepdims=True)
    lane_f = lane.astype(F32)
    i0 = jnp.min(jnp.where(logits == v0, lane_f, float(LANE)), axis=-1, keepdims=True)
    rest = jnp.where(lane_f == i0, neg, logits)
    v1 = jnp.max(rest, axis=-1, keepdims=True)
    i1 = jnp.min(jnp.where(rest == v1, lane_f, float(LANE)), axis=-1, keepdims=True)
    g0 = 1.0 / (1.0 + jnp.exp(v1 - v0))
    g1 = 1.0 - g0
    sel0 = lane_f == i0
    sel1 = lane_f == i1
    onehot = (sel0 | sel1).astype(BF16)
    rr = lax.broadcasted_iota(jnp.int32, (tm, tm), 0)
    cc = lax.broadcasted_iota(jnp.int32, (tm, tm), 1)
    before = (cc < rr).astype(BF16)
    prefix = _dot(before, onehot) + carry_scr[...]
    r0 = jnp.sum(jnp.where(sel0, prefix, 0.0), axis=-1, keepdims=True)
    r1 = jnp.sum(jnp.where(sel1, prefix, 0.0), axis=-1, keepdims=True)
    carry_scr[...] += jnp.sum(onehot.astype(F32), axis=0, keepdims=True)
    cnt_ref[...] = carry_scr[...]
    meta = jnp.where(lane == 0, i0, 0.0)
    meta = jnp.where(lane == 1, i1, meta)
    meta = jnp.where(lane == 2, g0, meta)
    meta = jnp.where(lane == 3, g1, meta)
    meta = jnp.where(lane == 4, r0, meta)
    meta = jnp.where(lane == 5, r1, meta)
    meta_ref[...] = meta


def _router(hh, modl, modc, g, whi, wlo, *, rows, tm, seq, n_exp):
    b, nt, d = hh.shape
    return pl.pallas_call(
        functools.partial(_router_kernel, seq=seq, n_exp=n_exp),
        grid=(b, rows // tm),
        in_specs=[pl.BlockSpec((None, tm, d), lambda bb, i: (bb, i, 0)),
                  pl.BlockSpec((None, 6, d), lambda bb, i: (bb, 0, 0)),
                  pl.BlockSpec((None, 6, d), lambda bb, i: (0, 0, 0)),
                  pl.BlockSpec((1, d), lambda bb, i: (0, 0)),
                  pl.BlockSpec((d, LANE), lambda bb, i: (0, 0)),
                  pl.BlockSpec((d, LANE), lambda bb, i: (0, 0))],
        out_specs=[pl.BlockSpec((None, tm, d), lambda bb, i: (bb, i, 0)),
                   pl.BlockSpec((None, tm, LANE), lambda bb, i: (bb, i, 0)),
                   pl.BlockSpec((1, LANE), lambda bb, i: (0, 0))],
        out_shape=[jax.ShapeDtypeStruct((b, rows, d), F32),
                   jax.ShapeDtypeStruct((b, rows, LANE), F32),
                   jax.ShapeDtypeStruct((1, LANE), F32)],
        scratch_shapes=[pltpu.VMEM((1, LANE), F32)],
        compiler_params=_cparams("arbitrary", "arbitrary"),
        name="moe_router",
    )(hh, modl, modc, g, whi, wlo)


def _dispatch_kernel(dest_ref, u_ref, xg_in, xg_hbm, sem, *, tm, rows_per_b):
    del xg_in
    base = (pl.program_id(0) * rows_per_b + pl.program_id(1) * tm) * TOP_K

    def issue(r, c):
        for k in range(TOP_K):
            dst = xg_hbm.at[pl.ds(dest_ref[base + r * TOP_K + k], 1)]
            pltpu.make_async_copy(u_ref.at[pl.ds(r, 1)], dst, sem).start()
        return c

    lax.fori_loop(0, tm, issue, 0, unroll=8)
    for k in range(TOP_K):
        pltpu.make_async_copy(u_ref, xg_hbm.at[pl.ds(0, tm)], sem).wait()


def _dispatch(dest, u, xg0, *, tm):
    b, rows, d = u.shape
    grid_spec = pltpu.PrefetchScalarGridSpec(
        num_scalar_prefetch=1, grid=(b, rows // tm),
        in_specs=[pl.BlockSpec((tm, d), lambda bb, i, ds: (bb * (rows // tm) + i, 0)),
                  pl.BlockSpec(memory_space=pl.ANY)],
        out_specs=pl.BlockSpec(memory_space=pl.ANY),
        scratch_shapes=[pltpu.SemaphoreType.DMA(())])
    return pl.pallas_call(
        functools.partial(_dispatch_kernel, tm=tm, rows_per_b=rows),
        grid_spec=grid_spec,
        out_shape=jax.ShapeDtypeStruct(xg0.shape, xg0.dtype),
        input_output_aliases={2: 0},
        compiler_params=_cparams("arbitrary", "arbitrary"),
        name="moe_dispatch",
    )(dest, u.reshape(b * rows, d), xg0)


def _expert_kernel(be_ref, nu_ref, x_ref, wg_ref, wu_ref, wd_ref, y_ref, xb_scr, acc_scr):
    blk = pl.program_id(0)
    j = pl.program_id(1)
    used = blk < nu_ref[0]

    @pl.when(used & (j == 0))
    def _():
        xb_scr[...] = x_ref[...].astype(BF16)
        acc_scr[...] = jnp.zeros_like(acc_scr)

    @pl.when(used)
    def _():
        x = xb_scr[...]
        a = (_silu(_dot(x, wg_ref[...].astype(BF16))) * _dot(x, wu_ref[...].astype(BF16))).astype(BF16)
        acc_scr[...] += _dot(a, wd_ref[...].astype(BF16))

    @pl.when(used & (j == pl.num_programs(1) - 1))
    def _():
        y_ref[...] = acc_scr[...]

    @pl.when(jnp.logical_not(used) & (j == pl.num_programs(1) - 1))
    def _():
        y_ref[...] = jnp.zeros_like(y_ref)


def _experts(block_expert, n_used, xg, wg, wu, wd, *, th):
    p, d = xg.shape
    hid = wg.shape[2]
    nb = p // MOE_BLOCK
    nj = hid // th

    def blk_eff(blk, nu):
        return jnp.minimum(blk, nu[0] - 1)

    def j_eff(blk, j, nu):
        return jnp.where(blk < nu[0], j, nj - 1)

    grid_spec = pltpu.PrefetchScalarGridSpec(
        num_scalar_prefetch=2, grid=(nb, nj),
        in_specs=[pl.BlockSpec((MOE_BLOCK, d), lambda blk, j, be, nu: (blk_eff(blk, nu), 0)),
                  pl.BlockSpec((None, d, th), lambda blk, j, be, nu: (be[blk_eff(blk, nu)], 0, j_eff(blk, j, nu))),
                  pl.BlockSpec((None, d, th), lambda blk, j, be, nu: (be[blk_eff(blk, nu)], 0, j_eff(blk, j, nu))),
                  pl.BlockSpec((None, th, d), lambda blk, j, be, nu: (be[blk_eff(blk, nu)], j_eff(blk, j, nu), 0))],
        out_specs=pl.BlockSpec((MOE_BLOCK, d), lambda blk, j, be, nu: (blk, 0)),
        scratch_shapes=[pltpu.VMEM((MOE_BLOCK, d), BF16), pltpu.VMEM((MOE_BLOCK, d), F32)])
    return pl.pallas_call(
        _expert_kernel, grid_spec=grid_spec,
        out_shape=jax.ShapeDtypeStruct((p, d), F32),
        compiler_params=_cparams("arbitrary", "arbitrary"),
        name="moe_experts",
    )(block_expert, n_used, xg, wg, wu, wd)


def _combine_kernel(dest_ref, yg_hbm, meta_ref, h_ref, modl_ref, modc_ref, o_ref, y0_scr, y1_scr, sem, *, tm,
                    rows_per_b, seq):
    bb = pl.program_id(0)
    i = pl.program_id(1)
    base = (bb * rows_per_b + i * tm) * TOP_K
    bufs = (y0_scr, y1_scr)

    def issue(r, c):
        for k in range(TOP_K):
            src = yg_hbm.at[pl.ds(dest_ref[base + r * TOP_K + k], 1)]
            pltpu.make_async_copy(src, bufs[k].at[pl.ds(r, 1)], sem).start()
        return c

    lax.fori_loop(0, tm, issue, 0, unroll=8)
    for k in range(TOP_K):
        pltpu.make_async_copy(yg_hbm.at[pl.ds(0, tm)], bufs[k], sem).wait()
    meta = meta_ref[...]
    y = meta[:, 2:3] * y0_scr[...] + meta[:, 3:4] * y1_scr[...]
    gate = _gate_rows(modl_ref, modc_ref, 5, tm, i * tm, seq)
    o_ref[...] = h_ref[...] + gate * y


def _combine(dest, yg, meta, hh, modl, modc, *, tm, seq):
    b, rows, _ = meta.shape
    d = hh.shape[2]
    grid_spec = pltpu.PrefetchScalarGridSpec(
        num_scalar_prefetch=1, grid=(b, rows // tm),
        in_specs=[pl.BlockSpec(memory_space=pl.ANY),
                  pl.BlockSpec((None, tm, LANE), lambda bb, i, ds: (bb, i, 0)),
                  pl.BlockSpec((None, tm, d), lambda bb, i, ds: (bb, i, 0)),
                  pl.BlockSpec((None, 6, d), lambda bb, i, ds: (bb, 0, 0)),
                  pl.BlockSpec((None, 6, d), lambda bb, i, ds: (0, 0, 0))],
        out_specs=pl.BlockSpec((None, tm, d), lambda bb, i, ds: (bb, i, 0)),
        scratch_shapes=[pltpu.VMEM((tm, d), F32), pltpu.VMEM((tm, d), F32), pltpu.SemaphoreType.DMA(())])
    return pl.pallas_call(
        functools.partial(_combine_kernel, tm=tm, rows_per_b=rows, seq=seq),
        grid_spec=grid_spec,
        out_shape=jax.ShapeDtypeStruct((b, rows, d), F32),
        compiler_params=_cparams("arbitrary", "arbitrary"),
        name="moe_combine",
    )(dest, yg, meta, hh, modl, modc)


def _moe(hh, modl, modc, g, w_router, wg, wu, wd, *, rows, seq):
    b, nt, d = hh.shape
    n_exp = w_router.shape[1]
    wr = jnp.pad(w_router, ((0, 0), (0, LANE - n_exp)))
    whi, wlo = _split2(wr)
    tm = _pick(rows, (768, 512, 256))
    u, meta, counts = _router(hh, modl, modc, g, whi, wlo, rows=rows, tm=tm, seq=seq, n_exp=n_exp)

    cnt = counts[0, :n_exp].astype(jnp.int32)
    padded = (cnt + MOE_BLOCK - 1) // MOE_BLOCK * MOE_BLOCK
    pend = jnp.cumsum(padded)
    pstart = pend - padded
    n_assign = b * rows * TOP_K
    nb = -(-(n_assign + n_exp * (MOE_BLOCK - 1)) // MOE_BLOCK)
    idx = meta[:, :, 0:TOP_K].astype(jnp.int32)
    rank = meta[:, :, 4:4 + TOP_K].astype(jnp.int32)
    dest = (sum(jnp.where(idx == e, pstart[e], 0) for e in range(n_exp)) + rank).reshape(-1)
    blk_start = jnp.arange(nb, dtype=jnp.int32) * MOE_BLOCK
    block_expert = jnp.minimum(jnp.sum(pend[None, :] <= blk_start[:, None], axis=1), n_exp - 1).astype(jnp.int32)
    n_used = (pend[-1:] // MOE_BLOCK).astype(jnp.int32)

    xg0 = jnp.zeros((nb * MOE_BLOCK, d), F32)
    td = tm
    xg = _dispatch(dest, u, xg0, tm=td)
    yg = _experts(block_expert, n_used, xg, wg, wu, wd, th=_pick(wg.shape[2], (512, 256, 128)))
    return _combine(dest, yg, meta, hh, modl, modc, tm=td, seq=seq)


def _rope_tables(seq, nt):
    rows = seq // GRID_W
    row = jnp.repeat(jnp.arange(rows), GRID_W).astype(F32)
    col = jnp.tile(jnp.arange(GRID_W), rows).astype(F32)
    n_freq = ATT_HEAD_DIM // 4
    inv_freq = ROPE_THETA ** (-jnp.arange(n_freq, dtype=F32) / n_freq)
    ang = jnp.concatenate([row[:, None] * inv_freq, col[:, None] * inv_freq], axis=-1)
    cos, sin = jnp.cos(ang), jnp.sin(ang)
    cos_h = jnp.concatenate([cos, cos], axis=-1)
    sin_h = jnp.concatenate([-sin, sin], axis=-1)
    reps = LANE // ATT_HEAD_DIM
    cos_t = jnp.tile(cos_h, (1, reps))
    sin_t = jnp.tile(sin_h, (1, reps))
    pad = nt - seq
    cos_t = jnp.concatenate([cos_t, jnp.ones((pad, LANE), F32)], axis=0)
    sin_t = jnp.concatenate([sin_t, jnp.zeros((pad, LANE), F32)], axis=0)
    return cos_t, sin_t


def _head_mean_matrix(n):
    r = jnp.arange(n) // ATT_HEAD_DIM
    return (r[:, None] == r[None, :]).astype(F32).astype(BF16) * jnp.asarray(1.0 / ATT_HEAD_DIM, BF16)


def kernel(x, c, ctx, c_ctx, w_mod, b_mod, norm_g, ssm_w_in, ssm_conv_w, ssm_conv_b, ssm_dt_bias, ssm_a_log, ssm_d, ssm_norm_g, ssm_w_out, att_w_qkv, att_q_norm, att_k_norm, att_w_out, pool_w_in, pool_w_grp, pool_scale, lru_w_in, lru_conv_w, lru_conv_b, lru_gate_w, lru_gate_b, lru_lambda, lru_w_out, ffn_w_gate, ffn_w_up, ffn_w_down, moe_w_router, moe_w_gate, moe_w_up, moe_w_down):
    b, seq, d = x.shape
    nctx = ctx.shape[1]
    nt = seq + nctx
    depth = w_mod.shape[0]
    n_mixers = 4

    bp = -(-(b + 1) // 8) * 8
    s_rows = jnp.concatenate([c, c_ctx[None, :], jnp.zeros((bp - b - 1, d), F32)], axis=0)
    mods = _mod_vectors(s_rows, w_mod, b_mod)
    modl_all = mods[:, :b].reshape(depth, b, 6, d)
    modc_all = mods[:, b:b + 1].reshape(depth, 1, 6, d)

    hh = jnp.concatenate([x, ctx], axis=1)
    tm_row = _pick(nt, (768, 512, 256))
    cos_t, sin_t = _rope_tables(seq, nt)
    out = None

    for i in range(depth):
        last = i == depth - 1
        kind, j = i % n_mixers, i // n_mixers
        modl, modc = modl_all[i], modc_all[i]
        g1 = norm_g[i, 0][None, :]
        g2 = norm_g[i, 1][None, :]
        proj = functools.partial(_proj, hh, modl, modc, g1, seq=seq, k_shift=0, k_scale=1)

        if kind == 0:
            inner = ssm_d.shape[1] * SSM_HEAD_DIM
            heads = ssm_d.shape[1]
            conv_dim = ssm_conv_w.shape[2]
            w_in = ssm_w_in[j].astype(BF16)
            w_z = w_in[:, :inner]
            w_xbc = w_in[:, inner:inner + conv_dim]
            w_dt = jnp.pad(w_in[:, inner + conv_dim:], ((0, 0), (0, LANE - 2 * heads)))
            dt_bias = jnp.pad(ssm_dt_bias[j].reshape(1, 2 * heads), ((0, 0), (0, LANE - 2 * heads)))
            z = proj(w_z, epi=_epi_plain, tm=tm_row, tn=512, out_dtype=BF16, name="ssd_proj_z")
            xbc = proj(w_xbc, epi=_epi_conv_silu, extra=(ssm_conv_w[j], ssm_conv_b[j][None, :]),
                       extra_specs=(_col_spec(ssm_conv_w.shape[1], 512), _col_spec(1, 512)),
                       tm=nt, tn=512, out_dtype=BF16, name="ssd_proj_xbc")
            dt = proj(w_dt, epi=_epi_softplus, extra=(dt_bias,), extra_specs=(_col_spec(1, LANE),),
                      tm=tm_row, tn=LANE, out_dtype=F32, name="ssd_proj_dt")
            nega = jnp.pad(-jnp.exp(ssm_a_log[j].astype(F32)).reshape(1, 2 * heads), ((0, 0), (0, LANE - 2 * heads)))
            head_of = jnp.arange(inner)[None, :] // SSM_HEAD_DIM
            lanes = jnp.arange(LANE)[:, None]
            expand = jnp.stack([(lanes == head_of + dd * heads) for dd in range(2)]).astype(F32).astype(BF16)
            dskip = jnp.repeat(ssm_d[j].astype(F32), SSM_HEAD_DIM)[None, :]
            yf, yb = _ssd(xbc, dt, nega, dskip, expand, seq=seq, heads=heads)
            hh = _mmres((yf, yb, z), (inner, inner, inner), ssm_w_out[j].astype(BF16), hh, modl, modc,
                        pro=_pro_ssd, tm=tm_row, k_gate=2, seq=seq, name="ssd_out",
                        row_ins=(ssm_norm_g[j][None, :],))
        elif kind == 1:
            nq = att_w_out.shape[1]
            nk = ATT_KV_HEADS * ATT_HEAD_DIM
            w_qkv = att_w_qkv[j].astype(BF16)
            tab_specs = (pl.BlockSpec((tm_row, LANE), lambda bb, ii, jj: (ii, 0)),) * 2

            def qk_extra(n, gain):
                return ((_head_mean_matrix(n), jnp.tile(gain, n // ATT_HEAD_DIM)[None, :], cos_t, sin_t),
                        (pl.BlockSpec((n, n), lambda bb, ii, jj: (0, 0)),
                         pl.BlockSpec((1, n), lambda bb, ii, jj: (0, 0))) + tab_specs)

            ex, sp = qk_extra(nq, att_q_norm[j])
            q = proj(w_qkv[:, :nq], epi=functools.partial(_epi_qk, out_scale=ATT_HEAD_DIM ** -0.5 * math.log2(math.e)),
                     extra=ex,
                     extra_specs=sp, tm=tm_row, tn=nq, out_dtype=BF16, name="att_proj_q")
            ex, sp = qk_extra(nk, att_k_norm[j])
            k = proj(w_qkv[:, nq:nq + nk], epi=functools.partial(_epi_qk, out_scale=1.0), extra=ex,
                     extra_specs=sp, tm=tm_row, tn=nk, out_dtype=BF16, name="att_proj_k")
            v = proj(w_qkv[:, nq + nk:], epi=_epi_plain, tm=tm_row, tn=nk, out_dtype=BF16, name="att_proj_v")
            o = _attention(q, k, v, seq=seq, tq=_pick(nctx, (128,)))
            hh = _mmres((o,), (nq,), att_w_out[j].astype(BF16), hh, modl, modc, pro=_pro_plain, tm=tm_row,
                        k_gate=2, seq=seq, name="att_out")
        elif kind == 2:
            v = proj(pool_w_in[j].astype(BF16), epi=_epi_plain, tm=tm_row, tn=512, out_dtype=F32,
                     name="pool_proj")
            hh = _pool(v, hh, modl, modc, pool_w_grp[j].astype(BF16), pool_scale[j][None, :], seq=seq,
                       tm=_pick(nctx, (256, 128)))
        else:
            width = lru_w_out.shape[1]
            w_in = lru_w_in[j].astype(BF16)
            gg = proj(w_in[:, :width], epi=_epi_gelu, tm=tm_row, tn=256, out_dtype=BF16, name="lru_proj_gate")
            xr = proj(w_in[:, width:], epi=_epi_conv_bias, extra=(lru_conv_w[j], lru_conv_b[j][None, :]),
                      extra_specs=(_col_spec(lru_conv_w.shape[1], 256), _col_spec(1, 256)),
                      tm=nt, tn=256, out_dtype=BF16, name="lru_proj_x")
            gw = lru_gate_w[j]
            w_cat = jnp.concatenate([gw[:, 0], gw[:, 1]], axis=-1).astype(BF16)
            hs = _lru(xr, w_cat, lru_gate_b[j], lru_lambda[j][:, None, :], seq=seq, tc=_pick(nctx, (256, 128)))
            rows = seq if last else nt
            hh = _mmres((hs[0], hs[1], gg), (width, width, width), lru_w_out[j].astype(BF16), hh, modl, modc,
                        pro=_pro_lru, tm=_pick(rows, (1024, 768, 512, 256)), k_gate=2, seq=seq, name="lru_out",
                        rows=rows)

        kf = i // 2
        if i % 2 == 0:
            hh = _dense_ffn(hh, modl, modc, g2, ffn_w_gate[kf].astype(BF16), ffn_w_up[kf].astype(BF16),
                            ffn_w_down[kf].astype(BF16), tm=tm_row, th=512, seq=seq)
        else:
            rows = seq if last else nt
            hh = _moe(hh, modl, modc, g2, moe_w_router[kf], moe_w_gate[kf], moe_w_up[kf], moe_w_down[kf],
                      rows=rows, seq=seq)
        if last:
            out = hh[:, :seq] if hh.shape[1] != seq else hh
    return out
```

```python
import functools
import math

import jax
import jax.numpy as jnp
from jax import lax
from jax.experimental import pallas as pl
from jax.experimental.pallas import tpu as pltpu

F32 = jnp.float32
BF16 = jnp.bfloat16
EPS = 1e-6

GRID_W = 64
SSM_HEAD_DIM = 64
SSM_GROUPS = 4
SSM_STATE = 128
SSM_CHUNK = 128
ATT_HEAD_DIM = 64
ATT_KV_HEADS = 4
ROPE_THETA = 10000.0
POOL_WINDOWS = (2, 4, 8, 16)
LRU_BLOCKS = 10
LRU_C = 8.0
TOP_K = 2
MOE_BLOCK = 1024

LANE = 128
SUBLANES = 8
POOL_HALO = 64
VMEM_LIMIT = 56 * 1024 * 1024


def _cparams(*sem):
    return pltpu.CompilerParams(dimension_semantics=sem, vmem_limit_bytes=VMEM_LIMIT)


def _pick(n, candidates):
    for c in candidates:
        if n % c == 0:
            return c
    raise ValueError(f"no tile in {candidates} divides {n}")


def _sigmoid(x):
    return 0.5 * jnp.tanh(0.5 * x) + 0.5


def _silu(x):
    return x * _sigmoid(x)


def _split2(x):
    hi = x.astype(BF16)
    lo = (x - hi.astype(F32)).astype(BF16)
    return hi, lo


def _split3(x):
    x1 = x.astype(BF16)
    r = x - x1.astype(F32)
    x2 = r.astype(BF16)
    x3 = (r - x2.astype(F32)).astype(BF16)
    return x1, x2, x3


def _dot(a, b):
    return jnp.dot(a, b, preferred_element_type=F32)


def _dot_nt(a, b):
    return lax.dot_general(a, b, (((1,), (1,)), ((), ())), preferred_element_type=F32)


def _dot_tn(a, b):
    return lax.dot_general(a, b, (((0,), (0,)), ((), ())), preferred_element_type=F32)


def _dot_exact_lhs(a_bf16, x_f32):
    x1, x2, x3 = _split3(x_f32)
    return _dot(a_bf16, x1) + _dot(a_bf16, x2) + _dot(a_bf16, x3)


def _modulated(h, modl_ref, modc_ref, g_ref, k_shift, k_scale, row0, seq):
    tm = h.shape[0]
    row = row0 + lax.broadcasted_iota(jnp.int32, (tm, 1), 0)
    is_ctx = row >= seq
    shift = jnp.where(is_ctx, modc_ref[k_shift:k_shift + 1, :], modl_ref[k_shift:k_shift + 1, :])
    scale = jnp.where(is_ctx, modc_ref[k_scale:k_scale + 1, :], modl_ref[k_scale:k_scale + 1, :])
    ms = jnp.mean(h * h, axis=-1, keepdims=True)
    y = h * lax.rsqrt(ms + EPS) * g_ref[...]
    return y * (1.0 + scale) + shift


def _gate_rows(modl_ref, modc_ref, k_gate, tm, row0, seq):
    row = row0 + lax.broadcasted_iota(jnp.int32, (tm, 1), 0)
    return jnp.where(row >= seq, modc_ref[k_gate:k_gate + 1, :], modl_ref[k_gate:k_gate + 1, :])


def _mod_kernel(s_ref, w_ref, b_ref, o_ref):
    s = _silu(s_ref[...])
    o_ref[...] = _dot(s.astype(BF16), w_ref[...].astype(BF16)) + b_ref[...]


def _mod_vectors(s_rows, w_mod, b_mod):
    depth, d, n6 = w_mod.shape
    bp = s_rows.shape[0]
    tn = _pick(n6, (1536, 1024, 512, 256, 128))
    return pl.pallas_call(
        _mod_kernel,
        grid=(depth, n6 // tn),
        in_specs=[pl.BlockSpec((bp, d), lambda l, j: (0, 0)),
                  pl.BlockSpec((None, d, tn), lambda l, j: (l, 0, j)),
                  pl.BlockSpec((None, 1, tn), lambda l, j: (l, 0, j))],
        out_specs=pl.BlockSpec((None, bp, tn), lambda l, j: (l, 0, j)),
        out_shape=jax.ShapeDtypeStruct((depth, bp, n6), F32),
        compiler_params=_cparams("parallel", "parallel"),
        name="mod_vectors",
    )(s_rows, w_mod, b_mod.reshape(depth, 1, n6))


def _seg_conv(x, w_ref, seq):
    nt = x.shape[0]
    row = lax.broadcasted_iota(jnp.int32, (nt, 1), 0)
    in_ctx = row >= seq
    pos = jnp.where(in_ctx, row - seq, row)
    seglen = jnp.where(in_ctx, nt - seq, seq)
    out = x * w_ref[2:3, :]
    for k, off in ((0, -2), (1, -1), (3, 1)):
        shifted = pltpu.roll(x, (-off) % nt, axis=0)
        valid = (pos + off >= 0) & (pos + off < seglen)
        out = out + jnp.where(valid, shifted, 0.0) * w_ref[k:k + 1, :]
    return out


def _epi_plain(acc, extra, row0, seq):
    return acc


def _epi_softplus(acc, extra, row0, seq):
    (b_ref,) = extra
    x = acc + b_ref[...]
    return jnp.maximum(x, 0.0) + jnp.log1p(jnp.exp(-jnp.abs(x)))


def _epi_conv_silu(acc, extra, row0, seq):
    w_ref, b_ref = extra
    return _silu(_seg_conv(acc, w_ref, seq) + b_ref[...])


def _epi_conv_bias(acc, extra, row0, seq):
    w_ref, b_ref = extra
    return _seg_conv(acc, w_ref, seq) + b_ref[...]


def _epi_gelu(acc, extra, row0, seq):
    return jax.nn.gelu(acc)


def _epi_qk(acc, extra, row0, seq, *, out_scale):
    g_ref, gain_ref, cos_ref, sin_ref = extra
    tn = acc.shape[1]
    ms = _dot((acc * acc).astype(BF16), g_ref[...])
    xn = acc * lax.rsqrt(ms + EPS) * gain_ref[...]
    lane = lax.broadcasted_iota(jnp.int32, (1, tn), 1)
    half = ATT_HEAD_DIM // 2
    second = (lane % ATT_HEAD_DIM) >= half
    partner = jnp.where(second, pltpu.roll(xn, half, axis=1), pltpu.roll(xn, tn - half, axis=1))
    reps = tn // LANE
    cos = jnp.concatenate([cos_ref[...]] * reps, axis=1) if reps > 1 else cos_ref[...]
    sin = jnp.concatenate([sin_ref[...]] * reps, axis=1) if reps > 1 else sin_ref[...]
    return (xn * cos + partner * sin) * out_scale


def _proj_kernel(*refs, epi, n_extra, k_shift, k_scale, seq):
    h_ref, modl_ref, modc_ref, g_ref, w_ref = refs[:5]
    extra = refs[5:5 + n_extra]
    o_ref = refs[5 + n_extra]
    u_scr = refs[6 + n_extra]
    tm = h_ref.shape[0]
    row0 = pl.program_id(1) * tm

    @pl.when(pl.program_id(2) == 0)
    def _():
        u_scr[...] = _modulated(h_ref[...], modl_ref, modc_ref, g_ref, k_shift, k_scale, row0, seq).astype(BF16)

    acc = _dot(u_scr[...], w_ref[...])
    o_ref[...] = epi(acc, extra, row0, seq).astype(o_ref.dtype)


def _proj(hh, modl, modc, g, w, *, epi, extra=(), extra_specs=(), tm, tn, out_dtype, seq, k_shift, k_scale,
          name):
    b, nt, d = hh.shape
    n = w.shape[1]
    grid = (b, nt // tm, n // tn)
    in_specs = [pl.BlockSpec((None, tm, d), lambda bb, i, j: (bb, i, 0)),
                pl.BlockSpec((None, 6, d), lambda bb, i, j: (bb, 0, 0)),
                pl.BlockSpec((None, 6, d), lambda bb, i, j: (0, 0, 0)),
                pl.BlockSpec((1, d), lambda bb, i, j: (0, 0)),
                pl.BlockSpec((d, tn), lambda bb, i, j: (0, j))] + list(extra_specs)
    kern = functools.partial(_proj_kernel, epi=epi, n_extra=len(extra), k_shift=k_shift, k_scale=k_scale, seq=seq)
    return pl.pallas_call(
        kern, grid=grid, in_specs=in_specs,
        out_specs=pl.BlockSpec((None, tm, tn), lambda bb, i, j: (bb, i, j)),
        out_shape=jax.ShapeDtypeStruct((b, nt, n), out_dtype),
        scratch_shapes=[pltpu.VMEM((tm, d), BF16)],
        compiler_params=_cparams("parallel", "parallel", "arbitrary"),
        name=name,
    )(hh, modl, modc, g, w, *extra)


def _col_spec(rows, tn):
    return pl.BlockSpec((rows, tn), lambda bb, i, j: (0, j))


def _pro_plain(ins):
    (a_ref,) = ins
    return a_ref[...]


def _pro_ssd(ins):
    yf_ref, yb_ref, z_ref, ng_ref = ins
    y = (yf_ref[...].astype(F32) + yb_ref[...].astype(F32)) * _silu(z_ref[...].astype(F32))
    ms = jnp.mean(y * y, axis=-1, keepdims=True)
    return (y * lax.rsqrt(ms + EPS) * ng_ref[...]).astype(BF16)


def _pro_lru(ins):
    hf_ref, hb_ref, gg_ref = ins
    hsum = (hf_ref[...].astype(F32) + hb_ref[...].astype(F32))
    return (hsum * gg_ref[...].astype(F32)).astype(BF16)


def _mmres_kernel(*refs, pro, n_in, k_gate, seq):
    ins = refs[:n_in]
    w_ref, h_ref, modl_ref, modc_ref, o_ref = refs[n_in:n_in + 5]
    tm = h_ref.shape[0]
    row0 = pl.program_id(1) * tm
    y = _dot(pro(ins), w_ref[...])
    gate = _gate_rows(modl_ref, modc_ref, k_gate, tm, row0, seq)
    o_ref[...] = h_ref[...] + gate * y


def _mmres(ins, in_widths, w, hh, modl, modc, *, pro, tm, k_gate, seq, name, rows=None, row_ins=()):
    b, nt, d = hh.shape
    rows = nt if rows is None else rows
    k = w.shape[0]
    grid = (b, rows // tm)
    in_specs = []
    for item, wd in zip(ins, in_widths):
        if isinstance(item, tuple):
            in_specs.append(pl.BlockSpec((None, None, tm, wd), functools.partial(lambda bb, i, s: (s, bb, i, 0), s=item[1])))
        else:
            in_specs.append(pl.BlockSpec((None, tm, wd), lambda bb, i: (bb, i, 0)))
    ins = [item[0] if isinstance(item, tuple) else item for item in ins]
    in_specs += [pl.BlockSpec((1, r.shape[1]), lambda bb, i: (0, 0)) for r in row_ins]
    in_specs += [pl.BlockSpec((k, d), lambda bb, i: (0, 0)),
                 pl.BlockSpec((None, tm, d), lambda bb, i: (bb, i, 0)),
                 pl.BlockSpec((None, 6, d), lambda bb, i: (bb, 0, 0)),
                 pl.BlockSpec((None, 6, d), lambda bb, i: (0, 0, 0))]
    kern = functools.partial(_mmres_kernel, pro=pro, n_in=len(ins) + len(row_ins), k_gate=k_gate, seq=seq)
    return pl.pallas_call(
        kern, grid=grid, in_specs=in_specs,
        out_specs=pl.BlockSpec((None, tm, d), lambda bb, i: (bb, i, 0)),
        out_shape=jax.ShapeDtypeStruct((b, rows, d), F32),
        compiler_params=_cparams("parallel", "parallel"),
        name=name,
    )(*ins, *row_ins, w, hh, modl, modc)


def _ffn_kernel(h_ref, modl_ref, modc_ref, g_ref, wg_ref, wu_ref, wd_ref, o_ref, u_scr, acc_scr, *, seq):
    tm = h_ref.shape[0]
    row0 = pl.program_id(1) * tm
    j = pl.program_id(2)

    @pl.when(j == 0)
    def _():
        u_scr[...] = _modulated(h_ref[...], modl_ref, modc_ref, g_ref, 3, 4, row0, seq).astype(BF16)
        acc_scr[...] = jnp.zeros_like(acc_scr)

    u = u_scr[...]
    a = (_silu(_dot(u, wg_ref[...])) * _dot(u, wu_ref[...])).astype(BF16)
    acc_scr[...] += _dot(a, wd_ref[...])

    @pl.when(j == pl.num_programs(2) - 1)
    def _():
        gate = _gate_rows(modl_ref, modc_ref, 5, tm, row0, seq)
        o_ref[...] = h_ref[...] + gate * acc_scr[...]


def _dense_ffn(hh, modl, modc, g, wg, wu, wd, *, tm, th, seq):
    b, nt, d = hh.shape
    hid = wg.shape[1]
    grid = (b, nt // tm, hid // th)
    return pl.pallas_call(
        functools.partial(_ffn_kernel, seq=seq),
        grid=grid,
        in_specs=[pl.BlockSpec((None, tm, d), lambda bb, i, j: (bb, i, 0)),
                  pl.BlockSpec((None, 6, d), lambda bb, i, j: (bb, 0, 0)),
                  pl.BlockSpec((None, 6, d), lambda bb, i, j: (0, 0, 0)),
                  pl.BlockSpec((1, d), lambda bb, i, j: (0, 0)),
                  pl.BlockSpec((d, th), lambda bb, i, j: (0, j)),
                  pl.BlockSpec((d, th), lambda bb, i, j: (0, j)),
                  pl.BlockSpec((th, d), lambda bb, i, j: (j, 0))],
        out_specs=pl.BlockSpec((None, tm, d), lambda bb, i, j: (bb, i, 0)),
        out_shape=jax.ShapeDtypeStruct((b, nt, d), F32),
        scratch_shapes=[pltpu.VMEM((tm, d), BF16), pltpu.VMEM((tm, d), F32)],
        compiler_params=_cparams("parallel", "parallel", "arbitrary"),
        name="dense_ffn",
    )(hh, modl, modc, g, wg, wu, wd)


def _sel01(x_f32, e_bf16):
    x1, x2, x3 = _split3(x_f32)
    return _dot(x1, e_bf16) + _dot(x2, e_bf16) + _dot(x3, e_bf16)


def _ssd_chunk(xbc, dt, nega, dskip, expand, lane0, state_ref, o_ref, *, reverse, add_skip):
    q = xbc.shape[0]
    g_n = SSM_GROUPS * SSM_STATE
    inner = xbc.shape[1] - 2 * g_n
    heads = inner // SSM_HEAD_DIM
    rp = inner // SSM_GROUPS
    r_heads = heads // SSM_GROUPS

    rows = lax.broadcasted_iota(jnp.int32, (q, q), 0)
    cols = lax.broadcasted_iota(jnp.int32, (q, q), 1)
    keep = (rows <= cols) if reverse else (rows >= cols)
    tri = keep.astype(BF16)

    la = dt * nega
    ac = _dot_exact_lhs(tri, la)
    ac_t = jnp.transpose(ac)
    dt_t = jnp.transpose(dt)
    last = 0 if reverse else q - 1
    ac_end = ac[last:last + 1, :]
    from_start = jnp.exp(ac)
    w_end = dt * jnp.exp(ac_end - ac)

    x_b = xbc[:, :inner]
    x = x_b.astype(F32)
    xend_b = (x * _dot(w_end.astype(BF16), expand)).astype(BF16)
    total = jnp.broadcast_to(from_start[last:last + 1, :], (2 * SUBLANES, LANE))
    chunk_decay = _sel01(total, expand)[0:1, :]

    for g in range(SSM_GROUPS):
        bg = xbc[:, inner + g * SSM_STATE: inner + (g + 1) * SSM_STATE]
        cg = xbc[:, inner + g_n + g * SSM_STATE: inner + g_n + (g + 1) * SSM_STATE]
        cg_f = cg.astype(F32)
        cb = _dot_nt(cg, bg)
        sl = slice(g * rp, (g + 1) * rp)
        h_in = state_ref[g]
        h_in_b = h_in.astype(BF16)
        pieces = []
        for r in range(r_heads):
            hd = g * r_heads + r
            ln = lane0 + hd
            diff = ac[:, ln:ln + 1] - ac_t[ln:ln + 1, :]
            dec = jnp.where(keep, jnp.exp(jnp.minimum(diff, 0.0)), 0.0)
            m = (cb * dec * dt_t[ln:ln + 1, :]).astype(BF16)
            c_in = (cg_f * from_start[:, ln:ln + 1]).astype(BF16)
            lhs = jnp.concatenate([m, c_in], axis=1)
            ch = slice(hd * SSM_HEAD_DIM, (hd + 1) * SSM_HEAD_DIM)
            rhs = jnp.concatenate([x_b[:, ch], h_in_b[:, r * SSM_HEAD_DIM:(r + 1) * SSM_HEAD_DIM]], axis=0)
            pieces.append(_dot(lhs, rhs))
        y = jnp.concatenate(pieces, axis=1)
        states = _dot_tn(bg, xend_b[:, sl])
        state_ref[g] = h_in * chunk_decay[:, sl] + states
        if add_skip:
            y = y + x[:, sl] * dskip[:, sl]
        o_ref[:, sl] = y.astype(o_ref.dtype)


def _ssd_kernel(xf_ref, xb_ref, dtf_ref, dtb_ref, nega_ref, dskip_ref, exp_ref, yf_ref, yb_ref, st_ref, *, heads):
    @pl.when(pl.program_id(1) == 0)
    def _():
        st_ref[...] = jnp.zeros_like(st_ref)

    _ssd_chunk(xf_ref[...], dtf_ref[...], nega_ref[...], dskip_ref[...], exp_ref[0], 0, st_ref.at[0], yf_ref,
               reverse=False, add_skip=True)
    _ssd_chunk(xb_ref[...], dtb_ref[...], nega_ref[...], dskip_ref[...], exp_ref[1], heads, st_ref.at[1], yb_ref,
               reverse=True, add_skip=False)


def _ssd(xbc, dt, nega, dskip, expand, *, seq, heads):
    b, nt, width = xbc.shape
    q = SSM_CHUNK
    inner = heads * SSM_HEAD_DIM
    nch = nt // q
    ncl = seq // q

    def fwd(bb, j):
        return (bb, (j + ncl) % nch, 0)

    def bwd(bb, j):
        return (bb, nch - 1 - j, 0)

    return pl.pallas_call(
        functools.partial(_ssd_kernel, heads=heads),
        grid=(b, nch),
        in_specs=[pl.BlockSpec((None, q, width), fwd),
                  pl.BlockSpec((None, q, width), bwd),
                  pl.BlockSpec((None, q, LANE), fwd),
                  pl.BlockSpec((None, q, LANE), bwd),
                  pl.BlockSpec((1, LANE), lambda bb, j: (0, 0)),
                  pl.BlockSpec((1, inner), lambda bb, j: (0, 0)),
                  pl.BlockSpec((2, LANE, inner), lambda bb, j: (0, 0, 0))],
        out_specs=[pl.BlockSpec((None, q, inner), fwd),
                   pl.BlockSpec((None, q, inner), bwd)],
        out_shape=[jax.ShapeDtypeStruct((b, nt, inner), BF16)] * 2,
        scratch_shapes=[pltpu.VMEM((2, SSM_GROUPS, SSM_STATE, inner // SSM_GROUPS), F32)],
        compiler_params=_cparams("parallel", "arbitrary"),
        name="ssd_scan",
    )(xbc, xbc, dt, dt, nega, dskip, expand)


def _attn_heads(q, k, v, o_ref):
    hd = ATT_HEAD_DIM
    n_q = q.shape[1] // hd
    rep = n_q // ATT_KV_HEADS
    tq = q.shape[0]
    for g in range(ATT_KV_HEADS):
        kg = k[:, g * hd:(g + 1) * hd]
        vg = v[:, g * hd:(g + 1) * hd]
        qg = jnp.concatenate([q[:, (g * rep + r) * hd:(g * rep + r + 1) * hd] for r in range(rep)], axis=0)
        s_t = _dot_nt(kg, qg)
        nk = s_t.shape[0]
        slab = _pick(nk, (LANE, 2 * SUBLANES))
        m_slab = jnp.max(s_t.reshape(nk // slab, slab, s_t.shape[1]), axis=0)
        m = jnp.max(m_slab, axis=0, keepdims=True)
        p_t = jnp.exp2(s_t - m).astype(BF16)
        v_aug = jnp.concatenate([vg, jnp.ones_like(vg)], axis=1)
        o_aug = _dot_tn(v_aug, p_t)
        o = jnp.transpose(o_aug[:hd, :] / o_aug[hd:hd + 1, :])
        og = jnp.concatenate([o[r * tq:(r + 1) * tq, :] for r in range(rep)], axis=1)
        o_ref[:, g * rep * hd:(g + 1) * rep * hd] = og.astype(o_ref.dtype)


def _attn_kernel(q_ref, k_ref, v_ref, o_ref, *, seq):
    tq = q_ref.shape[0]
    is_lat = pl.program_id(1) * tq < seq

    @pl.when(is_lat)
    def _():
        _attn_heads(q_ref[...], k_ref[...], v_ref[...], o_ref)

    @pl.when(jnp.logical_not(is_lat))
    def _():
        _attn_heads(q_ref[...], k_ref[seq:, :], v_ref[seq:, :], o_ref)


def _attention(q, k, v, *, seq, tq):
    b, nt, dq = q.shape
    dk = k.shape[2]
    return pl.pallas_call(
        functools.partial(_attn_kernel, seq=seq),
        grid=(b, nt // tq),
        in_specs=[pl.BlockSpec((None, tq, dq), lambda bb, i: (bb, i, 0)),
                  pl.BlockSpec((None, nt, dk), lambda bb, i: (bb, 0, 0)),
                  pl.BlockSpec((None, nt, dk), lambda bb, i: (bb, 0, 0))],
        out_specs=pl.BlockSpec((None, tq, dq), lambda bb, i: (bb, i, 0)),
        out_shape=jax.ShapeDtypeStruct((b, nt, dq), BF16),
        compiler_params=_cparams("parallel", "parallel"),
        name="gqa_attention",
    )(q, k, v)


def _pool_kernel(v_ref, vp_ref, vn_ref, h_ref, modl_ref, modc_ref, wg_ref, sc_ref, o_ref, *, seq, nt):
    tm = v_ref.shape[0]
    halo = vp_ref.shape[0]
    i = pl.program_id(1)
    row0 = i * tm
    v = v_ref[...]
    vext = jnp.concatenate([vp_ref[...], v, vn_ref[...]], axis=0)
    vhi, vlo = _split2(vext)
    t = row0 + lax.broadcasted_iota(jnp.int32, (tm, 1), 0)
    s = row0 - halo + lax.broadcasted_iota(jnp.int32, (1, tm + 2 * halo), 1)
    in_ctx = t >= seq
    seg_lo = jnp.where(in_ctx, seq, 0)
    seg_hi = jnp.where(in_ctx, nt, seq)
    grp = v.shape[1] // len(POOL_WINDOWS)
    outs = []
    for gi, win in enumerate(POOL_WINDOWS):
        lo = jnp.maximum(t - win // 2, seg_lo)
        hi = jnp.minimum(t + win // 2, seg_hi)
        band = ((s >= lo) & (s < hi)).astype(BF16)
        sl = slice(gi * grp, (gi + 1) * grp)
        summed = _dot(band, vhi[:, sl]) + _dot(band, vlo[:, sl])
        pooled = summed / (hi - lo).astype(F32) - v[:, sl]
        outs.append(_dot(pooled.astype(BF16), wg_ref[gi]))
    y = jnp.concatenate(outs, axis=1) * sc_ref[...]
    gate = _gate_rows(modl_ref, modc_ref, 2, tm, row0, seq)
    o_ref[...] = h_ref[...] + gate * y


def _pool(v, hh, modl, modc, w_grp, scale, *, seq, tm):
    b, nt, d = hh.shape
    halo = POOL_HALO
    per = tm // halo
    nhb = nt // halo
    grp = w_grp.shape[1]
    return pl.pallas_call(
        functools.partial(_pool_kernel, seq=seq, nt=nt),
        grid=(b, nt // tm),
        in_specs=[pl.BlockSpec((None, tm, d), lambda bb, i: (bb, i, 0)),
                  pl.BlockSpec((None, halo, d), lambda bb, i: (bb, jnp.maximum(i * per - 1, 0), 0)),
                  pl.BlockSpec((None, halo, d), lambda bb, i: (bb, jnp.minimum((i + 1) * per, nhb - 1), 0)),
                  pl.BlockSpec((None, tm, d), lambda bb, i: (bb, i, 0)),
                  pl.BlockSpec((None, 6, d), lambda bb, i: (bb, 0, 0)),
                  pl.BlockSpec((None, 6, d), lambda bb, i: (0, 0, 0)),
                  pl.BlockSpec((len(POOL_WINDOWS), grp, grp), lambda bb, i: (0, 0, 0)),
                  pl.BlockSpec((1, d), lambda bb, i: (0, 0))],
        out_specs=pl.BlockSpec((None, tm, d), lambda bb, i: (bb, i, 0)),
        out_shape=jax.ShapeDtypeStruct((b, nt, d), F32),
        compiler_params=_cparams("parallel", "parallel"),
        name="pool_mixer",
    )(v, v, v, hh, modl, modc, w_grp, scale)


def _lru_scan_chunk(a_scr, bx_scr, h_scr, carry_scr, *, reverse):
    nblk, tc, _ = a_scr.shape

    def step(tt, hs):
        t = tc - 1 - tt if reverse else tt
        out = []
        for k in range(nblk):
            h = a_scr[k, pl.ds(t, 1), :] * hs[k] + bx_scr[k, pl.ds(t, 1), :]
            h_scr[k, pl.ds(t, 1), :] = h
            out.append(h)
        return tuple(out)

    hs = lax.fori_loop(0, tc, step, tuple(carry_scr[k] for k in range(nblk)), unroll=8)
    for k in range(nblk):
        carry_scr[k] = hs[k]


def _lru_kernel(x_ref, w_ref, b_ref, lam_ref, o_ref, a_scr, bx_scr, h_scr, carry_scr):
    d_dir = pl.program_id(1)
    tc, width = x_ref.shape
    bw = width // LRU_BLOCKS

    @pl.when(pl.program_id(2) == 0)
    def _():
        carry_scr[...] = jnp.zeros_like(carry_scr)

    xb = x_ref[...]
    for k in range(LRU_BLOCKS):
        sl = slice(k * bw, (k + 1) * bw)
        pre = _dot(xb[:, sl], w_ref[k])
        r_gate = _sigmoid(pre[:, :bw] + b_ref[0:1, sl])
        i_gate = _sigmoid(pre[:, bw:] + b_ref[1:2, sl])
        neg_lam = -lam_ref[:, sl]
        softplus = jnp.maximum(neg_lam, 0.0) + jnp.log1p(jnp.exp(-jnp.abs(neg_lam)))
        log_a = (-LRU_C) * r_gate * softplus
        a = jnp.exp(log_a)
        a_scr[k] = a
        bx_scr[k] = jnp.sqrt(-jnp.tanh(log_a) * (a * a + 1.0)) * (i_gate * xb[:, sl].astype(F32))

    @pl.when(d_dir == 0)
    def _():
        _lru_scan_chunk(a_scr, bx_scr, h_scr, carry_scr, reverse=False)

    @pl.when(d_dir == 1)
    def _():
        _lru_scan_chunk(a_scr, bx_scr, h_scr, carry_scr, reverse=True)

    for k in range(LRU_BLOCKS):
        o_ref[:, k * bw:(k + 1) * bw] = h_scr[k].astype(o_ref.dtype)


def _lru(xr, w_cat, bias, lam, *, seq, tc):
    b, nt, width = xr.shape
    nc = nt // tc
    nl = seq // tc

    def chunk(d, j):
        return d * (nc - 1 - j) + (1 - d) * ((j + nl) % nc)

    return pl.pallas_call(
        _lru_kernel,
        grid=(b, 2, nc),
        in_specs=[pl.BlockSpec((None, tc, width), lambda bb, d, j: (bb, chunk(d, j), 0)),
                  pl.BlockSpec((None,) + w_cat.shape[1:], lambda bb, d, j: (d, 0, 0, 0)),
                  pl.BlockSpec((None, 2, width), lambda bb, d, j: (d, 0, 0)),
                  pl.BlockSpec((None, 1, width), lambda bb, d, j: (d, 0, 0))],
        out_specs=pl.BlockSpec((None, None, tc, width), lambda bb, d, j: (d, bb, chunk(d, j), 0)),
        out_shape=jax.ShapeDtypeStruct((2, b, nt, width), BF16),
        scratch_shapes=[pltpu.VMEM((LRU_BLOCKS, tc, width // LRU_BLOCKS), F32)] * 3
        + [pltpu.VMEM((LRU_BLOCKS, 1, width // LRU_BLOCKS), F32)],
        compiler_params=_cparams("parallel", "arbitrary", "arbitrary"),
        name="rglru_scan",
    )(xr, w_cat, bias, lam)


def _router_kernel(h_ref, modl_ref, modc_ref, g_ref, whi_ref, wlo_ref, u_ref, meta_ref, cnt_ref, carry_scr, *,
                   seq, n_exp):
    tm = h_ref.shape[0]
    row0 = pl.program_id(1) * tm
    first = (pl.program_id(0) == 0) & (pl.program_id(1) == 0)

    @pl.when(first)
    def _():
        carry_scr[...] = jnp.zeros_like(carry_scr)

    u = _modulated(h_ref[...], modl_ref, modc_ref, g_ref, 3, 4, row0, seq)
    u_ref[...] = u
    uhi, ulo = _split2(u)
    logits = _dot(uhi, whi_ref[...]) + _dot(ulo, whi_ref[...]) + _dot(uhi, wlo_ref[...])
    lane = lax.broadcasted_iota(jnp.int32, (tm, LANE), 1)
    neg = jnp.float32(-jnp.inf)
    logits = jnp.where(lane < n_exp, logits, neg)
    v0 = jnp.max(logits, axis=-1, keepdims=True)
    lane_f = lane.astype(F32)
    i0 = jnp.min(jnp.where(logits == v0, lane_f, float(LANE)), axis=-1, keepdims=True)
    rest = jnp.where(lane_f == i0, neg, logits)
    v1 = jnp.max(rest, axis=-1, keepdims=True)
    i1 = jnp.min(jnp.where(rest == v1, lane_f, float(LANE)), axis=-1, keepdims=True)
    g0 = 1.0 / (1.0 + jnp.exp(v1 - v0))
    g1 = 1.0 - g0
    sel0 = lane_f == i0
    sel1 = lane_f == i1
    onehot = (sel0 | sel1).astype(BF16)
    rr = lax.broadcasted_iota(jnp.int32, (tm, tm), 0)
    cc = lax.broadcasted_iota(jnp.int32, (tm, tm), 1)
    before = (cc < rr).astype(BF16)
    prefix = _dot(before, onehot) + carry_scr[...]
    r0 = jnp.sum(jnp.where(sel0, prefix, 0.0), axis=-1, keepdims=True)
    r1 = jnp.sum(jnp.where(sel1, prefix, 0.0), axis=-1, keepdims=True)
    carry_scr[...] += jnp.sum(onehot.astype(F32), axis=0, keepdims=True)
    cnt_ref[...] = carry_scr[...]
    meta = jnp.where(lane == 0, i0, 0.0)
    meta = jnp.where(lane == 1, i1, meta)
    meta = jnp.where(lane == 2, g0, meta)
    meta = jnp.where(lane == 3, g1, meta)
    meta = jnp.where(lane == 4, r0, meta)
    meta = jnp.where(lane == 5, r1, meta)
    meta_ref[...] = meta


def _router(hh, modl, modc, g, whi, wlo, *, rows, tm, seq, n_exp):
    b, nt, d = hh.shape
    return pl.pallas_call(
        functools.partial(_router_kernel, seq=seq, n_exp=n_exp),
        grid=(b, rows // tm),
        in_specs=[pl.BlockSpec((None, tm, d), lambda bb, i: (bb, i, 0)),
                  pl.BlockSpec((None, 6, d), lambda bb, i: (bb, 0, 0)),
                  pl.BlockSpec((None, 6, d), lambda bb, i: (0, 0, 0)),
                  pl.BlockSpec((1, d), lambda bb, i: (0, 0)),
                  pl.BlockSpec((d, LANE), lambda bb, i: (0, 0)),
                  pl.BlockSpec((d, LANE), lambda bb, i: (0, 0))],
        out_specs=[pl.BlockSpec((None, tm, d), lambda bb, i: (bb, i, 0)),
                   pl.BlockSpec((None, tm, LANE), lambda bb, i: (bb, i, 0)),
                   pl.BlockSpec((1, LANE), lambda bb, i: (0, 0))],
        out_shape=[jax.ShapeDtypeStruct((b, rows, d), F32),
                   jax.ShapeDtypeStruct((b, rows, LANE), F32),
                   jax.ShapeDtypeStruct((1, LANE), F32)],
        scratch_shapes=[pltpu.VMEM((1, LANE), F32)],
        compiler_params=_cparams("arbitrary", "arbitrary"),
        name="moe_router",
    )(hh, modl, modc, g, whi, wlo)


def _dispatch_kernel(dest_ref, fill_ref, u_ref, xg_hbm, zero_scr, sem, *, tm, rows_per_b):
    base = (pl.program_id(0) * rows_per_b + pl.program_id(1) * tm) * TOP_K

    @pl.when((pl.program_id(0) == 0) & (pl.program_id(1) == 0))
    def _():
        zero_scr[...] = jnp.zeros_like(zero_scr)

        def fill(blk, c):
            @pl.when(fill_ref[blk] != 0)
            def _():
                row0 = pl.multiple_of(blk * MOE_BLOCK, MOE_BLOCK)
                cp = pltpu.make_async_copy(zero_scr, xg_hbm.at[pl.ds(row0, MOE_BLOCK)], sem)
                cp.start()
                cp.wait()
            return c

        lax.fori_loop(0, fill_ref.shape[0], fill, 0)

    def issue(r, c):
        for k in range(TOP_K):
            dst = xg_hbm.at[pl.ds(dest_ref[base + r * TOP_K + k], 1)]
            pltpu.make_async_copy(u_ref.at[pl.ds(r, 1)], dst, sem).start()
        return c

    lax.fori_loop(0, tm, issue, 0, unroll=8)
    for k in range(TOP_K):
        pltpu.make_async_copy(u_ref, xg_hbm.at[pl.ds(0, tm)], sem).wait()


def _dispatch(dest, fill, u, *, tm):
    b, rows, d = u.shape
    grid_spec = pltpu.PrefetchScalarGridSpec(
        num_scalar_prefetch=2, grid=(b, rows // tm),
        in_specs=[pl.BlockSpec((tm, d), lambda bb, i, ds, fl: (bb * (rows // tm) + i, 0))],
        out_specs=pl.BlockSpec(memory_space=pl.ANY),
        scratch_shapes=[pltpu.VMEM((MOE_BLOCK, d), F32), pltpu.SemaphoreType.DMA(())])
    return pl.pallas_call(
        functools.partial(_dispatch_kernel, tm=tm, rows_per_b=rows),
        grid_spec=grid_spec,
        out_shape=jax.ShapeDtypeStruct((fill.shape[0] * MOE_BLOCK, d), F32),
        compiler_params=_cparams("arbitrary", "arbitrary"),
        name="moe_dispatch",
    )(dest, fill, u.reshape(b * rows, d))


def _expert_kernel(be_ref, nu_ref, x_ref, wg_ref, wu_ref, wd_ref, y_ref, xb_scr, acc_scr):
    blk = pl.program_id(0)
    j = pl.program_id(1)
    used = blk < nu_ref[0]

    @pl.when(used & (j == 0))
    def _():
        xb_scr[...] = x_ref[...].astype(BF16)
        acc_scr[...] = jnp.zeros_like(acc_scr)

    @pl.when(used)
    def _():
        x = xb_scr[...]
        a = (_silu(_dot(x, wg_ref[...].astype(BF16))) * _dot(x, wu_ref[...].astype(BF16))).astype(BF16)
        acc_scr[...] += _dot(a, wd_ref[...].astype(BF16))

    @pl.when(used & (j == pl.num_programs(1) - 1))
    def _():
        y_ref[...] = acc_scr[...]

    @pl.when(jnp.logical_not(used) & (j == pl.num_programs(1) - 1))
    def _():
        y_ref[...] = jnp.zeros_like(y_ref)


def _experts(block_expert, n_used, xg, wg, wu, wd, *, th, layer):
    p, d = xg.shape
    hid = wg.shape[3]
    nb = p // MOE_BLOCK
    nj = hid // th

    def blk_eff(blk, nu):
        return jnp.minimum(blk, nu[0] - 1)

    def j_eff(blk, j, nu):
        return jnp.where(blk < nu[0], j, nj - 1)

    def w_in_map(blk, j, be, nu):
        return (layer, be[blk_eff(blk, nu)], 0, j_eff(blk, j, nu))

    def w_out_map(blk, j, be, nu):
        return (layer, be[blk_eff(blk, nu)], j_eff(blk, j, nu), 0)

    grid_spec = pltpu.PrefetchScalarGridSpec(
        num_scalar_prefetch=2, grid=(nb, nj),
        in_specs=[pl.BlockSpec((MOE_BLOCK, d), lambda blk, j, be, nu: (blk_eff(blk, nu), 0)),
                  pl.BlockSpec((None, None, d, th), w_in_map),
                  pl.BlockSpec((None, None, d, th), w_in_map),
                  pl.BlockSpec((None, None, th, d), w_out_map)],
        out_specs=pl.BlockSpec((MOE_BLOCK, d), lambda blk, j, be, nu: (blk, 0)),
        scratch_shapes=[pltpu.VMEM((MOE_BLOCK, d), BF16), pltpu.VMEM((MOE_BLOCK, d), F32)])
    return pl.pallas_call(
        _expert_kernel, grid_spec=grid_spec,
        out_shape=jax.ShapeDtypeStruct((p, d), F32),
        compiler_params=_cparams("arbitrary", "arbitrary"),
        name="moe_experts",
    )(block_expert, n_used, xg, wg, wu, wd)


def _combine_kernel(dest_ref, yg_hbm, meta_ref, h_ref, modl_ref, modc_ref, o_ref, y0_scr, y1_scr, sem, *, tm,
                    rows_per_b, seq):
    bb = pl.program_id(0)
    i = pl.program_id(1)
    base = (bb * rows_per_b + i * tm) * TOP_K
    bufs = (y0_scr, y1_scr)

    def issue(r, c):
        for k in range(TOP_K):
            src = yg_hbm.at[pl.ds(dest_ref[base + r * TOP_K + k], 1)]
            pltpu.make_async_copy(src, bufs[k].at[pl.ds(r, 1)], sem).start()
        return c

    lax.fori_loop(0, tm, issue, 0, unroll=8)
    for k in range(TOP_K):
        pltpu.make_async_copy(yg_hbm.at[pl.ds(0, tm)], bufs[k], sem).wait()
    meta = meta_ref[...]
    y = meta[:, 2:3] * y0_scr[...] + meta[:, 3:4] * y1_scr[...]
    gate = _gate_rows(modl_ref, modc_ref, 5, tm, i * tm, seq)
    o_ref[...] = h_ref[...] + gate * y


def _combine(dest, yg, meta, hh, modl, modc, *, tm, seq):
    b, rows, _ = meta.shape
    d = hh.shape[2]
    grid_spec = pltpu.PrefetchScalarGridSpec(
        num_scalar_prefetch=1, grid=(b, rows // tm),
        in_specs=[pl.BlockSpec(memory_space=pl.ANY),
                  pl.BlockSpec((None, tm, LANE), lambda bb, i, ds: (bb, i, 0)),
                  pl.BlockSpec((None, tm, d), lambda bb, i, ds: (bb, i, 0)),
                  pl.BlockSpec((None, 6, d), lambda bb, i, ds: (bb, 0, 0)),
                  pl.BlockSpec((None, 6, d), lambda bb, i, ds: (0, 0, 0))],
        out_specs=pl.BlockSpec((None, tm, d), lambda bb, i, ds: (bb, i, 0)),
        scratch_shapes=[pltpu.VMEM((tm, d), F32), pltpu.VMEM((tm, d), F32), pltpu.SemaphoreType.DMA(())])
    return pl.pallas_call(
        functools.partial(_combine_kernel, tm=tm, rows_per_b=rows, seq=seq),
        grid_spec=grid_spec,
        out_shape=jax.ShapeDtypeStruct((b, rows, d), F32),
        compiler_params=_cparams("arbitrary", "arbitrary"),
        name="moe_combine",
    )(dest, yg, meta, hh, modl, modc)


def _moe(hh, modl, modc, g, w_router, wg, wu, wd, *, rows, seq, layer):
    b, nt, d = hh.shape
    n_exp = w_router.shape[1]
    wr = jnp.pad(w_router, ((0, 0), (0, LANE - n_exp)))
    whi, wlo = _split2(wr)
    tm = _pick(rows, (768, 512, 256))
    u, meta, counts = _router(hh, modl, modc, g, whi, wlo, rows=rows, tm=tm, seq=seq, n_exp=n_exp)

    cnt = counts[0, :n_exp].astype(jnp.int32)
    padded = (cnt + MOE_BLOCK - 1) // MOE_BLOCK * MOE_BLOCK
    pend = jnp.cumsum(padded)
    pstart = pend - padded
    n_assign = b * rows * TOP_K
    nb = -(-(n_assign + n_exp * (MOE_BLOCK - 1)) // MOE_BLOCK)
    idx = meta[:, :, 0:TOP_K].astype(jnp.int32)
    rank = meta[:, :, 4:4 + TOP_K].astype(jnp.int32)
    dest = (sum(jnp.where(idx == e, pstart[e], 0) for e in range(n_exp)) + rank).reshape(-1)
    blk_start = jnp.arange(nb, dtype=jnp.int32) * MOE_BLOCK
    block_expert = jnp.minimum(jnp.sum(pend[None, :] <= blk_start[:, None], axis=1), n_exp - 1).astype(jnp.int32)
    n_used = (pend[-1:] // MOE_BLOCK).astype(jnp.int32)

    blk_id = jnp.arange(nb, dtype=jnp.int32)
    holds_padding = jnp.any((pend[None, :] // MOE_BLOCK - 1 == blk_id[:, None]) & (padded[None, :] > 0), axis=1)
    fill = (holds_padding | (blk_id >= n_used[0])).astype(jnp.int32)
    td = tm
    xg = _dispatch(dest, fill, u, tm=td)
    yg = _experts(block_expert, n_used, xg, wg, wu, wd, th=_pick(wg.shape[3], (512, 256, 128)), layer=layer)
    return _combine(dest, yg, meta, hh, modl, modc, tm=td, seq=seq)


def _rope_tables(seq, nt):
    rows = seq // GRID_W
    row = jnp.repeat(jnp.arange(rows), GRID_W).astype(F32)
    col = jnp.tile(jnp.arange(GRID_W), rows).astype(F32)
    n_freq = ATT_HEAD_DIM // 4
    inv_freq = ROPE_THETA ** (-jnp.arange(n_freq, dtype=F32) / n_freq)
    ang = jnp.concatenate([row[:, None] * inv_freq, col[:, None] * inv_freq], axis=-1)
    cos, sin = jnp.cos(ang), jnp.sin(ang)
    cos_h = jnp.concatenate([cos, cos], axis=-1)
    sin_h = jnp.concatenate([-sin, sin], axis=-1)
    reps = LANE // ATT_HEAD_DIM
    cos_t = jnp.tile(cos_h, (1, reps))
    sin_t = jnp.tile(sin_h, (1, reps))
    pad = nt - seq
    cos_t = jnp.concatenate([cos_t, jnp.ones((pad, LANE), F32)], axis=0)
    sin_t = jnp.concatenate([sin_t, jnp.zeros((pad, LANE), F32)], axis=0)
    return cos_t, sin_t


def _head_mean_matrix(n):
    r = jnp.arange(n) // ATT_HEAD_DIM
    return (r[:, None] == r[None, :]).astype(F32).astype(BF16) * jnp.asarray(1.0 / ATT_HEAD_DIM, BF16)


def kernel(x, c, ctx, c_ctx, w_mod, b_mod, norm_g, ssm_w_in, ssm_conv_w, ssm_conv_b, ssm_dt_bias, ssm_a_log, ssm_d, ssm_norm_g, ssm_w_out, att_w_qkv, att_q_norm, att_k_norm, att_w_out, pool_w_in, pool_w_grp, pool_scale, lru_w_in, lru_conv_w, lru_conv_b, lru_gate_w, lru_gate_b, lru_lambda, lru_w_out, ffn_w_gate, ffn_w_up, ffn_w_down, moe_w_router, moe_w_gate, moe_w_up, moe_w_down):
    b, seq, d = x.shape
    nctx = ctx.shape[1]
    nt = seq + nctx
    depth = w_mod.shape[0]
    n_mixers = 4

    bp = -(-(b + 1) // 8) * 8
    s_rows = jnp.concatenate([c, c_ctx[None, :], jnp.zeros((bp - b - 1, d), F32)], axis=0)
    mods = _mod_vectors(s_rows, w_mod, b_mod)
    modl_all = mods[:, :b].reshape(depth, b, 6, d)
    modc_all = mods[:, b:b + 1].reshape(depth, 1, 6, d)

    hh = jnp.concatenate([x, ctx], axis=1)
    tm_row = _pick(nt, (768, 512, 256))
    cos_t, sin_t = _rope_tables(seq, nt)
    out = None

    for i in range(depth):
        last = i == depth - 1
        kind, j = i % n_mixers, i // n_mixers
        modl, modc = modl_all[i], modc_all[i]
        g1 = norm_g[i, 0][None, :]
        g2 = norm_g[i, 1][None, :]
        proj = functools.partial(_proj, hh, modl, modc, g1, seq=seq, k_shift=0, k_scale=1)

        if kind == 0:
            inner = ssm_d.shape[1] * SSM_HEAD_DIM
            heads = ssm_d.shape[1]
            conv_dim = ssm_conv_w.shape[2]
            w_in = ssm_w_in[j].astype(BF16)
            w_z = w_in[:, :inner]
            w_xbc = w_in[:, inner:inner + conv_dim]
            w_dt = jnp.pad(w_in[:, inner + conv_dim:], ((0, 0), (0, LANE - 2 * heads)))
            dt_bias = jnp.pad(ssm_dt_bias[j].reshape(1, 2 * heads), ((0, 0), (0, LANE - 2 * heads)))
            z = proj(w_z, epi=_epi_plain, tm=tm_row, tn=inner, out_dtype=BF16, name="ssd_proj_z")
            xbc = proj(w_xbc, epi=_epi_conv_silu, extra=(ssm_conv_w[j], ssm_conv_b[j][None, :]),
                       extra_specs=(_col_spec(ssm_conv_w.shape[1], 512), _col_spec(1, 512)),
                       tm=nt, tn=512, out_dtype=BF16, name="ssd_proj_xbc")
            dt = proj(w_dt, epi=_epi_softplus, extra=(dt_bias,), extra_specs=(_col_spec(1, LANE),),
                      tm=tm_row, tn=LANE, out_dtype=F32, name="ssd_proj_dt")
            nega = jnp.pad(-jnp.exp(ssm_a_log[j].astype(F32)).reshape(1, 2 * heads), ((0, 0), (0, LANE - 2 * heads)))
            head_of = jnp.arange(inner)[None, :] // SSM_HEAD_DIM
            lanes = jnp.arange(LANE)[:, None]
            expand = jnp.stack([(lanes == head_of + dd * heads) for dd in range(2)]).astype(F32).astype(BF16)
            dskip = jnp.repeat(ssm_d[j].astype(F32), SSM_HEAD_DIM)[None, :]
            yf, yb = _ssd(xbc, dt, nega, dskip, expand, seq=seq, heads=heads)
            hh = _mmres((yf, yb, z), (inner, inner, inner), ssm_w_out[j].astype(BF16), hh, modl, modc,
                        pro=_pro_ssd, tm=tm_row, k_gate=2, seq=seq, name="ssd_out",
                        row_ins=(ssm_norm_g[j][None, :],))
        elif kind == 1:
            nq = att_w_out.shape[1]
            nk = ATT_KV_HEADS * ATT_HEAD_DIM
            w_qkv = att_w_qkv[j].astype(BF16)
            tab_specs = (pl.BlockSpec((tm_row, LANE), lambda bb, ii, jj: (ii, 0)),) * 2

            def qk_extra(n, gain):
                return ((_head_mean_matrix(n), jnp.tile(gain, n // ATT_HEAD_DIM)[None, :], cos_t, sin_t),
                        (pl.BlockSpec((n, n), lambda bb, ii, jj: (0, 0)),
                         pl.BlockSpec((1, n), lambda bb, ii, jj: (0, 0))) + tab_specs)

            ex, sp = qk_extra(nq, att_q_norm[j])
            q = proj(w_qkv[:, :nq], epi=functools.partial(_epi_qk, out_scale=ATT_HEAD_DIM ** -0.5 * math.log2(math.e)),
                     extra=ex,
                     extra_specs=sp, tm=tm_row, tn=nq, out_dtype=BF16, name="att_proj_q")
            ex, sp = qk_extra(nk, att_k_norm[j])
            k = proj(w_qkv[:, nq:nq + nk], epi=functools.partial(_epi_qk, out_scale=1.0), extra=ex,
                     extra_specs=sp, tm=tm_row, tn=nk, out_dtype=BF16, name="att_proj_k")
            v = proj(w_qkv[:, nq + nk:], epi=_epi_plain, tm=tm_row, tn=nk, out_dtype=BF16, name="att_proj_v")
            o = _attention(q, k, v, seq=seq, tq=_pick(nctx, (128,)))
            hh = _mmres((o,), (nq,), att_w_out[j].astype(BF16), hh, modl, modc, pro=_pro_plain, tm=tm_row,
                        k_gate=2, seq=seq, name="att_out")
        elif kind == 2:
            v = proj(pool_w_in[j].astype(BF16), epi=_epi_plain, tm=tm_row, tn=d, out_dtype=F32,
                     name="pool_proj")
            hh = _pool(v, hh, modl, modc, pool_w_grp[j].astype(BF16), pool_scale[j][None, :], seq=seq,
                       tm=_pick(nctx, (256, 128)))
        else:
            width = lru_w_out.shape[1]
            w_in = lru_w_in[j].astype(BF16)
            gg = proj(w_in[:, :width], epi=_epi_gelu, tm=tm_row, tn=width, out_dtype=BF16, name="lru_proj_gate")
            tn_x = _pick(width, (640, 256, 128))
            xr = proj(w_in[:, width:], epi=_epi_conv_bias, extra=(lru_conv_w[j], lru_conv_b[j][None, :]),
                      extra_specs=(_col_spec(lru_conv_w.shape[1], tn_x), _col_spec(1, tn_x)),
                      tm=nt, tn=tn_x, out_dtype=BF16, name="lru_proj_x")
            gw = lru_gate_w[j]
            w_cat = jnp.concatenate([gw[:, 0], gw[:, 1]], axis=-1).astype(BF16)
            hs = _lru(xr, w_cat, lru_gate_b[j], lru_lambda[j][:, None, :], seq=seq, tc=_pick(nctx, (256, 128)))
            rows = seq if last else nt
            hh = _mmres(((hs, 0), (hs, 1), gg), (width, width, width), lru_w_out[j].astype(BF16), hh, modl, modc,
                        pro=_pro_lru, tm=_pick(rows, (1024, 768, 512, 256)), k_gate=2, seq=seq, name="lru_out",
                        rows=rows)

        kf = i // 2
        if i % 2 == 0:
            hh = _dense_ffn(hh, modl, modc, g2, ffn_w_gate[kf].astype(BF16), ffn_w_up[kf].astype(BF16),
                            ffn_w_down[kf].astype(BF16), tm=tm_row,
                            th=_pick(ffn_w_gate.shape[2], (896, 512, 256, 128)), seq=seq)
        else:
            rows = seq if last else nt
            hh = _moe(hh, modl, modc, g2, moe_w_router[kf], moe_w_gate, moe_w_up, moe_w_down, rows=rows, seq=seq,
                      layer=kf)
        if last:
            out = hh[:, :seq] if hh.shape[1] != seq else hh
    return out
```

```python
import functools
import math

import jax
import jax.numpy as jnp
from jax import lax
from jax.experimental import pallas as pl
from jax.experimental.pallas import tpu as pltpu

F32 = jnp.float32
BF16 = jnp.bfloat16
EPS = 1e-6

GRID_W = 64
SSM_HEAD_DIM = 64
SSM_GROUPS = 4
SSM_STATE = 128
SSM_CHUNK = 128
ATT_HEAD_DIM = 64
ATT_KV_HEADS = 4
ROPE_THETA = 10000.0
POOL_WINDOWS = (2, 4, 8, 16)
LRU_BLOCKS = 10
LRU_C = 8.0
TOP_K = 2
MOE_BLOCK = 1024

LANE = 128
SUBLANES = 8
POOL_HALO = 64
VMEM_LIMIT = 56 * 1024 * 1024


def _cparams(*sem):
    return pltpu.CompilerParams(dimension_semantics=sem, vmem_limit_bytes=VMEM_LIMIT)


def _pick(n, candidates):
    for c in candidates:
        if n % c == 0:
            return c
    raise ValueError(f"no tile in {candidates} divides {n}")


def _sigmoid(x):
    return 0.5 * jnp.tanh(0.5 * x) + 0.5


def _silu(x):
    return x * _sigmoid(x)


def _split2(x):
    hi = x.astype(BF16)
    lo = (x - hi.astype(F32)).astype(BF16)
    return hi, lo


def _split3(x):
    x1 = x.astype(BF16)
    r = x - x1.astype(F32)
    x2 = r.astype(BF16)
    x3 = (r - x2.astype(F32)).astype(BF16)
    return x1, x2, x3


def _dot(a, b):
    return jnp.dot(a, b, preferred_element_type=F32)


def _dot_nt(a, b):
    return lax.dot_general(a, b, (((1,), (1,)), ((), ())), preferred_element_type=F32)


def _dot_tn(a, b):
    return lax.dot_general(a, b, (((0,), (0,)), ((), ())), preferred_element_type=F32)


def _dot_exact_lhs(a_bf16, x_f32):
    x1, x2, x3 = _split3(x_f32)
    return _dot(a_bf16, x1) + _dot(a_bf16, x2) + _dot(a_bf16, x3)


def _modulated(h, modl_ref, modc_ref, g_ref, k_shift, k_scale, row0, seq):
    tm = h.shape[0]
    row = row0 + lax.broadcasted_iota(jnp.int32, (tm, 1), 0)
    is_ctx = row >= seq
    shift = jnp.where(is_ctx, modc_ref[k_shift:k_shift + 1, :], modl_ref[k_shift:k_shift + 1, :])
    scale = jnp.where(is_ctx, modc_ref[k_scale:k_scale + 1, :], modl_ref[k_scale:k_scale + 1, :])
    ms = jnp.mean(h * h, axis=-1, keepdims=True)
    y = h * lax.rsqrt(ms + EPS) * g_ref[...]
    return y * (1.0 + scale) + shift


def _gate_rows(modl_ref, modc_ref, k_gate, tm, row0, seq):
    row = row0 + lax.broadcasted_iota(jnp.int32, (tm, 1), 0)
    return jnp.where(row >= seq, modc_ref[k_gate:k_gate + 1, :], modl_ref[k_gate:k_gate + 1, :])


def _mod_kernel(s_ref, w_ref, b_ref, o_ref):
    s = _silu(s_ref[...])
    o_ref[...] = _dot(s.astype(BF16), w_ref[...].astype(BF16)) + b_ref[...]


def _mod_vectors(s_rows, w_mod, b_mod):
    depth, d, n6 = w_mod.shape
    bp = s_rows.shape[0]
    tn = _pick(n6, (1536, 1024, 512, 256, 128))
    return pl.pallas_call(
        _mod_kernel,
        grid=(depth, n6 // tn),
        in_specs=[pl.BlockSpec((bp, d), lambda l, j: (0, 0)),
                  pl.BlockSpec((None, d, tn), lambda l, j: (l, 0, j)),
                  pl.BlockSpec((None, 1, tn), lambda l, j: (l, 0, j))],
        out_specs=pl.BlockSpec((None, bp, tn), lambda l, j: (l, 0, j)),
        out_shape=jax.ShapeDtypeStruct((depth, bp, n6), F32),
        compiler_params=_cparams("parallel", "parallel"),
        name="mod_vectors",
    )(s_rows, w_mod, b_mod.reshape(depth, 1, n6))


def _seg_conv(x, w_ref, seq):
    nt = x.shape[0]
    row = lax.broadcasted_iota(jnp.int32, (nt, 1), 0)
    in_ctx = row >= seq
    pos = jnp.where(in_ctx, row - seq, row)
    seglen = jnp.where(in_ctx, nt - seq, seq)
    out = x * w_ref[2:3, :]
    for k, off in ((0, -2), (1, -1), (3, 1)):
        shifted = pltpu.roll(x, (-off) % nt, axis=0)
        valid = (pos + off >= 0) & (pos + off < seglen)
        out = out + jnp.where(valid, shifted, 0.0) * w_ref[k:k + 1, :]
    return out


def _epi_plain(acc, extra, row0, seq):
    return acc


def _epi_softplus(acc, extra, row0, seq):
    (b_ref,) = extra
    x = acc + b_ref[...]
    return jnp.maximum(x, 0.0) + jnp.log1p(jnp.exp(-jnp.abs(x)))


def _epi_conv_silu(acc, extra, row0, seq):
    w_ref, b_ref = extra
    return _silu(_seg_conv(acc, w_ref, seq) + b_ref[...])


def _epi_conv_bias(acc, extra, row0, seq):
    w_ref, b_ref = extra
    return _seg_conv(acc, w_ref, seq) + b_ref[...]


def _epi_gelu(acc, extra, row0, seq):
    return jax.nn.gelu(acc)


def _epi_qk(acc, extra, row0, seq, *, out_scale):
    g_ref, gain_ref, cos_ref, sin_ref = extra
    tn = acc.shape[1]
    ms = _dot((acc * acc).astype(BF16), g_ref[...])
    xn = acc * lax.rsqrt(ms + EPS) * gain_ref[...]
    lane = lax.broadcasted_iota(jnp.int32, (1, tn), 1)
    half = ATT_HEAD_DIM // 2
    second = (lane % ATT_HEAD_DIM) >= half
    partner = jnp.where(second, pltpu.roll(xn, half, axis=1), pltpu.roll(xn, tn - half, axis=1))
    reps = tn // LANE
    cos = jnp.concatenate([cos_ref[...]] * reps, axis=1) if reps > 1 else cos_ref[...]
    sin = jnp.concatenate([sin_ref[...]] * reps, axis=1) if reps > 1 else sin_ref[...]
    return (xn * cos + partner * sin) * out_scale


def _proj_kernel(*refs, epi, n_extra, k_shift, k_scale, seq):
    h_ref, modl_ref, modc_ref, g_ref, w_ref = refs[:5]
    extra = refs[5:5 + n_extra]
    o_ref = refs[5 + n_extra]
    u_scr = refs[6 + n_extra]
    tm = h_ref.shape[0]
    row0 = pl.program_id(1) * tm

    @pl.when(pl.program_id(2) == 0)
    def _():
        u_scr[...] = _modulated(h_ref[...], modl_ref, modc_ref, g_ref, k_shift, k_scale, row0, seq).astype(BF16)

    acc = _dot(u_scr[...], w_ref[...])
    o_ref[...] = epi(acc, extra, row0, seq).astype(o_ref.dtype)


def _proj(hh, modl, modc, g, w, *, epi, extra=(), extra_specs=(), tm, tn, out_dtype, seq, k_shift, k_scale,
          name):
    b, nt, d = hh.shape
    n = w.shape[1]
    grid = (b, nt // tm, n // tn)
    in_specs = [pl.BlockSpec((None, tm, d), lambda bb, i, j: (bb, i, 0)),
                pl.BlockSpec((None, 6, d), lambda bb, i, j: (bb, 0, 0)),
                pl.BlockSpec((None, 6, d), lambda bb, i, j: (0, 0, 0)),
                pl.BlockSpec((1, d), lambda bb, i, j: (0, 0)),
                pl.BlockSpec((d, tn), lambda bb, i, j: (0, j))] + list(extra_specs)
    kern = functools.partial(_proj_kernel, epi=epi, n_extra=len(extra), k_shift=k_shift, k_scale=k_scale, seq=seq)
    return pl.pallas_call(
        kern, grid=grid, in_specs=in_specs,
        out_specs=pl.BlockSpec((None, tm, tn), lambda bb, i, j: (bb, i, j)),
        out_shape=jax.ShapeDtypeStruct((b, nt, n), out_dtype),
        scratch_shapes=[pltpu.VMEM((tm, d), BF16)],
        compiler_params=_cparams("parallel", "parallel", "arbitrary"),
        name=name,
    )(hh, modl, modc, g, w, *extra)


def _col_spec(rows, tn):
    return pl.BlockSpec((rows, tn), lambda bb, i, j: (0, j))


def _pro_plain(ins):
    (a_ref,) = ins
    return a_ref[...]


def _pro_ssd(ins):
    yf_ref, yb_ref, z_ref, ng_ref = ins
    y = (yf_ref[...].astype(F32) + yb_ref[...].astype(F32)) * _silu(z_ref[...].astype(F32))
    ms = jnp.mean(y * y, axis=-1, keepdims=True)
    return (y * lax.rsqrt(ms + EPS) * ng_ref[...]).astype(BF16)


def _pro_lru(ins):
    hf_ref, hb_ref, gg_ref = ins
    hsum = (hf_ref[...].astype(F32) + hb_ref[...].astype(F32))
    return (hsum * gg_ref[...].astype(F32)).astype(BF16)


def _mmres_kernel(*refs, pro, n_in, k_gate, seq):
    ins = refs[:n_in]
    w_ref, h_ref, modl_ref, modc_ref, o_ref = refs[n_in:n_in + 5]
    tm = h_ref.shape[0]
    row0 = pl.program_id(1) * tm
    y = _dot(pro(ins), w_ref[...])
    gate = _gate_rows(modl_ref, modc_ref, k_gate, tm, row0, seq)
    o_ref[...] = h_ref[...] + gate * y


def _mmres(ins, in_widths, w, hh, modl, modc, *, pro, tm, k_gate, seq, name, rows=None, row_ins=()):
    b, nt, d = hh.shape
    rows = nt if rows is None else rows
    k = w.shape[0]
    grid = (b, rows // tm)
    in_specs = []
    for item, wd in zip(ins, in_widths):
        if isinstance(item, tuple):
            in_specs.append(pl.BlockSpec((None, None, tm, wd), functools.partial(lambda bb, i, s: (s, bb, i, 0), s=item[1])))
        else:
            in_specs.append(pl.BlockSpec((None, tm, wd), lambda bb, i: (bb, i, 0)))
    ins = [item[0] if isinstance(item, tuple) else item for item in ins]
    in_specs += [pl.BlockSpec((1, r.shape[1]), lambda bb, i: (0, 0)) for r in row_ins]
    in_specs += [pl.BlockSpec((k, d), lambda bb, i: (0, 0)),
                 pl.BlockSpec((None, tm, d), lambda bb, i: (bb, i, 0)),
                 pl.BlockSpec((None, 6, d), lambda bb, i: (bb, 0, 0)),
                 pl.BlockSpec((None, 6, d), lambda bb, i: (0, 0, 0))]
    kern = functools.partial(_mmres_kernel, pro=pro, n_in=len(ins) + len(row_ins), k_gate=k_gate, seq=seq)
    return pl.pallas_call(
        kern, grid=grid, in_specs=in_specs,
        out_specs=pl.BlockSpec((None, tm, d), lambda bb, i: (bb, i, 0)),
        out_shape=jax.ShapeDtypeStruct((b, rows, d), F32),
        compiler_params=_cparams("parallel", "parallel"),
        name=name,
    )(*ins, *row_ins, w, hh, modl, modc)


SWIGLU_PIECES = 2


def _swiglu_tile(x, wg_ref, wu_ref, wd_ref):
    th = wg_ref.shape[1]
    sub = th // SWIGLU_PIECES
    cols = [slice(s * sub, (s + 1) * sub) for s in range(SWIGLU_PIECES)]
    gates = [_dot(x, wg_ref[:, c].astype(BF16)) for c in cols]
    ups = [_dot(x, wu_ref[:, c].astype(BF16)) for c in cols]
    out = None
    for c, gt, up in zip(cols, gates, ups):
        part = _dot((_silu(gt) * up).astype(BF16), wd_ref[c, :].astype(BF16))
        out = part if out is None else out + part
    return out


def _ffn_kernel(h_ref, modl_ref, modc_ref, g_ref, wg_ref, wu_ref, wd_ref, o_ref, u_scr, acc_scr, *, seq):
    tm = h_ref.shape[0]
    row0 = pl.program_id(1) * tm
    j = pl.program_id(2)

    @pl.when(j == 0)
    def _():
        u_scr[...] = _modulated(h_ref[...], modl_ref, modc_ref, g_ref, 3, 4, row0, seq).astype(BF16)
        acc_scr[...] = jnp.zeros_like(acc_scr)

    acc_scr[...] += _swiglu_tile(u_scr[...], wg_ref, wu_ref, wd_ref)

    @pl.when(j == pl.num_programs(2) - 1)
    def _():
        gate = _gate_rows(modl_ref, modc_ref, 5, tm, row0, seq)
        o_ref[...] = h_ref[...] + gate * acc_scr[...]


def _dense_ffn(hh, modl, modc, g, wg, wu, wd, *, tm, th, seq):
    b, nt, d = hh.shape
    hid = wg.shape[1]
    grid = (b, nt // tm, hid // th)
    return pl.pallas_call(
        functools.partial(_ffn_kernel, seq=seq),
        grid=grid,
        in_specs=[pl.BlockSpec((None, tm, d), lambda bb, i, j: (bb, i, 0)),
                  pl.BlockSpec((None, 6, d), lambda bb, i, j: (bb, 0, 0)),
                  pl.BlockSpec((None, 6, d), lambda bb, i, j: (0, 0, 0)),
                  pl.BlockSpec((1, d), lambda bb, i, j: (0, 0)),
                  pl.BlockSpec((d, th), lambda bb, i, j: (0, j)),
                  pl.BlockSpec((d, th), lambda bb, i, j: (0, j)),
                  pl.BlockSpec((th, d), lambda bb, i, j: (j, 0))],
        out_specs=pl.BlockSpec((None, tm, d), lambda bb, i, j: (bb, i, 0)),
        out_shape=jax.ShapeDtypeStruct((b, nt, d), F32),
        scratch_shapes=[pltpu.VMEM((tm, d), BF16), pltpu.VMEM((tm, d), F32)],
        compiler_params=_cparams("parallel", "parallel", "arbitrary"),
        name="dense_ffn",
    )(hh, modl, modc, g, wg, wu, wd)


def _sel01(x_f32, e_bf16):
    x1, x2, x3 = _split3(x_f32)
    return _dot(x1, e_bf16) + _dot(x2, e_bf16) + _dot(x3, e_bf16)


def _ssd_chunk(xbc, dt, nega, dskip, expand, lane0, state_ref, o_ref, *, reverse, add_skip):
    q = xbc.shape[0]
    g_n = SSM_GROUPS * SSM_STATE
    inner = xbc.shape[1] - 2 * g_n
    heads = inner // SSM_HEAD_DIM
    rp = inner // SSM_GROUPS
    r_heads = heads // SSM_GROUPS

    rows = lax.broadcasted_iota(jnp.int32, (q, q), 0)
    cols = lax.broadcasted_iota(jnp.int32, (q, q), 1)
    keep = (rows <= cols) if reverse else (rows >= cols)
    tri = keep.astype(BF16)

    la = dt * nega
    ac = _dot_exact_lhs(tri, la)
    ac_t = jnp.transpose(ac)
    last = 0 if reverse else q - 1
    ac_end = ac[last:last + 1, :]

    x = xbc[:, :inner].astype(F32)
    decay_in = jnp.exp(ac)
    xdt = x * _dot(dt.astype(BF16), expand)
    to_end = _dot(jnp.exp(ac_end - ac).astype(BF16), expand)
    from_start = _dot(decay_in.astype(BF16), expand)
    total = jnp.broadcast_to(decay_in[last:last + 1, :], (2 * SUBLANES, LANE))
    chunk_decay = _sel01(total, expand)[0:1, :]
    xdt_b = xdt.astype(BF16)
    xend_b = (xdt * to_end).astype(BF16)

    for g in range(SSM_GROUPS):
        bg = xbc[:, inner + g * SSM_STATE: inner + (g + 1) * SSM_STATE]
        cg = xbc[:, inner + g_n + g * SSM_STATE: inner + g_n + (g + 1) * SSM_STATE]
        cb = _dot_nt(cg, bg).astype(BF16)
        pieces = []
        for r in range(r_heads):
            hd = g * r_heads + r
            ln = lane0 + hd
            diff = ac[:, ln:ln + 1] - ac_t[ln:ln + 1, :]
            dec = jnp.exp(jnp.minimum(diff, 0.0).astype(BF16))
            m = jnp.where(keep, cb * dec, jnp.zeros_like(dec))
            pieces.append(_dot(m, xdt_b[:, hd * SSM_HEAD_DIM:(hd + 1) * SSM_HEAD_DIM]))
        y_diag = jnp.concatenate(pieces, axis=1)
        sl = slice(g * rp, (g + 1) * rp)
        h_in = state_ref[g]
        y_off = _dot(cg, h_in.astype(BF16)) * from_start[:, sl]
        states = _dot_tn(bg, xend_b[:, sl])
        state_ref[g] = h_in * chunk_decay[:, sl] + states
        y = y_diag + y_off
        if add_skip:
            y = y + x[:, sl] * dskip[:, sl]
        o_ref[:, sl] = y.astype(o_ref.dtype)


def _ssd_kernel(xf_ref, xb_ref, dtf_ref, dtb_ref, nega_ref, dskip_ref, exp_ref, yf_ref, yb_ref, st_ref, *, heads):
    @pl.when(pl.program_id(1) == 0)
    def _():
        st_ref[...] = jnp.zeros_like(st_ref)

    _ssd_chunk(xf_ref[...], dtf_ref[...], nega_ref[...], dskip_ref[...], exp_ref[0], 0, st_ref.at[0], yf_ref,
               reverse=False, add_skip=True)
    _ssd_chunk(xb_ref[...], dtb_ref[...], nega_ref[...], dskip_ref[...], exp_ref[1], heads, st_ref.at[1], yb_ref,
               reverse=True, add_skip=False)


def _ssd(xbc, dt, nega, dskip, expand, *, seq, heads):
    b, nt, width = xbc.shape
    q = SSM_CHUNK
    inner = heads * SSM_HEAD_DIM
    nch = nt // q
    ncl = seq // q

    def fwd(bb, j):
        return (bb, (j + ncl) % nch, 0)

    def bwd(bb, j):
        return (bb, nch - 1 - j, 0)

    return pl.pallas_call(
        functools.partial(_ssd_kernel, heads=heads),
        grid=(b, nch),
        in_specs=[pl.BlockSpec((None, q, width), fwd),
                  pl.BlockSpec((None, q, width), bwd),
                  pl.BlockSpec((None, q, LANE), fwd),
                  pl.BlockSpec((None, q, LANE), bwd),
                  pl.BlockSpec((1, LANE), lambda bb, j: (0, 0)),
                  pl.BlockSpec((1, inner), lambda bb, j: (0, 0)),
                  pl.BlockSpec((2, LANE, inner), lambda bb, j: (0, 0, 0))],
        out_specs=[pl.BlockSpec((None, q, inner), fwd),
                   pl.BlockSpec((None, q, inner), bwd)],
        out_shape=[jax.ShapeDtypeStruct((b, nt, inner), BF16)] * 2,
        scratch_shapes=[pltpu.VMEM((2, SSM_GROUPS, SSM_STATE, inner // SSM_GROUPS), F32)],
        compiler_params=_cparams("parallel", "arbitrary"),
        name="ssd_scan",
    )(xbc, xbc, dt, dt, nega, dskip, expand)


def _attn_heads(q, k, v, o_ref):
    hd = ATT_HEAD_DIM
    n_q = q.shape[1] // hd
    rep = n_q // ATT_KV_HEADS
    tq = q.shape[0]
    for g in range(ATT_KV_HEADS):
        kg = k[:, g * hd:(g + 1) * hd]
        vg = v[:, g * hd:(g + 1) * hd]
        v_aug = jnp.concatenate([vg, jnp.ones_like(vg)], axis=1)
        qg = jnp.concatenate([q[:, (g * rep + r) * hd:(g * rep + r + 1) * hd] for r in range(rep)], axis=0)
        s = _dot_nt(qg, kg).astype(BF16)
        m = jnp.max(s, axis=-1, keepdims=True)
        p = jnp.exp2(s - m)
        o_aug = _dot(p, v_aug)
        o = o_aug[:, :hd] / o_aug[:, hd:hd + 1]
        og = jnp.concatenate([o[r * tq:(r + 1) * tq, :] for r in range(rep)], axis=1)
        o_ref[:, g * rep * hd:(g + 1) * rep * hd] = og.astype(o_ref.dtype)


def _attn_kernel(q_ref, k_ref, v_ref, o_ref, *, seq):
    tq = q_ref.shape[0]
    is_lat = pl.program_id(1) * tq < seq

    @pl.when(is_lat)
    def _():
        _attn_heads(q_ref[...], k_ref[...], v_ref[...], o_ref)

    @pl.when(jnp.logical_not(is_lat))
    def _():
        _attn_heads(q_ref[...], k_ref[seq:, :], v_ref[seq:, :], o_ref)


def _attention(q, k, v, *, seq, tq):
    b, nt, dq = q.shape
    dk = k.shape[2]
    return pl.pallas_call(
        functools.partial(_attn_kernel, seq=seq),
        grid=(b, nt // tq),
        in_specs=[pl.BlockSpec((None, tq, dq), lambda bb, i: (bb, i, 0)),
                  pl.BlockSpec((None, nt, dk), lambda bb, i: (bb, 0, 0)),
                  pl.BlockSpec((None, nt, dk), lambda bb, i: (bb, 0, 0))],
        out_specs=pl.BlockSpec((None, tq, dq), lambda bb, i: (bb, i, 0)),
        out_shape=jax.ShapeDtypeStruct((b, nt, dq), BF16),
        compiler_params=_cparams("parallel", "parallel"),
        name="gqa_attention",
    )(q, k, v)


def _pool_kernel(v_ref, vp_ref, vn_ref, h_ref, modl_ref, modc_ref, wg_ref, sc_ref, o_ref, *, seq, nt):
    tm = v_ref.shape[0]
    halo = vp_ref.shape[0]
    i = pl.program_id(1)
    row0 = i * tm
    v = v_ref[...]
    vext = jnp.concatenate([vp_ref[...], v, vn_ref[...]], axis=0)
    vhi, vlo = _split2(vext)
    t = row0 + lax.broadcasted_iota(jnp.int32, (tm, 1), 0)
    s = row0 - halo + lax.broadcasted_iota(jnp.int32, (1, tm + 2 * halo), 1)
    in_ctx = t >= seq
    seg_lo = jnp.where(in_ctx, seq, 0)
    seg_hi = jnp.where(in_ctx, nt, seq)
    grp = v.shape[1] // len(POOL_WINDOWS)
    outs = []
    for gi, win in enumerate(POOL_WINDOWS):
        lo = jnp.maximum(t - win // 2, seg_lo)
        hi = jnp.minimum(t + win // 2, seg_hi)
        band = ((s >= lo) & (s < hi)).astype(BF16)
        sl = slice(gi * grp, (gi + 1) * grp)
        summed = _dot(band, vhi[:, sl]) + _dot(band, vlo[:, sl])
        pooled = summed / (hi - lo).astype(F32) - v[:, sl]
        outs.append(_dot(pooled.astype(BF16), wg_ref[gi]))
    y = jnp.concatenate(outs, axis=1) * sc_ref[...]
    gate = _gate_rows(modl_ref, modc_ref, 2, tm, row0, seq)
    o_ref[...] = h_ref[...] + gate * y


def _pool(v, hh, modl, modc, w_grp, scale, *, seq, tm):
    b, nt, d = hh.shape
    halo = POOL_HALO
    per = tm // halo
    nhb = nt // halo
    grp = w_grp.shape[1]
    return pl.pallas_call(
        functools.partial(_pool_kernel, seq=seq, nt=nt),
        grid=(b, nt // tm),
        in_specs=[pl.BlockSpec((None, tm, d), lambda bb, i: (bb, i, 0)),
                  pl.BlockSpec((None, halo, d), lambda bb, i: (bb, jnp.maximum(i * per - 1, 0), 0)),
                  pl.BlockSpec((None, halo, d), lambda bb, i: (bb, jnp.minimum((i + 1) * per, nhb - 1), 0)),
                  pl.BlockSpec((None, tm, d), lambda bb, i: (bb, i, 0)),
                  pl.BlockSpec((None, 6, d), lambda bb, i: (bb, 0, 0)),
                  pl.BlockSpec((None, 6, d), lambda bb, i: (0, 0, 0)),
                  pl.BlockSpec((len(POOL_WINDOWS), grp, grp), lambda bb, i: (0, 0, 0)),
                  pl.BlockSpec((1, d), lambda bb, i: (0, 0))],
        out_specs=pl.BlockSpec((None, tm, d), lambda bb, i: (bb, i, 0)),
        out_shape=jax.ShapeDtypeStruct((b, nt, d), F32),
        compiler_params=_cparams("parallel", "parallel"),
        name="pool_mixer",
    )(v, v, v, hh, modl, modc, w_grp, scale)


def _lru_scan_chunk(a_scr, bx_scr, h_scr, carry_scr, *, reverse):
    tc = a_scr.shape[0]

    def step(tt, h):
        t = tc - 1 - tt if reverse else tt
        h = a_scr[pl.ds(t, 1), :] * h + bx_scr[pl.ds(t, 1), :]
        h_scr[pl.ds(t, 1), :] = h
        return h

    carry_scr[...] = lax.fori_loop(0, tc, step, carry_scr[...], unroll=8)


def _lru_kernel(x_ref, w_ref, b_ref, lam_ref, o_ref, a_scr, bx_scr, h_scr, carry_scr):
    d_dir = pl.program_id(1)
    tc, width = x_ref.shape
    bw = width // LRU_BLOCKS

    @pl.when(pl.program_id(2) == 0)
    def _():
        carry_scr[...] = jnp.zeros_like(carry_scr)

    xb = x_ref[...]
    for k in range(LRU_BLOCKS):
        sl = slice(k * bw, (k + 1) * bw)
        pre = _dot(xb[:, sl], w_ref[k])
        r_gate = _sigmoid(pre[:, :bw] + b_ref[0:1, sl])
        i_gate = _sigmoid(pre[:, bw:] + b_ref[1:2, sl])
        neg_lam = -lam_ref[:, sl]
        softplus = jnp.maximum(neg_lam, 0.0) + jnp.log1p(jnp.exp(-jnp.abs(neg_lam)))
        log_a = (-LRU_C) * r_gate * softplus
        a = jnp.exp(log_a)
        a_scr[:, sl] = a
        bx_scr[:, sl] = jnp.sqrt(-jnp.tanh(log_a) * (a * a + 1.0)) * (i_gate * xb[:, sl].astype(F32))

    @pl.when(d_dir == 0)
    def _():
        _lru_scan_chunk(a_scr, bx_scr, h_scr, carry_scr, reverse=False)

    @pl.when(d_dir == 1)
    def _():
        _lru_scan_chunk(a_scr, bx_scr, h_scr, carry_scr, reverse=True)

    o_ref[...] = h_scr[...].astype(o_ref.dtype)


def _lru(xr, w_cat, bias, lam, *, seq, tc):
    b, nt, width = xr.shape
    nc = nt // tc
    nl = seq // tc

    def chunk(d, j):
        return d * (nc - 1 - j) + (1 - d) * ((j + nl) % nc)

    return pl.pallas_call(
        _lru_kernel,
        grid=(b, 2, nc),
        in_specs=[pl.BlockSpec((None, tc, width), lambda bb, d, j: (bb, chunk(d, j), 0)),
                  pl.BlockSpec((None,) + w_cat.shape[1:], lambda bb, d, j: (d, 0, 0, 0)),
                  pl.BlockSpec((None, 2, width), lambda bb, d, j: (d, 0, 0)),
                  pl.BlockSpec((None, 1, width), lambda bb, d, j: (d, 0, 0))],
        out_specs=pl.BlockSpec((None, None, tc, width), lambda bb, d, j: (d, bb, chunk(d, j), 0)),
        out_shape=jax.ShapeDtypeStruct((2, b, nt, width), BF16),
        scratch_shapes=[pltpu.VMEM((tc, width), F32)] * 3 + [pltpu.VMEM((1, width), F32)],
        compiler_params=_cparams("parallel", "arbitrary", "arbitrary"),
        name="rglru_scan",
    )(xr, w_cat, bias, lam)


def _router_kernel(h_ref, modl_ref, modc_ref, g_ref, whi_ref, wlo_ref, u_ref, meta_ref, cnt_ref, carry_scr, *,
                   seq, n_exp):
    tm = h_ref.shape[0]
    row0 = pl.program_id(1) * tm
    first = (pl.program_id(0) == 0) & (pl.program_id(1) == 0)

    @pl.when(first)
    def _():
        carry_scr[...] = jnp.zeros_like(carry_scr)

    u = _modulated(h_ref[...], modl_ref, modc_ref, g_ref, 3, 4, row0, seq)
    u_ref[...] = u
    uhi, ulo = _split2(u)
    logits = _dot(uhi, whi_ref[...]) + _dot(ulo, whi_ref[...]) + _dot(uhi, wlo_ref[...])
    lane = lax.broadcasted_iota(jnp.int32, (tm, LANE), 1)
    neg = jnp.float32(-jnp.inf)
    logits = jnp.where(lane < n_exp, logits, neg)
    v0 = jnp.max(logits, axis=-1, keepdims=True)
    lane_f = lane.astype(F32)
    i0 = jnp.min(jnp.where(logits == v0, lane_f, float(LANE)), axis=-1, keepdims=True)
    rest = jnp.where(lane_f == i0, neg, logits)
    v1 = jnp.max(rest, axis=-1, keepdims=True)
    i1 = jnp.min(jnp.where(rest == v1, lane_f, float(LANE)), axis=-1, keepdims=True)
    g0 = 1.0 / (1.0 + jnp.exp(v1 - v0))
    g1 = 1.0 - g0
    sel0 = lane_f == i0
    sel1 = lane_f == i1
    onehot = (sel0 | sel1).astype(BF16)
    rr = lax.broadcasted_iota(jnp.int32, (tm, tm), 0)
    cc = lax.broadcasted_iota(jnp.int32, (tm, tm), 1)
    before = (cc < rr).astype(BF16)
    prefix = _dot(before, onehot) + carry_scr[...]
    r0 = jnp.sum(jnp.where(sel0, prefix, 0.0), axis=-1, keepdims=True)
    r1 = jnp.sum(jnp.where(sel1, prefix, 0.0), axis=-1, keepdims=True)
    carry_scr[...] += jnp.sum(onehot.astype(F32), axis=0, keepdims=True)
    cnt_ref[...] = carry_scr[...]
    meta = jnp.where(lane == 0, i0, 0.0)
    meta = jnp.where(lane == 1, i1, meta)
    meta = jnp.where(lane == 2, g0, meta)
    meta = jnp.where(lane == 3, g1, meta)
    meta = jnp.where(lane == 4, r0, meta)
    meta = jnp.where(lane == 5, r1, meta)
    meta_ref[...] = meta


def _router(hh, modl, modc, g, whi, wlo, *, rows, tm, seq, n_exp):
    b, nt, d = hh.shape
    return pl.pallas_call(
        functools.partial(_router_kernel, seq=seq, n_exp=n_exp),
        grid=(b, rows // tm),
        in_specs=[pl.BlockSpec((None, tm, d), lambda bb, i: (bb, i, 0)),
                  pl.BlockSpec((None, 6, d), lambda bb, i: (bb, 0, 0)),
                  pl.BlockSpec((None, 6, d), lambda bb, i: (0, 0, 0)),
                  pl.BlockSpec((1, d), lambda bb, i: (0, 0)),
                  pl.BlockSpec((d, LANE), lambda bb, i: (0, 0)),
                  pl.BlockSpec((d, LANE), lambda bb, i: (0, 0))],
        out_specs=[pl.BlockSpec((None, tm, d), lambda bb, i: (bb, i, 0)),
                   pl.BlockSpec((None, tm, LANE), lambda bb, i: (bb, i, 0)),
                   pl.BlockSpec((1, LANE), lambda bb, i: (0, 0))],
        out_shape=[jax.ShapeDtypeStruct((b, rows, d), F32),
                   jax.ShapeDtypeStruct((b, rows, LANE), F32),
                   jax.ShapeDtypeStruct((1, LANE), F32)],
        scratch_shapes=[pltpu.VMEM((1, LANE), F32)],
        compiler_params=_cparams("arbitrary", "arbitrary"),
        name="moe_router",
    )(hh, modl, modc, g, whi, wlo)


def _dispatch_kernel(dest_ref, fill_ref, u_ref, xg_hbm, zero_scr, sem, *, tm, rows_per_b):
    base = (pl.program_id(0) * rows_per_b + pl.program_id(1) * tm) * TOP_K

    @pl.when((pl.program_id(0) == 0) & (pl.program_id(1) == 0))
    def _():
        zero_scr[...] = jnp.zeros_like(zero_scr)

        def fill(blk, c):
            @pl.when(fill_ref[blk] != 0)
            def _():
                row0 = pl.multiple_of(blk * MOE_BLOCK, MOE_BLOCK)
                cp = pltpu.make_async_copy(zero_scr, xg_hbm.at[pl.ds(row0, MOE_BLOCK)], sem)
                cp.start()
                cp.wait()
            return c

        lax.fori_loop(0, fill_ref.shape[0], fill, 0)

    def issue(r, c):
        for k in range(TOP_K):
            dst = xg_hbm.at[pl.ds(dest_ref[base + r * TOP_K + k], 1)]
            pltpu.make_async_copy(u_ref.at[pl.ds(r, 1)], dst, sem).start()
        return c

    lax.fori_loop(0, tm, issue, 0, unroll=8)
    for k in range(TOP_K):
        pltpu.make_async_copy(u_ref, xg_hbm.at[pl.ds(0, tm)], sem).wait()


def _dispatch(dest, fill, u, *, tm):
    b, rows, d = u.shape
    grid_spec = pltpu.PrefetchScalarGridSpec(
        num_scalar_prefetch=2, grid=(b, rows // tm),
        in_specs=[pl.BlockSpec((tm, d), lambda bb, i, ds, fl: (bb * (rows // tm) + i, 0))],
        out_specs=pl.BlockSpec(memory_space=pl.ANY),
        scratch_shapes=[pltpu.VMEM((MOE_BLOCK, d), F32), pltpu.SemaphoreType.DMA(())])
    return pl.pallas_call(
        functools.partial(_dispatch_kernel, tm=tm, rows_per_b=rows),
        grid_spec=grid_spec,
        out_shape=jax.ShapeDtypeStruct((fill.shape[0] * MOE_BLOCK, d), F32),
        compiler_params=_cparams("arbitrary", "arbitrary"),
        name="moe_dispatch",
    )(dest, fill, u.reshape(b * rows, d))


def _expert_kernel(be_ref, nu_ref, x_ref, wg_ref, wu_ref, wd_ref, y_ref, xb_scr, acc_scr):
    blk = pl.program_id(0)
    j = pl.program_id(1)
    used = blk < nu_ref[0]

    @pl.when(used & (j == 0))
    def _():
        xb_scr[...] = x_ref[...].astype(BF16)
        acc_scr[...] = jnp.zeros_like(acc_scr)

    @pl.when(used)
    def _():
        acc_scr[...] += _swiglu_tile(xb_scr[...], wg_ref, wu_ref, wd_ref)

    @pl.when(used & (j == pl.num_programs(1) - 1))
    def _():
        y_ref[...] = acc_scr[...]

    @pl.when(jnp.logical_not(used) & (j == pl.num_programs(1) - 1))
    def _():
        y_ref[...] = jnp.zeros_like(y_ref)


def _experts(block_expert, n_used, xg, wg, wu, wd, *, th, layer):
    p, d = xg.shape
    hid = wg.shape[3]
    nb = p // MOE_BLOCK
    nj = hid // th

    def blk_eff(blk, nu):
        return jnp.minimum(blk, nu[0] - 1)

    def j_eff(blk, j, nu):
        return jnp.where(blk < nu[0], j, nj - 1)

    def w_in_map(blk, j, be, nu):
        return (layer, be[blk_eff(blk, nu)], 0, j_eff(blk, j, nu))

    def w_out_map(blk, j, be, nu):
        return (layer, be[blk_eff(blk, nu)], j_eff(blk, j, nu), 0)

    grid_spec = pltpu.PrefetchScalarGridSpec(
        num_scalar_prefetch=2, grid=(nb, nj),
        in_specs=[pl.BlockSpec((MOE_BLOCK, d), lambda blk, j, be, nu: (blk_eff(blk, nu), 0)),
                  pl.BlockSpec((None, None, d, th), w_in_map),
                  pl.BlockSpec((None, None, d, th), w_in_map),
                  pl.BlockSpec((None, None, th, d), w_out_map)],
        out_specs=pl.BlockSpec((MOE_BLOCK, d), lambda blk, j, be, nu: (blk, 0)),
        scratch_shapes=[pltpu.VMEM((MOE_BLOCK, d), BF16), pltpu.VMEM((MOE_BLOCK, d), F32)])
    return pl.pallas_call(
        _expert_kernel, grid_spec=grid_spec,
        out_shape=jax.ShapeDtypeStruct((p, d), F32),
        compiler_params=_cparams("arbitrary", "arbitrary"),
        name="moe_experts",
    )(block_expert, n_used, xg, wg, wu, wd)


def _combine_kernel(dest_ref, yg_hbm, meta_ref, h_ref, modl_ref, modc_ref, o_ref, y0_scr, y1_scr, sem, *, tm,
                    rows_per_b, seq):
    bb = pl.program_id(0)
    i = pl.program_id(1)
    base = (bb * rows_per_b + i * tm) * TOP_K
    bufs = (y0_scr, y1_scr)

    def issue(r, c):
        for k in range(TOP_K):
            src = yg_hbm.at[pl.ds(dest_ref[base + r * TOP_K + k], 1)]
            pltpu.make_async_copy(src, bufs[k].at[pl.ds(r, 1)], sem).start()
        return c

    lax.fori_loop(0, tm, issue, 0, unroll=8)
    for k in range(TOP_K):
        pltpu.make_async_copy(yg_hbm.at[pl.ds(0, tm)], bufs[k], sem).wait()
    meta = meta_ref[...]
    y = meta[:, 2:3] * y0_scr[...] + meta[:, 3:4] * y1_scr[...]
    gate = _gate_rows(modl_ref, modc_ref, 5, tm, i * tm, seq)
    o_ref[...] = h_ref[...] + gate * y


def _combine(dest, yg, meta, hh, modl, modc, *, tm, seq):
    b, rows, _ = meta.shape
    d = hh.shape[2]
    grid_spec = pltpu.PrefetchScalarGridSpec(
        num_scalar_prefetch=1, grid=(b, rows // tm),
        in_specs=[pl.BlockSpec(memory_space=pl.ANY),
                  pl.BlockSpec((None, tm, LANE), lambda bb, i, ds: (bb, i, 0)),
                  pl.BlockSpec((None, tm, d), lambda bb, i, ds: (bb, i, 0)),
                  pl.BlockSpec((None, 6, d), lambda bb, i, ds: (bb, 0, 0)),
                  pl.BlockSpec((None, 6, d), lambda bb, i, ds: (0, 0, 0))],
        out_specs=pl.BlockSpec((None, tm, d), lambda bb, i, ds: (bb, i, 0)),
        scratch_shapes=[pltpu.VMEM((tm, d), F32), pltpu.VMEM((tm, d), F32), pltpu.SemaphoreType.DMA(())])
    return pl.pallas_call(
        functools.partial(_combine_kernel, tm=tm, rows_per_b=rows, seq=seq),
        grid_spec=grid_spec,
        out_shape=jax.ShapeDtypeStruct((b, rows, d), F32),
        compiler_params=_cparams("arbitrary", "arbitrary"),
        name="moe_combine",
    )(dest, yg, meta, hh, modl, modc)


def _moe(hh, modl, modc, g, w_router, wg, wu, wd, *, rows, seq, layer):
    b, nt, d = hh.shape
    n_exp = w_router.shape[1]
    wr = jnp.pad(w_router, ((0, 0), (0, LANE - n_exp)))
    whi, wlo = _split2(wr)
    tm = _pick(rows, (768, 512, 256))
    u, meta, counts = _router(hh, modl, modc, g, whi, wlo, rows=rows, tm=tm, seq=seq, n_exp=n_exp)

    cnt = counts[0, :n_exp].astype(jnp.int32)
    padded = (cnt + MOE_BLOCK - 1) // MOE_BLOCK * MOE_BLOCK
    pend = jnp.cumsum(padded)
    pstart = pend - padded
    n_assign = b * rows * TOP_K
    nb = -(-(n_assign + n_exp * (MOE_BLOCK - 1)) // MOE_BLOCK)
    idx = meta[:, :, 0:TOP_K].astype(jnp.int32)
    rank = meta[:, :, 4:4 + TOP_K].astype(jnp.int32)
    dest = (sum(jnp.where(idx == e, pstart[e], 0) for e in range(n_exp)) + rank).reshape(-1)
    blk_start = jnp.arange(nb, dtype=jnp.int32) * MOE_BLOCK
    block_expert = jnp.minimum(jnp.sum(pend[None, :] <= blk_start[:, None], axis=1), n_exp - 1).astype(jnp.int32)
    n_used = (pend[-1:] // MOE_BLOCK).astype(jnp.int32)

    blk_id = jnp.arange(nb, dtype=jnp.int32)
    holds_padding = jnp.any((pend[None, :] // MOE_BLOCK - 1 == blk_id[:, None]) & (padded[None, :] > 0), axis=1)
    fill = (holds_padding | (blk_id >= n_used[0])).astype(jnp.int32)
    td = tm
    xg = _dispatch(dest, fill, u, tm=td)
    yg = _experts(block_expert, n_used, xg, wg, wu, wd, th=_pick(wg.shape[3], (512, 256, 128)), layer=layer)
    return _combine(dest, yg, meta, hh, modl, modc, tm=td, seq=seq)


def _rope_tables(seq, nt):
    rows = seq // GRID_W
    row = jnp.repeat(jnp.arange(rows), GRID_W).astype(F32)
    col = jnp.tile(jnp.arange(GRID_W), rows).astype(F32)
    n_freq = ATT_HEAD_DIM // 4
    inv_freq = ROPE_THETA ** (-jnp.arange(n_freq, dtype=F32) / n_freq)
    ang = jnp.concatenate([row[:, None] * inv_freq, col[:, None] * inv_freq], axis=-1)
    cos, sin = jnp.cos(ang), jnp.sin(ang)
    cos_h = jnp.concatenate([cos, cos], axis=-1)
    sin_h = jnp.concatenate([-sin, sin], axis=-1)
    reps = LANE // ATT_HEAD_DIM
    cos_t = jnp.tile(cos_h, (1, reps))
    sin_t = jnp.tile(sin_h, (1, reps))
    pad = nt - seq
    cos_t = jnp.concatenate([cos_t, jnp.ones((pad, LANE), F32)], axis=0)
    sin_t = jnp.concatenate([sin_t, jnp.zeros((pad, LANE), F32)], axis=0)
    return cos_t, sin_t


def _head_mean_matrix(n):
    r = jnp.arange(n) // ATT_HEAD_DIM
    return (r[:, None] == r[None, :]).astype(F32).astype(BF16) * jnp.asarray(1.0 / ATT_HEAD_DIM, BF16)


def kernel(x, c, ctx, c_ctx, w_mod, b_mod, norm_g, ssm_w_in, ssm_conv_w, ssm_conv_b, ssm_dt_bias, ssm_a_log, ssm_d, ssm_norm_g, ssm_w_out, att_w_qkv, att_q_norm, att_k_norm, att_w_out, pool_w_in, pool_w_grp, pool_scale, lru_w_in, lru_conv_w, lru_conv_b, lru_gate_w, lru_gate_b, lru_lambda, lru_w_out, ffn_w_gate, ffn_w_up, ffn_w_down, moe_w_router, moe_w_gate, moe_w_up, moe_w_down):
    b, seq, d = x.shape
    nctx = ctx.shape[1]
    nt = seq + nctx
    depth = w_mod.shape[0]
    n_mixers = 4

    bp = -(-(b + 1) // 8) * 8
    s_rows = jnp.concatenate([c, c_ctx[None, :], jnp.zeros((bp - b - 1, d), F32)], axis=0)
    mods = _mod_vectors(s_rows, w_mod, b_mod)
    modl_all = mods[:, :b].reshape(depth, b, 6, d)
    modc_all = mods[:, b:b + 1].reshape(depth, 1, 6, d)

    hh = jnp.concatenate([x, ctx], axis=1)
    tm_row = _pick(nt, (768, 512, 256))
    cos_t, sin_t = _rope_tables(seq, nt)
    out = None

    for i in range(depth):
        last = i == depth - 1
        kind, j = i % n_mixers, i // n_mixers
        modl, modc = modl_all[i], modc_all[i]
        g1 = norm_g[i, 0][None, :]
        g2 = norm_g[i, 1][None, :]
        proj = functools.partial(_proj, hh, modl, modc, g1, seq=seq, k_shift=0, k_scale=1)

        if kind == 0:
            inner = ssm_d.shape[1] * SSM_HEAD_DIM
            heads = ssm_d.shape[1]
            conv_dim = ssm_conv_w.shape[2]
            w_in = ssm_w_in[j].astype(BF16)
            w_z = w_in[:, :inner]
            w_xbc = w_in[:, inner:inner + conv_dim]
            w_dt = jnp.pad(w_in[:, inner + conv_dim:], ((0, 0), (0, LANE - 2 * heads)))
            dt_bias = jnp.pad(ssm_dt_bias[j].reshape(1, 2 * heads), ((0, 0), (0, LANE - 2 * heads)))
            z = proj(w_z, epi=_epi_plain, tm=tm_row, tn=inner, out_dtype=BF16, name="ssd_proj_z")
            xbc = proj(w_xbc, epi=_epi_conv_silu, extra=(ssm_conv_w[j], ssm_conv_b[j][None, :]),
                       extra_specs=(_col_spec(ssm_conv_w.shape[1], 512), _col_spec(1, 512)),
                       tm=nt, tn=512, out_dtype=BF16, name="ssd_proj_xbc")
            dt = proj(w_dt, epi=_epi_softplus, extra=(dt_bias,), extra_specs=(_col_spec(1, LANE),),
                      tm=tm_row, tn=LANE, out_dtype=F32, name="ssd_proj_dt")
            nega = jnp.pad(-jnp.exp(ssm_a_log[j].astype(F32)).reshape(1, 2 * heads), ((0, 0), (0, LANE - 2 * heads)))
            head_of = jnp.arange(inner)[None, :] // SSM_HEAD_DIM
            lanes = jnp.arange(LANE)[:, None]
            expand = jnp.stack([(lanes == head_of + dd * heads) for dd in range(2)]).astype(F32).astype(BF16)
            dskip = jnp.repeat(ssm_d[j].astype(F32), SSM_HEAD_DIM)[None, :]
            yf, yb = _ssd(xbc, dt, nega, dskip, expand, seq=seq, heads=heads)
            hh = _mmres((yf, yb, z), (inner, inner, inner), ssm_w_out[j].astype(BF16), hh, modl, modc,
                        pro=_pro_ssd, tm=tm_row, k_gate=2, seq=seq, name="ssd_out",
                        row_ins=(ssm_norm_g[j][None, :],))
        elif kind == 1:
            nq = att_w_out.shape[1]
            nk = ATT_KV_HEADS * ATT_HEAD_DIM
            w_qkv = att_w_qkv[j].astype(BF16)
            tab_specs = (pl.BlockSpec((tm_row, LANE), lambda bb, ii, jj: (ii, 0)),) * 2

            def qk_extra(n, gain):
                return ((_head_mean_matrix(n), jnp.tile(gain, n // ATT_HEAD_DIM)[None, :], cos_t, sin_t),
                        (pl.BlockSpec((n, n), lambda bb, ii, jj: (0, 0)),
                         pl.BlockSpec((1, n), lambda bb, ii, jj: (0, 0))) + tab_specs)

            ex, sp = qk_extra(nq, att_q_norm[j])
            q = proj(w_qkv[:, :nq], epi=functools.partial(_epi_qk, out_scale=ATT_HEAD_DIM ** -0.5 * math.log2(math.e)),
                     extra=ex,
                     extra_specs=sp, tm=tm_row, tn=nq, out_dtype=BF16, name="att_proj_q")
            ex, sp = qk_extra(nk, att_k_norm[j])
            k = proj(w_qkv[:, nq:nq + nk], epi=functools.partial(_epi_qk, out_scale=1.0), extra=ex,
                     extra_specs=sp, tm=tm_row, tn=nk, out_dtype=BF16, name="att_proj_k")
            v = proj(w_qkv[:, nq + nk:], epi=_epi_plain, tm=tm_row, tn=nk, out_dtype=BF16, name="att_proj_v")
            o = _attention(q, k, v, seq=seq, tq=_pick(nctx, (128,)))
            hh = _mmres((o,), (nq,), att_w_out[j].astype(BF16), hh, modl, modc, pro=_pro_plain, tm=tm_row,
                        k_gate=2, seq=seq, name="att_out")
        elif kind == 2:
            v = proj(pool_w_in[j].astype(BF16), epi=_epi_plain, tm=tm_row, tn=d, out_dtype=F32,
                     name="pool_proj")
            hh = _pool(v, hh, modl, modc, pool_w_grp[j].astype(BF16), pool_scale[j][None, :], seq=seq,
                       tm=_pick(nctx, (256, 128)))
        else:
            width = lru_w_out.shape[1]
            w_in = lru_w_in[j].astype(BF16)
            gg = proj(w_in[:, :width], epi=_epi_gelu, tm=tm_row, tn=width, out_dtype=BF16, name="lru_proj_gate")
            tn_x = _pick(width, (640, 256, 128))
            xr = proj(w_in[:, width:], epi=_epi_conv_bias, extra=(lru_conv_w[j], lru_conv_b[j][None, :]),
                      extra_specs=(_col_spec(lru_conv_w.shape[1], tn_x), _col_spec(1, tn_x)),
                      tm=nt, tn=tn_x, out_dtype=BF16, name="lru_proj_x")
            gw = lru_gate_w[j]
            w_cat = jnp.concatenate([gw[:, 0], gw[:, 1]], axis=-1).astype(BF16)
            hs = _lru(xr, w_cat, lru_gate_b[j], lru_lambda[j][:, None, :], seq=seq, tc=_pick(nctx, (256, 128)))
            rows = seq if last else nt
            hh = _mmres(((hs, 0), (hs, 1), gg), (width, width, width), lru_w_out[j].astype(BF16), hh, modl, modc,
                        pro=_pro_lru, tm=_pick(rows, (1024, 768, 512, 256)), k_gate=2, seq=seq, name="lru_out",
                        rows=rows)

        kf = i // 2
        if i % 2 == 0:
            hh = _dense_ffn(hh, modl, modc, g2, ffn_w_gate[kf].astype(BF16), ffn_w_up[kf].astype(BF16),
                            ffn_w_down[kf].astype(BF16), tm=tm_row,
                            th=_pick(ffn_w_gate.shape[2], (512, 256, 128)), seq=seq)
        else:
            rows = seq if last else nt
            hh = _moe(hh, modl, modc, g2, moe_w_router[kf], moe_w_gate, moe_w_up, moe_w_down, rows=rows, seq=seq,
                      layer=kf)
        if last:
            out = hh[:, :seq] if hh.shape[1] != seq else hh
    return out
```

```python
import functools
import math

import jax
import jax.numpy as jnp
from jax import lax
from jax.experimental import pallas as pl
from jax.experimental.pallas import tpu as pltpu

F32 = jnp.float32
BF16 = jnp.bfloat16
EPS = 1e-6

GRID_W = 64
SSM_HEAD_DIM = 64
SSM_GROUPS = 4
SSM_STATE = 128
SSM_CHUNK = 128
ATT_HEAD_DIM = 64
ATT_KV_HEADS = 4
ROPE_THETA = 10000.0
POOL_WINDOWS = (2, 4, 8, 16)
LRU_BLOCKS = 10
LRU_C = 8.0
TOP_K = 2
MOE_BLOCK = 1024

LANE = 128
SUBLANES = 8
POOL_HALO = 64
VMEM_LIMIT = 56 * 1024 * 1024


def _cparams(*sem):
    return pltpu.CompilerParams(dimension_semantics=sem, vmem_limit_bytes=VMEM_LIMIT)


def _pick(n, candidates):
    for c in candidates:
        if n % c == 0:
            return c
    raise ValueError(f"no tile in {candidates} divides {n}")


def _sigmoid(x):
    return 0.5 * jnp.tanh(0.5 * x) + 0.5


def _silu(x):
    return x * _sigmoid(x)


def _split2(x):
    hi = x.astype(BF16)
    lo = (x - hi.astype(F32)).astype(BF16)
    return hi, lo


def _split3(x):
    x1 = x.astype(BF16)
    r = x - x1.astype(F32)
    x2 = r.astype(BF16)
    x3 = (r - x2.astype(F32)).astype(BF16)
    return x1, x2, x3


def _dot(a, b):
    return jnp.dot(a, b, preferred_element_type=F32)


def _dot_nt(a, b):
    return lax.dot_general(a, b, (((1,), (1,)), ((), ())), preferred_element_type=F32)


def _dot_tn(a, b):
    return lax.dot_general(a, b, (((0,), (0,)), ((), ())), preferred_element_type=F32)


def _dot_exact_lhs(a_bf16, x_f32):
    x1, x2, x3 = _split3(x_f32)
    return _dot(a_bf16, x1) + _dot(a_bf16, x2) + _dot(a_bf16, x3)


def _modulated(h, modl_ref, modc_ref, g_ref, k_shift, k_scale, row0, seq):
    tm = h.shape[0]
    row = row0 + lax.broadcasted_iota(jnp.int32, (tm, 1), 0)
    is_ctx = row >= seq
    shift = jnp.where(is_ctx, modc_ref[k_shift:k_shift + 1, :], modl_ref[k_shift:k_shift + 1, :])
    scale = jnp.where(is_ctx, modc_ref[k_scale:k_scale + 1, :], modl_ref[k_scale:k_scale + 1, :])
    ms = jnp.mean(h * h, axis=-1, keepdims=True)
    y = h * lax.rsqrt(ms + EPS) * g_ref[...]
    return y * (1.0 + scale) + shift


def _gate_rows(modl_ref, modc_ref, k_gate, tm, row0, seq):
    row = row0 + lax.broadcasted_iota(jnp.int32, (tm, 1), 0)
    return jnp.where(row >= seq, modc_ref[k_gate:k_gate + 1, :], modl_ref[k_gate:k_gate + 1, :])


def _mod_kernel(s_ref, w_ref, b_ref, o_ref):
    s = _silu(s_ref[...])
    o_ref[...] = _dot(s.astype(BF16), w_ref[...].astype(BF16)) + b_ref[...]


def _mod_vectors(s_rows, w_mod, b_mod):
    depth, d, n6 = w_mod.shape
    bp = s_rows.shape[0]
    tn = _pick(n6, (1536, 1024, 512, 256, 128))
    return pl.pallas_call(
        _mod_kernel,
        grid=(depth, n6 // tn),
        in_specs=[pl.BlockSpec((bp, d), lambda l, j: (0, 0)),
                  pl.BlockSpec((None, d, tn), lambda l, j: (l, 0, j)),
                  pl.BlockSpec((None, 1, tn), lambda l, j: (l, 0, j))],
        out_specs=pl.BlockSpec((None, bp, tn), lambda l, j: (l, 0, j)),
        out_shape=jax.ShapeDtypeStruct((depth, bp, n6), F32),
        compiler_params=_cparams("parallel", "parallel"),
        name="mod_vectors",
    )(s_rows, w_mod, b_mod.reshape(depth, 1, n6))


def _seg_conv(x, w_ref, seq):
    nt = x.shape[0]
    row = lax.broadcasted_iota(jnp.int32, (nt, 1), 0)
    in_ctx = row >= seq
    pos = jnp.where(in_ctx, row - seq, row)
    seglen = jnp.where(in_ctx, nt - seq, seq)
    out = x * w_ref[2:3, :]
    for k, off in ((0, -2), (1, -1), (3, 1)):
        shifted = pltpu.roll(x, (-off) % nt, axis=0)
        valid = (pos + off >= 0) & (pos + off < seglen)
        out = out + jnp.where(valid, shifted, 0.0) * w_ref[k:k + 1, :]
    return out


def _epi_plain(acc, extra, row0, seq):
    return acc


def _epi_softplus(acc, extra, row0, seq):
    (b_ref,) = extra
    x = acc + b_ref[...]
    return jnp.maximum(x, 0.0) + jnp.log1p(jnp.exp(-jnp.abs(x)))


def _epi_conv_silu(acc, extra, row0, seq):
    w_ref, b_ref = extra
    return _silu(_seg_conv(acc, w_ref, seq) + b_ref[...])


def _epi_conv_bias(acc, extra, row0, seq):
    w_ref, b_ref = extra
    return _seg_conv(acc, w_ref, seq) + b_ref[...]


def _epi_gelu(acc, extra, row0, seq):
    return jax.nn.gelu(acc)


def _epi_qk(acc, extra, row0, seq, *, out_scale):
    g_ref, gain_ref, cos_ref, sin_ref = extra
    tn = acc.shape[1]
    ms = _dot((acc * acc).astype(BF16), g_ref[...])
    xn = acc * lax.rsqrt(ms + EPS) * gain_ref[...]
    lane = lax.broadcasted_iota(jnp.int32, (1, tn), 1)
    half = ATT_HEAD_DIM // 2
    second = (lane % ATT_HEAD_DIM) >= half
    partner = jnp.where(second, pltpu.roll(xn, half, axis=1), pltpu.roll(xn, tn - half, axis=1))
    reps = tn // LANE
    cos = jnp.concatenate([cos_ref[...]] * reps, axis=1) if reps > 1 else cos_ref[...]
    sin = jnp.concatenate([sin_ref[...]] * reps, axis=1) if reps > 1 else sin_ref[...]
    return (xn * cos + partner * sin) * out_scale


def _proj_kernel(*refs, epi, n_extra, k_shift, k_scale, seq):
    h_ref, modl_ref, modc_ref, g_ref, w_ref = refs[:5]
    extra = refs[5:5 + n_extra]
    o_ref = refs[5 + n_extra]
    u_scr = refs[6 + n_extra]
    tm = h_ref.shape[0]
    row0 = pl.program_id(1) * tm

    @pl.when(pl.program_id(2) == 0)
    def _():
        u_scr[...] = _modulated(h_ref[...], modl_ref, modc_ref, g_ref, k_shift, k_scale, row0, seq).astype(BF16)

    acc = _dot(u_scr[...], w_ref[...])
    o_ref[...] = epi(acc, extra, row0, seq).astype(o_ref.dtype)


def _proj(hh, modl, modc, g, w, *, epi, extra=(), extra_specs=(), tm, tn, out_dtype, seq, k_shift, k_scale,
          name):
    b, nt, d = hh.shape
    n = w.shape[1]
    grid = (b, nt // tm, n // tn)
    in_specs = [pl.BlockSpec((None, tm, d), lambda bb, i, j: (bb, i, 0)),
                pl.BlockSpec((None, 6, d), lambda bb, i, j: (bb, 0, 0)),
                pl.BlockSpec((None, 6, d), lambda bb, i, j: (0, 0, 0)),
                pl.BlockSpec((1, d), lambda bb, i, j: (0, 0)),
                pl.BlockSpec((d, tn), lambda bb, i, j: (0, j))] + list(extra_specs)
    kern = functools.partial(_proj_kernel, epi=epi, n_extra=len(extra), k_shift=k_shift, k_scale=k_scale, seq=seq)
    return pl.pallas_call(
        kern, grid=grid, in_specs=in_specs,
        out_specs=pl.BlockSpec((None, tm, tn), lambda bb, i, j: (bb, i, j)),
        out_shape=jax.ShapeDtypeStruct((b, nt, n), out_dtype),
        scratch_shapes=[pltpu.VMEM((tm, d), BF16)],
        compiler_params=_cparams("parallel", "parallel", "arbitrary"),
        name=name,
    )(hh, modl, modc, g, w, *extra)


def _col_spec(rows, tn):
    return pl.BlockSpec((rows, tn), lambda bb, i, j: (0, j))


def _pro_plain(ins):
    (a_ref,) = ins
    return a_ref[...]


def _pro_ssd(ins):
    yf_ref, yb_ref, z_ref, ng_ref = ins
    y = (yf_ref[...].astype(F32) + yb_ref[...].astype(F32)) * _silu(z_ref[...].astype(F32))
    ms = jnp.mean(y * y, axis=-1, keepdims=True)
    return (y * lax.rsqrt(ms + EPS) * ng_ref[...]).astype(BF16)


def _pro_lru(ins):
    hf_ref, hb_ref, gg_ref = ins
    hsum = (hf_ref[...].astype(F32) + hb_ref[...].astype(F32))
    return (hsum * gg_ref[...].astype(F32)).astype(BF16)


def _mmres_kernel(*refs, pro, n_in, k_gate, seq):
    ins = refs[:n_in]
    w_ref, h_ref, modl_ref, modc_ref, o_ref = refs[n_in:n_in + 5]
    tm = h_ref.shape[0]
    row0 = pl.program_id(1) * tm
    y = _dot(pro(ins), w_ref[...])
    gate = _gate_rows(modl_ref, modc_ref, k_gate, tm, row0, seq)
    o_ref[...] = h_ref[...] + gate * y


def _mmres(ins, in_widths, w, hh, modl, modc, *, pro, tm, k_gate, seq, name, rows=None, row_ins=()):
    b, nt, d = hh.shape
    rows = nt if rows is None else rows
    k = w.shape[0]
    grid = (b, rows // tm)
    in_specs = []
    for item, wd in zip(ins, in_widths):
        if isinstance(item, tuple):
            in_specs.append(pl.BlockSpec((None, None, tm, wd), functools.partial(lambda bb, i, s: (s, bb, i, 0), s=item[1])))
        else:
            in_specs.append(pl.BlockSpec((None, tm, wd), lambda bb, i: (bb, i, 0)))
    ins = [item[0] if isinstance(item, tuple) else item for item in ins]
    in_specs += [pl.BlockSpec((1, r.shape[1]), lambda bb, i: (0, 0)) for r in row_ins]
    in_specs += [pl.BlockSpec((k, d), lambda bb, i: (0, 0)),
                 pl.BlockSpec((None, tm, d), lambda bb, i: (bb, i, 0)),
                 pl.BlockSpec((None, 6, d), lambda bb, i: (bb, 0, 0)),
                 pl.BlockSpec((None, 6, d), lambda bb, i: (0, 0, 0))]
    kern = functools.partial(_mmres_kernel, pro=pro, n_in=len(ins) + len(row_ins), k_gate=k_gate, seq=seq)
    return pl.pallas_call(
        kern, grid=grid, in_specs=in_specs,
        out_specs=pl.BlockSpec((None, tm, d), lambda bb, i: (bb, i, 0)),
        out_shape=jax.ShapeDtypeStruct((b, rows, d), F32),
        compiler_params=_cparams("parallel", "parallel"),
        name=name,
    )(*ins, *row_ins, w, hh, modl, modc)


def _swiglu_tile(x, wg_ref, wu_ref, wd_ref):
    gate = _dot(x, wg_ref[...].astype(BF16))
    up = _dot(x, wu_ref[...].astype(BF16))
    return _dot((_silu(gate) * up).astype(BF16), wd_ref[...].astype(BF16))


def _ffn_kernel(h_ref, modl_ref, modc_ref, g_ref, wg_ref, wu_ref, wd_ref, o_ref, u_scr, acc_scr, *, seq):
    tm = h_ref.shape[0]
    row0 = pl.program_id(1) * tm
    j = pl.program_id(2)

    @pl.when(j == 0)
    def _():
        u_scr[...] = _modulated(h_ref[...], modl_ref, modc_ref, g_ref, 3, 4, row0, seq).astype(BF16)
        acc_scr[...] = jnp.zeros_like(acc_scr)

    acc_scr[...] += _swiglu_tile(u_scr[...], wg_ref, wu_ref, wd_ref)

    @pl.when(j == pl.num_programs(2) - 1)
    def _():
        gate = _gate_rows(modl_ref, modc_ref, 5, tm, row0, seq)
        o_ref[...] = h_ref[...] + gate * acc_scr[...]


def _dense_ffn(hh, modl, modc, g, wg, wu, wd, *, tm, th, seq):
    b, nt, d = hh.shape
    hid = wg.shape[1]
    grid = (b, nt // tm, hid // th)
    return pl.pallas_call(
        functools.partial(_ffn_kernel, seq=seq),
        grid=grid,
        in_specs=[pl.BlockSpec((None, tm, d), lambda bb, i, j: (bb, i, 0)),
                  pl.BlockSpec((None, 6, d), lambda bb, i, j: (bb, 0, 0)),
                  pl.BlockSpec((None, 6, d), lambda bb, i, j: (0, 0, 0)),
                  pl.BlockSpec((1, d), lambda bb, i, j: (0, 0)),
                  pl.BlockSpec((d, th), lambda bb, i, j: (0, j)),
                  pl.BlockSpec((d, th), lambda bb, i, j: (0, j)),
                  pl.BlockSpec((th, d), lambda bb, i, j: (j, 0))],
        out_specs=pl.BlockSpec((None, tm, d), lambda bb, i, j: (bb, i, 0)),
        out_shape=jax.ShapeDtypeStruct((b, nt, d), F32),
        scratch_shapes=[pltpu.VMEM((tm, d), BF16), pltpu.VMEM((tm, d), F32)],
        compiler_params=_cparams("parallel", "parallel", "arbitrary"),
        name="dense_ffn",
    )(hh, modl, modc, g, wg, wu, wd)


def _sel01(x_f32, e_bf16):
    x1, x2, x3 = _split3(x_f32)
    return _dot(x1, e_bf16) + _dot(x2, e_bf16) + _dot(x3, e_bf16)


def _ssd_chunk(xbc, dt, nega, dskip, expand, lane0, state_ref, o_ref, *, reverse, add_skip):
    q = xbc.shape[0]
    g_n = SSM_GROUPS * SSM_STATE
    inner = xbc.shape[1] - 2 * g_n
    heads = inner // SSM_HEAD_DIM
    rp = inner // SSM_GROUPS
    r_heads = heads // SSM_GROUPS

    rows = lax.broadcasted_iota(jnp.int32, (q, q), 0)
    cols = lax.broadcasted_iota(jnp.int32, (q, q), 1)
    keep = (rows <= cols) if reverse else (rows >= cols)
    tri = keep.astype(BF16)

    la = dt * nega
    ac = _dot_exact_lhs(tri, la)
    ac_t = jnp.transpose(ac)
    last = 0 if reverse else q - 1
    ac_end = ac[last:last + 1, :]

    x = xbc[:, :inner].astype(F32)
    decay_in = jnp.exp(ac)
    xdt = x * _dot(dt.astype(BF16), expand)
    to_end = _dot(jnp.exp(ac_end - ac).astype(BF16), expand)
    from_start = _dot(decay_in.astype(BF16), expand)
    total = jnp.broadcast_to(decay_in[last:last + 1, :], (2 * SUBLANES, LANE))
    chunk_decay = _sel01(total, expand)[0:1, :]
    xdt_b = xdt.astype(BF16)
    xend_b = (xdt * to_end).astype(BF16)

    for g in range(SSM_GROUPS):
        bg = xbc[:, inner + g * SSM_STATE: inner + (g + 1) * SSM_STATE]
        cg = xbc[:, inner + g_n + g * SSM_STATE: inner + g_n + (g + 1) * SSM_STATE]
        cb = _dot_nt(cg, bg).astype(BF16)
        pieces = []
        for r in range(r_heads):
            hd = g * r_heads + r
            ln = lane0 + hd
            diff = ac[:, ln:ln + 1] - ac_t[ln:ln + 1, :]
            dec = jnp.exp(jnp.minimum(diff, 0.0).astype(BF16))
            m = jnp.where(keep, cb * dec, jnp.zeros_like(dec))
            pieces.append(_dot(m, xdt_b[:, hd * SSM_HEAD_DIM:(hd + 1) * SSM_HEAD_DIM]))
        y_diag = jnp.concatenate(pieces, axis=1)
        sl = slice(g * rp, (g + 1) * rp)
        h_in = state_ref[g]
        y_off = _dot(cg, h_in.astype(BF16)) * from_start[:, sl]
        states = _dot_tn(bg, xend_b[:, sl])
        state_ref[g] = h_in * chunk_decay[:, sl] + states
        y = y_diag + y_off
        if add_skip:
            y = y + x[:, sl] * dskip[:, sl]
        o_ref[:, sl] = y.astype(o_ref.dtype)


def _ssd_kernel(xf_ref, xb_ref, dtf_ref, dtb_ref, nega_ref, dskip_ref, exp_ref, yf_ref, yb_ref, st_ref, *, heads):
    @pl.when(pl.program_id(1) == 0)
    def _():
        st_ref[...] = jnp.zeros_like(st_ref)

    _ssd_chunk(xf_ref[...], dtf_ref[...], nega_ref[...], dskip_ref[...], exp_ref[0], 0, st_ref.at[0], yf_ref,
               reverse=False, add_skip=True)
    _ssd_chunk(xb_ref[...], dtb_ref[...], nega_ref[...], dskip_ref[...], exp_ref[1], heads, st_ref.at[1], yb_ref,
               reverse=True, add_skip=False)


def _ssd(xbc, dt, nega, dskip, expand, *, seq, heads):
    b, nt, width = xbc.shape
    q = SSM_CHUNK
    inner = heads * SSM_HEAD_DIM
    nch = nt // q
    ncl = seq // q

    def fwd(bb, j):
        return (bb, (j + ncl) % nch, 0)

    def bwd(bb, j):
        return (bb, nch - 1 - j, 0)

    return pl.pallas_call(
        functools.partial(_ssd_kernel, heads=heads),
        grid=(b, nch),
        in_specs=[pl.BlockSpec((None, q, width), fwd),
                  pl.BlockSpec((None, q, width), bwd),
                  pl.BlockSpec((None, q, LANE), fwd),
                  pl.BlockSpec((None, q, LANE), bwd),
                  pl.BlockSpec((1, LANE), lambda bb, j: (0, 0)),
                  pl.BlockSpec((1, inner), lambda bb, j: (0, 0)),
                  pl.BlockSpec((2, LANE, inner), lambda bb, j: (0, 0, 0))],
        out_specs=[pl.BlockSpec((None, q, inner), fwd),
                   pl.BlockSpec((None, q, inner), bwd)],
        out_shape=[jax.ShapeDtypeStruct((b, nt, inner), BF16)] * 2,
        scratch_shapes=[pltpu.VMEM((2, SSM_GROUPS, SSM_STATE, inner // SSM_GROUPS), F32)],
        compiler_params=_cparams("parallel", "arbitrary"),
        name="ssd_scan",
    )(xbc, xbc, dt, dt, nega, dskip, expand)


def _attn_heads(q, k, v, o_ref):
    hd = ATT_HEAD_DIM
    n_q = q.shape[1] // hd
    rep = n_q // ATT_KV_HEADS
    tq = q.shape[0]
    for g in range(ATT_KV_HEADS):
        kg = k[:, g * hd:(g + 1) * hd]
        vg = v[:, g * hd:(g + 1) * hd]
        v_aug = jnp.concatenate([vg, jnp.ones_like(vg)], axis=1)
        qg = jnp.concatenate([q[:, (g * rep + r) * hd:(g * rep + r + 1) * hd] for r in range(rep)], axis=0)
        s_t = _dot_nt(kg, qg).astype(BF16)
        m = jnp.max(s_t, axis=0, keepdims=True)
        p_t = jnp.exp2(s_t - m)
        o_aug = _dot_tn(v_aug, p_t)
        o = jnp.transpose(o_aug[:hd, :] / o_aug[hd:hd + 1, :])
        og = jnp.concatenate([o[r * tq:(r + 1) * tq, :] for r in range(rep)], axis=1)
        o_ref[:, g * rep * hd:(g + 1) * rep * hd] = og.astype(o_ref.dtype)


def _attn_kernel(q_ref, k_ref, v_ref, o_ref, *, seq):
    tq = q_ref.shape[0]
    is_lat = pl.program_id(1) * tq < seq

    @pl.when(is_lat)
    def _():
        _attn_heads(q_ref[...], k_ref[...], v_ref[...], o_ref)

    @pl.when(jnp.logical_not(is_lat))
    def _():
        _attn_heads(q_ref[...], k_ref[seq:, :], v_ref[seq:, :], o_ref)


def _attention(q, k, v, *, seq, tq):
    b, nt, dq = q.shape
    dk = k.shape[2]
    return pl.pallas_call(
        functools.partial(_attn_kernel, seq=seq),
        grid=(b, nt // tq),
        in_specs=[pl.BlockSpec((None, tq, dq), lambda bb, i: (bb, i, 0)),
                  pl.BlockSpec((None, nt, dk), lambda bb, i: (bb, 0, 0)),
                  pl.BlockSpec((None, nt, dk), lambda bb, i: (bb, 0, 0))],
        out_specs=pl.BlockSpec((None, tq, dq), lambda bb, i: (bb, i, 0)),
        out_shape=jax.ShapeDtypeStruct((b, nt, dq), BF16),
        compiler_params=_cparams("parallel", "parallel"),
        name="gqa_attention",
    )(q, k, v)


def _pool_kernel(v_ref, vp_ref, vn_ref, h_ref, modl_ref, modc_ref, wg_ref, sc_ref, o_ref, *, seq, nt):
    tm = v_ref.shape[0]
    halo = vp_ref.shape[0]
    i = pl.program_id(1)
    row0 = i * tm
    v = v_ref[...]
    vext = jnp.concatenate([vp_ref[...], v, vn_ref[...]], axis=0)
    vhi, vlo = _split2(vext)
    t = row0 + lax.broadcasted_iota(jnp.int32, (tm, 1), 0)
    s = row0 - halo + lax.broadcasted_iota(jnp.int32, (1, tm + 2 * halo), 1)
    in_ctx = t >= seq
    seg_lo = jnp.where(in_ctx, seq, 0)
    seg_hi = jnp.where(in_ctx, nt, seq)
    grp = v.shape[1] // len(POOL_WINDOWS)
    outs = []
    for gi, win in enumerate(POOL_WINDOWS):
        lo = jnp.maximum(t - win // 2, seg_lo)
        hi = jnp.minimum(t + win // 2, seg_hi)
        band = ((s >= lo) & (s < hi)).astype(BF16)
        sl = slice(gi * grp, (gi + 1) * grp)
        summed = _dot(band, vhi[:, sl]) + _dot(band, vlo[:, sl])
        pooled = summed / (hi - lo).astype(F32) - v[:, sl]
        outs.append(_dot(pooled.astype(BF16), wg_ref[gi]))
    y = jnp.concatenate(outs, axis=1) * sc_ref[...]
    gate = _gate_rows(modl_ref, modc_ref, 2, tm, row0, seq)
    o_ref[...] = h_ref[...] + gate * y


def _pool(v, hh, modl, modc, w_grp, scale, *, seq, tm):
    b, nt, d = hh.shape
    halo = POOL_HALO
    per = tm // halo
    nhb = nt // halo
    grp = w_grp.shape[1]
    return pl.pallas_call(
        functools.partial(_pool_kernel, seq=seq, nt=nt),
        grid=(b, nt // tm),
        in_specs=[pl.BlockSpec((None, tm, d), lambda bb, i: (bb, i, 0)),
                  pl.BlockSpec((None, halo, d), lambda bb, i: (bb, jnp.maximum(i * per - 1, 0), 0)),
                  pl.BlockSpec((None, halo, d), lambda bb, i: (bb, jnp.minimum((i + 1) * per, nhb - 1), 0)),
                  pl.BlockSpec((None, tm, d), lambda bb, i: (bb, i, 0)),
                  pl.BlockSpec((None, 6, d), lambda bb, i: (bb, 0, 0)),
                  pl.BlockSpec((None, 6, d), lambda bb, i: (0, 0, 0)),
                  pl.BlockSpec((len(POOL_WINDOWS), grp, grp), lambda bb, i: (0, 0, 0)),
                  pl.BlockSpec((1, d), lambda bb, i: (0, 0))],
        out_specs=pl.BlockSpec((None, tm, d), lambda bb, i: (bb, i, 0)),
        out_shape=jax.ShapeDtypeStruct((b, nt, d), F32),
        compiler_params=_cparams("parallel", "parallel"),
        name="pool_mixer",
    )(v, v, v, hh, modl, modc, w_grp, scale)


def _lru_scan_chunk(a_scr, bx_scr, h_scr, carry_scr, *, reverse):
    tc = a_scr.shape[0]

    def step(tt, h):
        t = tc - 1 - tt if reverse else tt
        h = a_scr[pl.ds(t, 1), :] * h + bx_scr[pl.ds(t, 1), :]
        h_scr[pl.ds(t, 1), :] = h
        return h

    carry_scr[...] = lax.fori_loop(0, tc, step, carry_scr[...], unroll=8)


def _lru_kernel(x_ref, w_ref, b_ref, lam_ref, o_ref, a_scr, bx_scr, h_scr, carry_scr):
    d_dir = pl.program_id(1)
    tc, width = x_ref.shape
    bw = width // LRU_BLOCKS

    @pl.when(pl.program_id(2) == 0)
    def _():
        carry_scr[...] = jnp.zeros_like(carry_scr)

    xb = x_ref[...]
    for k in range(LRU_BLOCKS):
        sl = slice(k * bw, (k + 1) * bw)
        pre = _dot(xb[:, sl], w_ref[k])
        r_gate = _sigmoid(pre[:, :bw] + b_ref[0:1, sl])
        i_gate = _sigmoid(pre[:, bw:] + b_ref[1:2, sl])
        neg_lam = -lam_ref[:, sl]
        softplus = jnp.maximum(neg_lam, 0.0) + jnp.log1p(jnp.exp(-jnp.abs(neg_lam)))
        log_a = (-LRU_C) * r_gate * softplus
        a = jnp.exp(log_a)
        a_scr[:, sl] = a
        bx_scr[:, sl] = jnp.sqrt(-jnp.tanh(log_a) * (a * a + 1.0)) * (i_gate * xb[:, sl].astype(F32))

    @pl.when(d_dir == 0)
    def _():
        _lru_scan_chunk(a_scr, bx_scr, h_scr, carry_scr, reverse=False)

    @pl.when(d_dir == 1)
    def _():
        _lru_scan_chunk(a_scr, bx_scr, h_scr, carry_scr, reverse=True)

    o_ref[...] = h_scr[...].astype(o_ref.dtype)


def _lru(xr, w_cat, bias, lam, *, seq, tc):
    b, nt, width = xr.shape
    nc = nt // tc
    nl = seq // tc

    def chunk(d, j):
        return d * (nc - 1 - j) + (1 - d) * ((j + nl) % nc)

    return pl.pallas_call(
        _lru_kernel,
        grid=(b, 2, nc),
        in_specs=[pl.BlockSpec((None, tc, width), lambda bb, d, j: (bb, chunk(d, j), 0)),
                  pl.BlockSpec((None,) + w_cat.shape[1:], lambda bb, d, j: (d, 0, 0, 0)),
                  pl.BlockSpec((None, 2, width), lambda bb, d, j: (d, 0, 0)),
                  pl.BlockSpec((None, 1, width), lambda bb, d, j: (d, 0, 0))],
        out_specs=pl.BlockSpec((None, None, tc, width), lambda bb, d, j: (d, bb, chunk(d, j), 0)),
        out_shape=jax.ShapeDtypeStruct((2, b, nt, width), BF16),
        scratch_shapes=[pltpu.VMEM((tc, width), F32)] * 3 + [pltpu.VMEM((1, width), F32)],
        compiler_params=_cparams("parallel", "arbitrary", "arbitrary"),
        name="rglru_scan",
    )(xr, w_cat, bias, lam)


def _router_kernel(h_ref, modl_ref, modc_ref, g_ref, whi_ref, wlo_ref, u_ref, meta_ref, cnt_ref, carry_scr, *,
                   seq, n_exp):
    tm = h_ref.shape[0]
    row0 = pl.program_id(1) * tm
    first = (pl.program_id(0) == 0) & (pl.program_id(1) == 0)

    @pl.when(first)
    def _():
        carry_scr[...] = jnp.zeros_like(carry_scr)

    u = _modulated(h_ref[...], modl_ref, modc_ref, g_ref, 3, 4, row0, seq)
    u_ref[...] = u.reshape(u_ref.shape)
    uhi, ulo = _split2(u)
    logits = _dot(uhi, whi_ref[...]) + _dot(ulo, whi_ref[...]) + _dot(uhi, wlo_ref[...])
    lane = lax.broadcasted_iota(jnp.int32, (tm, LANE), 1)
    neg = jnp.float32(-jnp.inf)
    logits = jnp.where(lane < n_exp, logits, neg)
    v0 = jnp.max(logits, axis=-1, keepdims=True)
    lane_f = lane.astype(F32)
    i0 = jnp.min(jnp.where(logits == v0, lane_f, float(LANE)), axis=-1, keepdims=True)
    rest = jnp.where(lane_f == i0, neg, logits)
    v1 = jnp.max(rest, axis=-1, keepdims=True)
    i1 = jnp.min(jnp.where(rest == v1, lane_f, float(LANE)), axis=-1, keepdims=True)
    g0 = 1.0 / (1.0 + jnp.exp(v1 - v0))
    g1 = 1.0 - g0
    sel0 = lane_f == i0
    sel1 = lane_f == i1
    onehot = (sel0 | sel1).astype(BF16)
    rr = lax.broadcasted_iota(jnp.int32, (tm, tm), 0)
    cc = lax.broadcasted_iota(jnp.int32, (tm, tm), 1)
    before = (cc < rr).astype(BF16)
    prefix = _dot(before, onehot) + carry_scr[...]
    r0 = jnp.sum(jnp.where(sel0, prefix, 0.0), axis=-1, keepdims=True)
    r1 = jnp.sum(jnp.where(sel1, prefix, 0.0), axis=-1, keepdims=True)
    carry_scr[...] += jnp.sum(onehot.astype(F32), axis=0, keepdims=True)
    cnt_ref[...] = carry_scr[...]
    meta = jnp.where(lane == 0, i0, 0.0)
    meta = jnp.where(lane == 1, i1, meta)
    meta = jnp.where(lane == 2, g0, meta)
    meta = jnp.where(lane == 3, g1, meta)
    meta = jnp.where(lane == 4, r0, meta)
    meta = jnp.where(lane == 5, r1, meta)
    meta_ref[...] = meta


def _router(hh, modl, modc, g, whi, wlo, *, rows, tm, seq, n_exp):
    b, nt, d = hh.shape
    return pl.pallas_call(
        functools.partial(_router_kernel, seq=seq, n_exp=n_exp),
        grid=(b, rows // tm),
        in_specs=[pl.BlockSpec((None, tm, d), lambda bb, i: (bb, i, 0)),
                  pl.BlockSpec((None, 6, d), lambda bb, i: (bb, 0, 0)),
                  pl.BlockSpec((None, 6, d), lambda bb, i: (0, 0, 0)),
                  pl.BlockSpec((1, d), lambda bb, i: (0, 0)),
                  pl.BlockSpec((d, LANE), lambda bb, i: (0, 0)),
                  pl.BlockSpec((d, LANE), lambda bb, i: (0, 0))],
        out_specs=[pl.BlockSpec((None, tm, d // LANE, LANE), lambda bb, i: (bb, i, 0, 0)),
                   pl.BlockSpec((None, tm, LANE), lambda bb, i: (bb, i, 0)),
                   pl.BlockSpec((1, LANE), lambda bb, i: (0, 0))],
        out_shape=[jax.ShapeDtypeStruct((b, rows, d // LANE, LANE), F32),
                   jax.ShapeDtypeStruct((b, rows, LANE), F32),
                   jax.ShapeDtypeStruct((1, LANE), F32)],
        scratch_shapes=[pltpu.VMEM((1, LANE), F32)],
        compiler_params=_cparams("arbitrary", "arbitrary"),
        name="moe_router",
    )(hh, modl, modc, g, whi, wlo)


def _dispatch_kernel(dest_ref, fill_ref, u_ref, xg_hbm, zero_scr, sem, *, tm, rows_per_b):
    base = (pl.program_id(0) * rows_per_b + pl.program_id(1) * tm) * TOP_K

    @pl.when((pl.program_id(0) == 0) & (pl.program_id(1) == 0))
    def _():
        zero_scr[...] = jnp.zeros_like(zero_scr)

        def fill(blk, c):
            @pl.when(fill_ref[blk] != 0)
            def _():
                row0 = pl.multiple_of(blk * MOE_BLOCK, MOE_BLOCK)
                cp = pltpu.make_async_copy(zero_scr, xg_hbm.at[pl.ds(row0, MOE_BLOCK)], sem)
                cp.start()
                cp.wait()
            return c

        lax.fori_loop(0, fill_ref.shape[0], fill, 0)

    def issue(r, c):
        for k in range(TOP_K):
            pltpu.make_async_copy(u_ref.at[r], xg_hbm.at[dest_ref[base + r * TOP_K + k]], sem).start()
        return c

    lax.fori_loop(0, tm, issue, 0, unroll=8)
    for k in range(TOP_K):
        pltpu.make_async_copy(u_ref, xg_hbm.at[pl.ds(0, tm)], sem).wait()


def _dispatch(dest, fill, u, *, tm):
    b, rows, sub, lane = u.shape
    grid_spec = pltpu.PrefetchScalarGridSpec(
        num_scalar_prefetch=2, grid=(b, rows // tm),
        in_specs=[pl.BlockSpec((tm, sub, lane), lambda bb, i, ds, fl: (bb * (rows // tm) + i, 0, 0))],
        out_specs=pl.BlockSpec(memory_space=pl.ANY),
        scratch_shapes=[pltpu.VMEM((MOE_BLOCK, sub, lane), F32), pltpu.SemaphoreType.DMA(())])
    return pl.pallas_call(
        functools.partial(_dispatch_kernel, tm=tm, rows_per_b=rows),
        grid_spec=grid_spec,
        out_shape=jax.ShapeDtypeStruct((fill.shape[0] * MOE_BLOCK, sub, lane), F32),
        compiler_params=_cparams("arbitrary", "arbitrary"),
        name="moe_dispatch",
    )(dest, fill, u.reshape(b * rows, sub, lane))


def _expert_kernel(be_ref, nu_ref, x_ref, wg_ref, wu_ref, wd_ref, y_ref, xb_scr, acc_scr):
    blk = pl.program_id(0)
    j = pl.program_id(1)
    used = blk < nu_ref[0]

    @pl.when(used & (j == 0))
    def _():
        xb_scr[...] = x_ref[...].reshape(xb_scr.shape).astype(BF16)
        acc_scr[...] = jnp.zeros_like(acc_scr)

    @pl.when(used)
    def _():
        acc_scr[...] += _swiglu_tile(xb_scr[...], wg_ref, wu_ref, wd_ref)

    @pl.when(used & (j == pl.num_programs(1) - 1))
    def _():
        y_ref[...] = acc_scr[...].reshape(y_ref.shape)

    @pl.when(jnp.logical_not(used) & (j == pl.num_programs(1) - 1))
    def _():
        y_ref[...] = jnp.zeros_like(y_ref)


def _experts(block_expert, n_used, xg, wg, wu, wd, *, th, layer):
    p, sub, lane = xg.shape
    d = sub * lane
    hid = wg.shape[3]
    nb = p // MOE_BLOCK
    nj = hid // th

    def blk_eff(blk, nu):
        return jnp.minimum(blk, nu[0] - 1)

    def j_eff(blk, j, nu):
        return jnp.where(blk < nu[0], j, nj - 1)

    def w_in_map(blk, j, be, nu):
        return (layer, be[blk_eff(blk, nu)], 0, j_eff(blk, j, nu))

    def w_out_map(blk, j, be, nu):
        return (layer, be[blk_eff(blk, nu)], j_eff(blk, j, nu), 0)

    grid_spec = pltpu.PrefetchScalarGridSpec(
        num_scalar_prefetch=2, grid=(nb, nj),
        in_specs=[pl.BlockSpec((MOE_BLOCK, sub, lane), lambda blk, j, be, nu: (blk_eff(blk, nu), 0, 0)),
                  pl.BlockSpec((None, None, d, th), w_in_map),
                  pl.BlockSpec((None, None, d, th), w_in_map),
                  pl.BlockSpec((None, None, th, d), w_out_map)],
        out_specs=pl.BlockSpec((MOE_BLOCK, sub, lane), lambda blk, j, be, nu: (blk, 0, 0)),
        scratch_shapes=[pltpu.VMEM((MOE_BLOCK, d), BF16), pltpu.VMEM((MOE_BLOCK, d), F32)])
    return pl.pallas_call(
        _expert_kernel, grid_spec=grid_spec,
        out_shape=jax.ShapeDtypeStruct((p, sub, lane), F32),
        compiler_params=_cparams("arbitrary", "arbitrary"),
        name="moe_experts",
    )(block_expert, n_used, xg, wg, wu, wd)


def _combine_kernel(dest_ref, yg_hbm, meta_ref, h_ref, modl_ref, modc_ref, o_ref, y0_scr, y1_scr, sem, *, tm,
                    rows_per_b, seq):
    bb = pl.program_id(0)
    i = pl.program_id(1)
    base = (bb * rows_per_b + i * tm) * TOP_K
    bufs = (y0_scr, y1_scr)

    def issue(r, c):
        for k in range(TOP_K):
            pltpu.make_async_copy(yg_hbm.at[dest_ref[base + r * TOP_K + k]], bufs[k].at[r], sem).start()
        return c

    lax.fori_loop(0, tm, issue, 0, unroll=8)
    for k in range(TOP_K):
        pltpu.make_async_copy(yg_hbm.at[pl.ds(0, tm)], bufs[k], sem).wait()
    meta = meta_ref[...]
    flat = h_ref.shape
    y = meta[:, 2:3] * y0_scr[...].reshape(flat) + meta[:, 3:4] * y1_scr[...].reshape(flat)
    gate = _gate_rows(modl_ref, modc_ref, 5, tm, i * tm, seq)
    o_ref[...] = h_ref[...] + gate * y


def _combine(dest, yg, meta, hh, modl, modc, *, tm, seq):
    b, rows, _ = meta.shape
    d = hh.shape[2]
    grid_spec = pltpu.PrefetchScalarGridSpec(
        num_scalar_prefetch=1, grid=(b, rows // tm),
        in_specs=[pl.BlockSpec(memory_space=pl.ANY),
                  pl.BlockSpec((None, tm, LANE), lambda bb, i, ds: (bb, i, 0)),
                  pl.BlockSpec((None, tm, d), lambda bb, i, ds: (bb, i, 0)),
                  pl.BlockSpec((None, 6, d), lambda bb, i, ds: (bb, 0, 0)),
                  pl.BlockSpec((None, 6, d), lambda bb, i, ds: (0, 0, 0))],
        out_specs=pl.BlockSpec((None, tm, d), lambda bb, i, ds: (bb, i, 0)),
        scratch_shapes=[pltpu.VMEM((tm,) + yg.shape[1:], F32)] * 2 + [pltpu.SemaphoreType.DMA(())])
    return pl.pallas_call(
        functools.partial(_combine_kernel, tm=tm, rows_per_b=rows, seq=seq),
        grid_spec=grid_spec,
        out_shape=jax.ShapeDtypeStruct((b, rows, d), F32),
        compiler_params=_cparams("arbitrary", "arbitrary"),
        name="moe_combine",
    )(dest, yg, meta, hh, modl, modc)


def _moe(hh, modl, modc, g, w_router, wg, wu, wd, *, rows, seq, layer):
    b, nt, d = hh.shape
    n_exp = w_router.shape[1]
    wr = jnp.pad(w_router, ((0, 0), (0, LANE - n_exp)))
    whi, wlo = _split2(wr)
    tm = _pick(rows, (768, 512, 256))
    u, meta, counts = _router(hh, modl, modc, g, whi, wlo, rows=rows, tm=tm, seq=seq, n_exp=n_exp)

    cnt = counts[0, :n_exp].astype(jnp.int32)
    padded = (cnt + MOE_BLOCK - 1) // MOE_BLOCK * MOE_BLOCK
    pend = jnp.cumsum(padded)
    pstart = pend - padded
    n_assign = b * rows * TOP_K
    nb = -(-(n_assign + n_exp * (MOE_BLOCK - 1)) // MOE_BLOCK)
    idx = meta[:, :, 0:TOP_K].astype(jnp.int32)
    rank = meta[:, :, 4:4 + TOP_K].astype(jnp.int32)
    dest = (sum(jnp.where(idx == e, pstart[e], 0) for e in range(n_exp)) + rank).reshape(-1)
    blk_start = jnp.arange(nb, dtype=jnp.int32) * MOE_BLOCK
    block_expert = jnp.minimum(jnp.sum(pend[None, :] <= blk_start[:, None], axis=1), n_exp - 1).astype(jnp.int32)
    n_used = (pend[-1:] // MOE_BLOCK).astype(jnp.int32)

    blk_id = jnp.arange(nb, dtype=jnp.int32)
    holds_padding = jnp.any((pend[None, :] // MOE_BLOCK - 1 == blk_id[:, None]) & (padded[None, :] > 0), axis=1)
    fill = (holds_padding | (blk_id >= n_used[0])).astype(jnp.int32)
    td = tm
    xg = _dispatch(dest, fill, u, tm=td)
    yg = _experts(block_expert, n_used, xg, wg, wu, wd, th=_pick(wg.shape[3], (512, 256, 128)), layer=layer)
    return _combine(dest, yg, meta, hh, modl, modc, tm=td, seq=seq)


def _rope_tables(seq, nt):
    rows = seq // GRID_W
    row = jnp.repeat(jnp.arange(rows), GRID_W).astype(F32)
    col = jnp.tile(jnp.arange(GRID_W), rows).astype(F32)
    n_freq = ATT_HEAD_DIM // 4
    inv_freq = ROPE_THETA ** (-jnp.arange(n_freq, dtype=F32) / n_freq)
    ang = jnp.concatenate([row[:, None] * inv_freq, col[:, None] * inv_freq], axis=-1)
    cos, sin = jnp.cos(ang), jnp.sin(ang)
    cos_h = jnp.concatenate([cos, cos], axis=-1)
    sin_h = jnp.concatenate([-sin, sin], axis=-1)
    reps = LANE // ATT_HEAD_DIM
    cos_t = jnp.tile(cos_h, (1, reps))
    sin_t = jnp.tile(sin_h, (1, reps))
    pad = nt - seq
    cos_t = jnp.concatenate([cos_t, jnp.ones((pad, LANE), F32)], axis=0)
    sin_t = jnp.concatenate([sin_t, jnp.zeros((pad, LANE), F32)], axis=0)
    return cos_t, sin_t


def _head_mean_matrix(n):
    r = jnp.arange(n) // ATT_HEAD_DIM
    return (r[:, None] == r[None, :]).astype(F32).astype(BF16) * jnp.asarray(1.0 / ATT_HEAD_DIM, BF16)


def kernel(x, c, ctx, c_ctx, w_mod, b_mod, norm_g, ssm_w_in, ssm_conv_w, ssm_conv_b, ssm_dt_bias, ssm_a_log, ssm_d, ssm_norm_g, ssm_w_out, att_w_qkv, att_q_norm, att_k_norm, att_w_out, pool_w_in, pool_w_grp, pool_scale, lru_w_in, lru_conv_w, lru_conv_b, lru_gate_w, lru_gate_b, lru_lambda, lru_w_out, ffn_w_gate, ffn_w_up, ffn_w_down, moe_w_router, moe_w_gate, moe_w_up, moe_w_down):
    b, seq, d = x.shape
    nctx = ctx.shape[1]
    nt = seq + nctx
    depth = w_mod.shape[0]
    n_mixers = 4

    bp = -(-(b + 1) // 8) * 8
    s_rows = jnp.concatenate([c, c_ctx[None, :], jnp.zeros((bp - b - 1, d), F32)], axis=0)
    mods = _mod_vectors(s_rows, w_mod, b_mod)
    modl_all = mods[:, :b].reshape(depth, b, 6, d)
    modc_all = mods[:, b:b + 1].reshape(depth, 1, 6, d)

    hh = jnp.concatenate([x, ctx], axis=1)
    tm_row = _pick(nt, (768, 512, 256))
    cos_t, sin_t = _rope_tables(seq, nt)
    out = None

    for i in range(depth):
        last = i == depth - 1
        kind, j = i % n_mixers, i // n_mixers
        modl, modc = modl_all[i], modc_all[i]
        g1 = norm_g[i, 0][None, :]
        g2 = norm_g[i, 1][None, :]
        proj = functools.partial(_proj, hh, modl, modc, g1, seq=seq, k_shift=0, k_scale=1)

        if kind == 0:
            inner = ssm_d.shape[1] * SSM_HEAD_DIM
            heads = ssm_d.shape[1]
            conv_dim = ssm_conv_w.shape[2]
            w_in = ssm_w_in[j].astype(BF16)
            w_z = w_in[:, :inner]
            w_xbc = w_in[:, inner:inner + conv_dim]
            w_dt = jnp.pad(w_in[:, inner + conv_dim:], ((0, 0), (0, LANE - 2 * heads)))
            dt_bias = jnp.pad(ssm_dt_bias[j].reshape(1, 2 * heads), ((0, 0), (0, LANE - 2 * heads)))
            z = proj(w_z, epi=_epi_plain, tm=tm_row, tn=inner, out_dtype=BF16, name="ssd_proj_z")
            xbc = proj(w_xbc, epi=_epi_conv_silu, extra=(ssm_conv_w[j], ssm_conv_b[j][None, :]),
                       extra_specs=(_col_spec(ssm_conv_w.shape[1], 512), _col_spec(1, 512)),
                       tm=nt, tn=512, out_dtype=BF16, name="ssd_proj_xbc")
            dt = proj(w_dt, epi=_epi_softplus, extra=(dt_bias,), extra_specs=(_col_spec(1, LANE),),
                      tm=tm_row, tn=LANE, out_dtype=F32, name="ssd_proj_dt")
            nega = jnp.pad(-jnp.exp(ssm_a_log[j].astype(F32)).reshape(1, 2 * heads), ((0, 0), (0, LANE - 2 * heads)))
            head_of = jnp.arange(inner)[None, :] // SSM_HEAD_DIM
            lanes = jnp.arange(LANE)[:, None]
            expand = jnp.stack([(lanes == head_of + dd * heads) for dd in range(2)]).astype(F32).astype(BF16)
            dskip = jnp.repeat(ssm_d[j].astype(F32), SSM_HEAD_DIM)[None, :]
            yf, yb = _ssd(xbc, dt, nega, dskip, expand, seq=seq, heads=heads)
            hh = _mmres((yf, yb, z), (inner, inner, inner), ssm_w_out[j].astype(BF16), hh, modl, modc,
                        pro=_pro_ssd, tm=tm_row, k_gate=2, seq=seq, name="ssd_out",
                        row_ins=(ssm_norm_g[j][None, :],))
        elif kind == 1:
            nq = att_w_out.shape[1]
            nk = ATT_KV_HEADS * ATT_HEAD_DIM
            w_qkv = att_w_qkv[j].astype(BF16)
            tab_specs = (pl.BlockSpec((tm_row, LANE), lambda bb, ii, jj: (ii, 0)),) * 2

            def qk_extra(n, gain):
                return ((_head_mean_matrix(n), jnp.tile(gain, n // ATT_HEAD_DIM)[None, :], cos_t, sin_t),
                        (pl.BlockSpec((n, n), lambda bb, ii, jj: (0, 0)),
                         pl.BlockSpec((1, n), lambda bb, ii, jj: (0, 0))) + tab_specs)

            ex, sp = qk_extra(nq, att_q_norm[j])
            q = proj(w_qkv[:, :nq], epi=functools.partial(_epi_qk, out_scale=ATT_HEAD_DIM ** -0.5 * math.log2(math.e)),
                     extra=ex,
                     extra_specs=sp, tm=tm_row, tn=nq, out_dtype=BF16, name="att_proj_q")
            ex, sp = qk_extra(nk, att_k_norm[j])
            k = proj(w_qkv[:, nq:nq + nk], epi=functools.partial(_epi_qk, out_scale=1.0), extra=ex,
                     extra_specs=sp, tm=tm_row, tn=nk, out_dtype=BF16, name="att_proj_k")
            v = proj(w_qkv[:, nq + nk:], epi=_epi_plain, tm=tm_row, tn=nk, out_dtype=BF16, name="att_proj_v")
            o = _attention(q, k, v, seq=seq, tq=_pick(nctx, (128,)))
            hh = _mmres((o,), (nq,), att_w_out[j].astype(BF16), hh, modl, modc, pro=_pro_plain, tm=tm_row,
                        k_gate=2, seq=seq, name="att_out")
        elif kind == 2:
            v = proj(pool_w_in[j].astype(BF16), epi=_epi_plain, tm=tm_row, tn=d, out_dtype=F32,
                     name="pool_proj")
            hh = _pool(v, hh, modl, modc, pool_w_grp[j].astype(BF16), pool_scale[j][None, :], seq=seq,
                       tm=_pick(nctx, (256, 128)))
        else:
            width = lru_w_out.shape[1]
            w_in = lru_w_in[j].astype(BF16)
            gg = proj(w_in[:, :width], epi=_epi_gelu, tm=tm_row, tn=width, out_dtype=BF16, name="lru_proj_gate")
            tn_x = _pick(width, (640, 256, 128))
            xr = proj(w_in[:, width:], epi=_epi_conv_bias, extra=(lru_conv_w[j], lru_conv_b[j][None, :]),
                      extra_specs=(_col_spec(lru_conv_w.shape[1], tn_x), _col_spec(1, tn_x)),
                      tm=nt, tn=tn_x, out_dtype=BF16, name="lru_proj_x")
            gw = lru_gate_w[j]
            w_cat = jnp.concatenate([gw[:, 0], gw[:, 1]], axis=-1).astype(BF16)
            hs = _lru(xr, w_cat, lru_gate_b[j], lru_lambda[j][:, None, :], seq=seq, tc=_pick(nctx, (256, 128)))
            rows = seq if last else nt
            hh = _mmres(((hs, 0), (hs, 1), gg), (width, width, width), lru_w_out[j].astype(BF16), hh, modl, modc,
                        pro=_pro_lru, tm=_pick(rows, (1024, 768, 512, 256)), k_gate=2, seq=seq, name="lru_out",
                        rows=rows)

        kf = i // 2
        if i % 2 == 0:
            hh = _dense_ffn(hh, modl, modc, g2, ffn_w_gate[kf].astype(BF16), ffn_w_up[kf].astype(BF16),
                            ffn_w_down[kf].astype(BF16), tm=tm_row,
                            th=_pick(ffn_w_gate.shape[2], (512, 256, 128)), seq=seq)
        else:
            rows = seq if last else nt
            hh = _moe(hh, modl, modc, g2, moe_w_router[kf], moe_w_gate, moe_w_up, moe_w_down, rows=rows, seq=seq,
                      layer=kf)
        if last:
            out = hh[:, :seq] if hh.shape[1] != seq else hh
    return out
```

```python
import functools
import math

import jax
import jax.numpy as jnp
from jax import lax
from jax.experimental import pallas as pl
from jax.experimental.pallas import tpu as pltpu

F32 = jnp.float32
BF16 = jnp.bfloat16
EPS = 1e-6

GRID_W = 64
SSM_HEAD_DIM = 64
SSM_GROUPS = 4
SSM_STATE = 128
SSM_CHUNK = 128
ATT_HEAD_DIM = 64
ATT_KV_HEADS = 4
ROPE_THETA = 10000.0
POOL_WINDOWS = (2, 4, 8, 16)
LRU_BLOCKS = 10
LRU_C = 8.0
TOP_K = 2
MOE_BLOCK = 1024

LANE = 128
SUBLANES = 8
POOL_HALO = 64
VMEM_LIMIT = 56 * 1024 * 1024


def _cparams(*sem):
    return pltpu.CompilerParams(dimension_semantics=sem, vmem_limit_bytes=VMEM_LIMIT)


def _pick(n, candidates):
    for c in candidates:
        if n % c == 0:
            return c
    raise ValueError(f"no tile in {candidates} divides {n}")


def _sigmoid(x):
    return 0.5 * jnp.tanh(0.5 * x) + 0.5


def _silu(x):
    return x * _sigmoid(x)


def _split2(x):
    hi = x.astype(BF16)
    lo = (x - hi.astype(F32)).astype(BF16)
    return hi, lo


def _split3(x):
    x1 = x.astype(BF16)
    r = x - x1.astype(F32)
    x2 = r.astype(BF16)
    x3 = (r - x2.astype(F32)).astype(BF16)
    return x1, x2, x3


def _dot(a, b):
    return jnp.dot(a, b, preferred_element_type=F32)


def _dot_nt(a, b):
    return lax.dot_general(a, b, (((1,), (1,)), ((), ())), preferred_element_type=F32)


def _dot_tn(a, b):
    return lax.dot_general(a, b, (((0,), (0,)), ((), ())), preferred_element_type=F32)


def _dot_exact_lhs(a_bf16, x_f32):
    x1, x2, x3 = _split3(x_f32)
    return _dot(a_bf16, x1) + _dot(a_bf16, x2) + _dot(a_bf16, x3)


def _modulated(h, modl_ref, modc_ref, g_ref, k_shift, k_scale, row0, seq):
    tm = h.shape[0]
    row = row0 + lax.broadcasted_iota(jnp.int32, (tm, 1), 0)
    is_ctx = row >= seq
    shift = jnp.where(is_ctx, modc_ref[k_shift:k_shift + 1, :], modl_ref[k_shift:k_shift + 1, :])
    scale = jnp.where(is_ctx, modc_ref[k_scale:k_scale + 1, :], modl_ref[k_scale:k_scale + 1, :])
    ms = jnp.mean(h * h, axis=-1, keepdims=True)
    y = h * lax.rsqrt(ms + EPS) * g_ref[...]
    return y * (1.0 + scale) + shift


def _gate_rows(modl_ref, modc_ref, k_gate, tm, row0, seq):
    row = row0 + lax.broadcasted_iota(jnp.int32, (tm, 1), 0)
    return jnp.where(row >= seq, modc_ref[k_gate:k_gate + 1, :], modl_ref[k_gate:k_gate + 1, :])


def _mod_kernel(s_ref, w_ref, b_ref, o_ref):
    s = _silu(s_ref[...])
    o_ref[...] = _dot(s.astype(BF16), w_ref[...].astype(BF16)) + b_ref[...]


def _mod_vectors(s_rows, w_mod, b_mod):
    depth, d, n6 = w_mod.shape
    bp = s_rows.shape[0]
    tn = _pick(n6, (1536, 1024, 512, 256, 128))
    return pl.pallas_call(
        _mod_kernel,
        grid=(depth, n6 // tn),
        in_specs=[pl.BlockSpec((bp, d), lambda l, j: (0, 0)),
                  pl.BlockSpec((None, d, tn), lambda l, j: (l, 0, j)),
                  pl.BlockSpec((None, 1, tn), lambda l, j: (l, 0, j))],
        out_specs=pl.BlockSpec((None, bp, tn), lambda l, j: (l, 0, j)),
        out_shape=jax.ShapeDtypeStruct((depth, bp, n6), F32),
        compiler_params=_cparams("parallel", "parallel"),
        name="mod_vectors",
    )(s_rows, w_mod, b_mod.reshape(depth, 1, n6))


def _seg_conv(x, w_ref, seq):
    nt = x.shape[0]
    row = lax.broadcasted_iota(jnp.int32, (nt, 1), 0)
    in_ctx = row >= seq
    pos = jnp.where(in_ctx, row - seq, row)
    seglen = jnp.where(in_ctx, nt - seq, seq)
    out = x * w_ref[2:3, :]
    for k, off in ((0, -2), (1, -1), (3, 1)):
        shifted = pltpu.roll(x, (-off) % nt, axis=0)
        valid = (pos + off >= 0) & (pos + off < seglen)
        out = out + jnp.where(valid, shifted, 0.0) * w_ref[k:k + 1, :]
    return out


def _epi_plain(acc, extra, row0, seq):
    return acc


def _epi_softplus(acc, extra, row0, seq):
    (b_ref,) = extra
    x = acc + b_ref[...]
    return jnp.maximum(x, 0.0) + jnp.log1p(jnp.exp(-jnp.abs(x)))


def _epi_conv_silu(acc, extra, row0, seq):
    w_ref, b_ref = extra
    return _silu(_seg_conv(acc, w_ref, seq) + b_ref[...])


def _epi_conv_bias(acc, extra, row0, seq):
    w_ref, b_ref = extra
    return _seg_conv(acc, w_ref, seq) + b_ref[...]


def _epi_gelu(acc, extra, row0, seq):
    return jax.nn.gelu(acc)


def _epi_qk(acc, extra, row0, seq, *, out_scale):
    g_ref, gain_ref, cos_ref, sin_ref = extra
    tn = acc.shape[1]
    ms = _dot((acc * acc).astype(BF16), g_ref[...])
    xn = acc * lax.rsqrt(ms + EPS) * gain_ref[...]
    lane = lax.broadcasted_iota(jnp.int32, (1, tn), 1)
    half = ATT_HEAD_DIM // 2
    second = (lane % ATT_HEAD_DIM) >= half
    partner = jnp.where(second, pltpu.roll(xn, half, axis=1), pltpu.roll(xn, tn - half, axis=1))
    reps = tn // LANE
    cos = jnp.concatenate([cos_ref[...]] * reps, axis=1) if reps > 1 else cos_ref[...]
    sin = jnp.concatenate([sin_ref[...]] * reps, axis=1) if reps > 1 else sin_ref[...]
    return (xn * cos + partner * sin) * out_scale


def _proj_kernel(*refs, epi, n_extra, k_shift, k_scale, seq):
    h_ref, modl_ref, modc_ref, g_ref, w_ref = refs[:5]
    extra = refs[5:5 + n_extra]
    o_ref = refs[5 + n_extra]
    u_scr = refs[6 + n_extra]
    tm = h_ref.shape[0]
    row0 = pl.program_id(1) * tm

    @pl.when(pl.program_id(2) == 0)
    def _():
        u_scr[...] = _modulated(h_ref[...], modl_ref, modc_ref, g_ref, k_shift, k_scale, row0, seq).astype(BF16)

    acc = _dot(u_scr[...], w_ref[...])
    o_ref[...] = epi(acc, extra, row0, seq).astype(o_ref.dtype)


def _proj(hh, modl, modc, g, w, *, epi, extra=(), extra_specs=(), tm, tn, out_dtype, seq, k_shift, k_scale,
          name):
    b, nt, d = hh.shape
    n = w.shape[1]
    grid = (b, nt // tm, n // tn)
    in_specs = [pl.BlockSpec((None, tm, d), lambda bb, i, j: (bb, i, 0)),
                pl.BlockSpec((None, 6, d), lambda bb, i, j: (bb, 0, 0)),
                pl.BlockSpec((None, 6, d), lambda bb, i, j: (0, 0, 0)),
                pl.BlockSpec((1, d), lambda bb, i, j: (0, 0)),
                pl.BlockSpec((d, tn), lambda bb, i, j: (0, j))] + list(extra_specs)
    kern = functools.partial(_proj_kernel, epi=epi, n_extra=len(extra), k_shift=k_shift, k_scale=k_scale, seq=seq)
    return pl.pallas_call(
        kern, grid=grid, in_specs=in_specs,
        out_specs=pl.BlockSpec((None, tm, tn), lambda bb, i, j: (bb, i, j)),
        out_shape=jax.ShapeDtypeStruct((b, nt, n), out_dtype),
        scratch_shapes=[pltpu.VMEM((tm, d), BF16)],
        compiler_params=_cparams("parallel", "parallel", "arbitrary"),
        name=name,
    )(hh, modl, modc, g, w, *extra)


def _col_spec(rows, tn):
    return pl.BlockSpec((rows, tn), lambda bb, i, j: (0, j))


def _pro_plain(ins):
    (a_ref,) = ins
    return a_ref[...]


def _pro_ssd(ins):
    yf_ref, yb_ref, z_ref, ng_ref = ins
    y = (yf_ref[...].astype(F32) + yb_ref[...].astype(F32)) * _silu(z_ref[...].astype(F32))
    ms = jnp.mean(y * y, axis=-1, keepdims=True)
    return (y * lax.rsqrt(ms + EPS) * ng_ref[...]).astype(BF16)


def _pro_lru(ins):
    hf_ref, hb_ref, gg_ref = ins
    hsum = (hf_ref[...].astype(F32) + hb_ref[...].astype(F32))
    return (hsum * gg_ref[...].astype(F32)).astype(BF16)


def _mmres_kernel(*refs, pro, n_in, k_gate, seq):
    ins = refs[:n_in]
    w_ref, h_ref, modl_ref, modc_ref, o_ref = refs[n_in:n_in + 5]
    tm = h_ref.shape[0]
    row0 = pl.program_id(1) * tm
    y = _dot(pro(ins), w_ref[...])
    gate = _gate_rows(modl_ref, modc_ref, k_gate, tm, row0, seq)
    o_ref[...] = h_ref[...] + gate * y


def _mmres(ins, in_widths, w, hh, modl, modc, *, pro, tm, k_gate, seq, name, rows=None, row_ins=()):
    b, nt, d = hh.shape
    rows = nt if rows is None else rows
    k = w.shape[0]
    grid = (b, rows // tm)
    in_specs = []
    for item, wd in zip(ins, in_widths):
        if isinstance(item, tuple):
            in_specs.append(pl.BlockSpec((None, None, tm, wd), functools.partial(lambda bb, i, s: (s, bb, i, 0), s=item[1])))
        else:
            in_specs.append(pl.BlockSpec((None, tm, wd), lambda bb, i: (bb, i, 0)))
    ins = [item[0] if isinstance(item, tuple) else item for item in ins]
    in_specs += [pl.BlockSpec((1, r.shape[1]), lambda bb, i: (0, 0)) for r in row_ins]
    in_specs += [pl.BlockSpec((k, d), lambda bb, i: (0, 0)),
                 pl.BlockSpec((None, tm, d), lambda bb, i: (bb, i, 0)),
                 pl.BlockSpec((None, 6, d), lambda bb, i: (bb, 0, 0)),
                 pl.BlockSpec((None, 6, d), lambda bb, i: (0, 0, 0))]
    kern = functools.partial(_mmres_kernel, pro=pro, n_in=len(ins) + len(row_ins), k_gate=k_gate, seq=seq)
    return pl.pallas_call(
        kern, grid=grid, in_specs=in_specs,
        out_specs=pl.BlockSpec((None, tm, d), lambda bb, i: (bb, i, 0)),
        out_shape=jax.ShapeDtypeStruct((b, rows, d), F32),
        compiler_params=_cparams("parallel", "parallel"),
        name=name,
    )(*ins, *row_ins, w, hh, modl, modc)


def _swiglu_tile(x, wg_ref, wu_ref, wd_ref):
    gate = _dot(x, wg_ref[...].astype(BF16))
    up = _dot(x, wu_ref[...].astype(BF16))
    return _dot((_silu(gate) * up).astype(BF16), wd_ref[...].astype(BF16))


def _ffn_kernel(h_ref, modl_ref, modc_ref, g_ref, wg_ref, wu_ref, wd_ref, o_ref, u_scr, acc_scr, *, seq):
    tm = h_ref.shape[0]
    row0 = pl.program_id(1) * tm
    j = pl.program_id(2)

    @pl.when(j == 0)
    def _():
        u_scr[...] = _modulated(h_ref[...], modl_ref, modc_ref, g_ref, 3, 4, row0, seq).astype(BF16)
        acc_scr[...] = jnp.zeros_like(acc_scr)

    acc_scr[...] += _swiglu_tile(u_scr[...], wg_ref, wu_ref, wd_ref)

    @pl.when(j == pl.num_programs(2) - 1)
    def _():
        gate = _gate_rows(modl_ref, modc_ref, 5, tm, row0, seq)
        o_ref[...] = h_ref[...] + gate * acc_scr[...]


def _dense_ffn(hh, modl, modc, g, wg, wu, wd, *, tm, th, seq):
    b, nt, d = hh.shape
    hid = wg.shape[1]
    grid = (b, nt // tm, hid // th)
    return pl.pallas_call(
        functools.partial(_ffn_kernel, seq=seq),
        grid=grid,
        in_specs=[pl.BlockSpec((None, tm, d), lambda bb, i, j: (bb, i, 0)),
                  pl.BlockSpec((None, 6, d), lambda bb, i, j: (bb, 0, 0)),
                  pl.BlockSpec((None, 6, d), lambda bb, i, j: (0, 0, 0)),
                  pl.BlockSpec((1, d), lambda bb, i, j: (0, 0)),
                  pl.BlockSpec((d, th), lambda bb, i, j: (0, j)),
                  pl.BlockSpec((d, th), lambda bb, i, j: (0, j)),
                  pl.BlockSpec((th, d), lambda bb, i, j: (j, 0))],
        out_specs=pl.BlockSpec((None, tm, d), lambda bb, i, j: (bb, i, 0)),
        out_shape=jax.ShapeDtypeStruct((b, nt, d), F32),
        scratch_shapes=[pltpu.VMEM((tm, d), BF16), pltpu.VMEM((tm, d), F32)],
        compiler_params=_cparams("parallel", "parallel", "arbitrary"),
        name="dense_ffn",
    )(hh, modl, modc, g, wg, wu, wd)


def _sel01(x_f32, e_bf16):
    x1, x2, x3 = _split3(x_f32)
    return _dot(x1, e_bf16) + _dot(x2, e_bf16) + _dot(x3, e_bf16)


def _ssd_chunk(xbc, dt, nega, dskip, expand, lane0, state_ref, o_ref, *, reverse, add_skip):
    q = xbc.shape[0]
    g_n = SSM_GROUPS * SSM_STATE
    inner = xbc.shape[1] - 2 * g_n
    heads = inner // SSM_HEAD_DIM
    rp = inner // SSM_GROUPS
    r_heads = heads // SSM_GROUPS

    rows = lax.broadcasted_iota(jnp.int32, (q, q), 0)
    cols = lax.broadcasted_iota(jnp.int32, (q, q), 1)
    keep = (rows <= cols) if reverse else (rows >= cols)
    tri = keep.astype(BF16)

    la = dt * nega
    ac = _dot_exact_lhs(tri, la)
    ac_t = jnp.transpose(ac)
    last = 0 if reverse else q - 1
    ac_end = ac[last:last + 1, :]

    x = xbc[:, :inner].astype(F32)
    decay_in = jnp.exp(ac)
    xdt = x * _dot(dt.astype(BF16), expand)
    to_end = _dot(jnp.exp(ac_end - ac).astype(BF16), expand)
    from_start = _dot(decay_in.astype(BF16), expand)
    total = jnp.broadcast_to(decay_in[last:last + 1, :], (2 * SUBLANES, LANE))
    chunk_decay = _sel01(total, expand)[0:1, :]
    xdt_b = xdt.astype(BF16)
    xend_b = (xdt * to_end).astype(BF16)

    for g in range(SSM_GROUPS):
        bg = xbc[:, inner + g * SSM_STATE: inner + (g + 1) * SSM_STATE]
        cg = xbc[:, inner + g_n + g * SSM_STATE: inner + g_n + (g + 1) * SSM_STATE]
        cb = _dot_nt(cg, bg).astype(BF16)
        pieces = []
        for r in range(r_heads):
            hd = g * r_heads + r
            ln = lane0 + hd
            diff = ac[:, ln:ln + 1] - ac_t[ln:ln + 1, :]
            dec = jnp.exp(jnp.minimum(diff, 0.0).astype(BF16))
            m = jnp.where(keep, cb * dec, jnp.zeros_like(dec))
            pieces.append(_dot(m, xdt_b[:, hd * SSM_HEAD_DIM:(hd + 1) * SSM_HEAD_DIM]))
        y_diag = jnp.concatenate(pieces, axis=1)
        sl = slice(g * rp, (g + 1) * rp)
        h_in = state_ref[g]
        y_off = _dot(cg, h_in.astype(BF16)) * from_start[:, sl]
        states = _dot_tn(bg, xend_b[:, sl])
        state_ref[g] = h_in * chunk_decay[:, sl] + states
        y = y_diag + y_off
        if add_skip:
            y = y + x[:, sl] * dskip[:, sl]
        o_ref[:, sl] = y.astype(o_ref.dtype)


def _ssd_kernel(xf_ref, xb_ref, dtf_ref, dtb_ref, nega_ref, dskip_ref, exp_ref, yf_ref, yb_ref, st_ref, *, heads):
    @pl.when(pl.program_id(1) == 0)
    def _():
        st_ref[...] = jnp.zeros_like(st_ref)

    _ssd_chunk(xf_ref[...], dtf_ref[...], nega_ref[...], dskip_ref[...], exp_ref[0], 0, st_ref.at[0], yf_ref,
               reverse=False, add_skip=True)
    _ssd_chunk(xb_ref[...], dtb_ref[...], nega_ref[...], dskip_ref[...], exp_ref[1], heads, st_ref.at[1], yb_ref,
               reverse=True, add_skip=False)


def _ssd(xbc, dt, nega, dskip, expand, *, seq, heads):
    b, nt, width = xbc.shape
    q = SSM_CHUNK
    inner = heads * SSM_HEAD_DIM
    nch = nt // q
    ncl = seq // q

    def fwd(bb, j):
        return (bb, (j + ncl) % nch, 0)

    def bwd(bb, j):
        return (bb, nch - 1 - j, 0)

    return pl.pallas_call(
        functools.partial(_ssd_kernel, heads=heads),
        grid=(b, nch),
        in_specs=[pl.BlockSpec((None, q, width), fwd),
                  pl.BlockSpec((None, q, width), bwd),
                  pl.BlockSpec((None, q, LANE), fwd),
                  pl.BlockSpec((None, q, LANE), bwd),
                  pl.BlockSpec((1, LANE), lambda bb, j: (0, 0)),
                  pl.BlockSpec((1, inner), lambda bb, j: (0, 0)),
                  pl.BlockSpec((2, LANE, inner), lambda bb, j: (0, 0, 0))],
        out_specs=[pl.BlockSpec((None, q, inner), fwd),
                   pl.BlockSpec((None, q, inner), bwd)],
        out_shape=[jax.ShapeDtypeStruct((b, nt, inner), BF16)] * 2,
        scratch_shapes=[pltpu.VMEM((2, SSM_GROUPS, SSM_STATE, inner // SSM_GROUPS), F32)],
        compiler_params=_cparams("parallel", "arbitrary"),
        name="ssd_scan",
    )(xbc, xbc, dt, dt, nega, dskip, expand)


def _attn_heads(q, k, v, o_ref):
    hd = ATT_HEAD_DIM
    n_q = q.shape[1] // hd
    rep = n_q // ATT_KV_HEADS
    tq = q.shape[0]
    for g in range(ATT_KV_HEADS):
        kg = k[:, g * hd:(g + 1) * hd]
        vg = v[:, g * hd:(g + 1) * hd]
        v_aug = jnp.concatenate([vg, jnp.ones_like(vg)], axis=1)
        qg = jnp.concatenate([q[:, (g * rep + r) * hd:(g * rep + r + 1) * hd] for r in range(rep)], axis=0)
        s_t = _dot_nt(kg, qg).astype(BF16)
        m = jnp.max(s_t, axis=0, keepdims=True)
        p_t = jnp.exp2(s_t - m)
        o_aug = _dot_tn(v_aug, p_t)
        o = jnp.transpose(o_aug[:hd, :] / o_aug[hd:hd + 1, :])
        og = jnp.concatenate([o[r * tq:(r + 1) * tq, :] for r in range(rep)], axis=1)
        o_ref[:, g * rep * hd:(g + 1) * rep * hd] = og.astype(o_ref.dtype)


def _attn_kernel(q_ref, k_ref, v_ref, o_ref, *, seq):
    tq = q_ref.shape[0]
    is_lat = pl.program_id(1) * tq < seq

    @pl.when(is_lat)
    def _():
        _attn_heads(q_ref[...], k_ref[...], v_ref[...], o_ref)

    @pl.when(jnp.logical_not(is_lat))
    def _():
        _attn_heads(q_ref[...], k_ref[seq:, :], v_ref[seq:, :], o_ref)


def _attention(q, k, v, *, seq, tq):
    b, nt, dq = q.shape
    dk = k.shape[2]
    return pl.pallas_call(
        functools.partial(_attn_kernel, seq=seq),
        grid=(b, nt // tq),
        in_specs=[pl.BlockSpec((None, tq, dq), lambda bb, i: (bb, i, 0)),
                  pl.BlockSpec((None, nt, dk), lambda bb, i: (bb, 0, 0)),
                  pl.BlockSpec((None, nt, dk), lambda bb, i: (bb, 0, 0))],
        out_specs=pl.BlockSpec((None, tq, dq), lambda bb, i: (bb, i, 0)),
        out_shape=jax.ShapeDtypeStruct((b, nt, dq), BF16),
        compiler_params=_cparams("parallel", "parallel"),
        name="gqa_attention",
    )(q, k, v)


def _pool_kernel(v_ref, vp_ref, vn_ref, h_ref, modl_ref, modc_ref, wg_ref, sc_ref, o_ref, *, seq, nt):
    tm = v_ref.shape[0]
    halo = vp_ref.shape[0]
    i = pl.program_id(1)
    row0 = i * tm
    v = v_ref[...]
    vext = jnp.concatenate([vp_ref[...], v, vn_ref[...]], axis=0)
    vhi, vlo = _split2(vext)
    t = row0 + lax.broadcasted_iota(jnp.int32, (tm, 1), 0)
    s = row0 - halo + lax.broadcasted_iota(jnp.int32, (1, tm + 2 * halo), 1)
    in_ctx = t >= seq
    seg_lo = jnp.where(in_ctx, seq, 0)
    seg_hi = jnp.where(in_ctx, nt, seq)
    grp = v.shape[1] // len(POOL_WINDOWS)
    outs = []
    for gi, win in enumerate(POOL_WINDOWS):
        lo = jnp.maximum(t - win // 2, seg_lo)
        hi = jnp.minimum(t + win // 2, seg_hi)
        band = ((s >= lo) & (s < hi)).astype(BF16)
        sl = slice(gi * grp, (gi + 1) * grp)
        summed = _dot(band, vhi[:, sl]) + _dot(band, vlo[:, sl])
        pooled = summed / (hi - lo).astype(F32) - v[:, sl]
        outs.append(_dot(pooled.astype(BF16), wg_ref[gi]))
    y = jnp.concatenate(outs, axis=1) * sc_ref[...]
    gate = _gate_rows(modl_ref, modc_ref, 2, tm, row0, seq)
    o_ref[...] = h_ref[...] + gate * y


def _pool(v, hh, modl, modc, w_grp, scale, *, seq, tm):
    b, nt, d = hh.shape
    halo = POOL_HALO
    per = tm // halo
    nhb = nt // halo
    grp = w_grp.shape[1]
    return pl.pallas_call(
        functools.partial(_pool_kernel, seq=seq, nt=nt),
        grid=(b, nt // tm),
        in_specs=[pl.BlockSpec((None, tm, d), lambda bb, i: (bb, i, 0)),
                  pl.BlockSpec((None, halo, d), lambda bb, i: (bb, jnp.maximum(i * per - 1, 0), 0)),
                  pl.BlockSpec((None, halo, d), lambda bb, i: (bb, jnp.minimum((i + 1) * per, nhb - 1), 0)),
                  pl.BlockSpec((None, tm, d), lambda bb, i: (bb, i, 0)),
                  pl.BlockSpec((None, 6, d), lambda bb, i: (bb, 0, 0)),
                  pl.BlockSpec((None, 6, d), lambda bb, i: (0, 0, 0)),
                  pl.BlockSpec((len(POOL_WINDOWS), grp, grp), lambda bb, i: (0, 0, 0)),
                  pl.BlockSpec((1, d), lambda bb, i: (0, 0))],
        out_specs=pl.BlockSpec((None, tm, d), lambda bb, i: (bb, i, 0)),
        out_shape=jax.ShapeDtypeStruct((b, nt, d), F32),
        compiler_params=_cparams("parallel", "parallel"),
        name="pool_mixer",
    )(v, v, v, hh, modl, modc, w_grp, scale)


def _lru_scan_chunk(a_scr, bx_scr, h_scr, carry_scr, *, reverse):
    tc = a_scr.shape[0]

    def step(tt, h):
        t = tc - 1 - tt if reverse else tt
        h = a_scr[pl.ds(t, 1), :] * h + bx_scr[pl.ds(t, 1), :]
        h_scr[pl.ds(t, 1), :] = h
        return h

    carry_scr[...] = lax.fori_loop(0, tc, step, carry_scr[...], unroll=8)


def _lru_kernel(x_ref, w_ref, b_ref, lam_ref, o_ref, a_scr, bx_scr, h_scr, carry_scr):
    d_dir = pl.program_id(1)
    tc, width = x_ref.shape
    bw = width // LRU_BLOCKS

    @pl.when(pl.program_id(2) == 0)
    def _():
        carry_scr[...] = jnp.zeros_like(carry_scr)

    xb = x_ref[...]
    for k in range(LRU_BLOCKS):
        sl = slice(k * bw, (k + 1) * bw)
        pre = _dot(xb[:, sl], w_ref[k])
        r_gate = _sigmoid(pre[:, :bw] + b_ref[0:1, sl])
        i_gate = _sigmoid(pre[:, bw:] + b_ref[1:2, sl])
        neg_lam = -lam_ref[:, sl]
        softplus = jnp.maximum(neg_lam, 0.0) + jnp.log1p(jnp.exp(-jnp.abs(neg_lam)))
        log_a = (-LRU_C) * r_gate * softplus
        a = jnp.exp(log_a)
        a_scr[:, sl] = a
        bx_scr[:, sl] = jnp.sqrt(-jnp.tanh(log_a) * (a * a + 1.0)) * (i_gate * xb[:, sl].astype(F32))

    @pl.when(d_dir == 0)
    def _():
        _lru_scan_chunk(a_scr, bx_scr, h_scr, carry_scr, reverse=False)

    @pl.when(d_dir == 1)
    def _():
        _lru_scan_chunk(a_scr, bx_scr, h_scr, carry_scr, reverse=True)

    o_ref[...] = h_scr[...].astype(o_ref.dtype)


def _lru(xr, w_cat, bias, lam, *, seq, tc):
    b, nt, width = xr.shape
    nc = nt // tc
    nl = seq // tc

    def chunk(d, j):
        return d * (nc - 1 - j) + (1 - d) * ((j + nl) % nc)

    return pl.pallas_call(
        _lru_kernel,
        grid=(b, 2, nc),
        in_specs=[pl.BlockSpec((None, tc, width), lambda bb, d, j: (bb, chunk(d, j), 0)),
                  pl.BlockSpec((None,) + w_cat.shape[1:], lambda bb, d, j: (d, 0, 0, 0)),
                  pl.BlockSpec((None, 2, width), lambda bb, d, j: (d, 0, 0)),
                  pl.BlockSpec((None, 1, width), lambda bb, d, j: (d, 0, 0))],
        out_specs=pl.BlockSpec((None, None, tc, width), lambda bb, d, j: (d, bb, chunk(d, j), 0)),
        out_shape=jax.ShapeDtypeStruct((2, b, nt, width), BF16),
        scratch_shapes=[pltpu.VMEM((tc, width), F32)] * 3 + [pltpu.VMEM((1, width), F32)],
        compiler_params=_cparams("parallel", "arbitrary", "arbitrary"),
        name="rglru_scan",
    )(xr, w_cat, bias, lam)


def _router_kernel(h_ref, modl_ref, modc_ref, g_ref, whi_ref, wlo_ref, u_ref, meta_ref, cnt_ref, carry_scr, *,
                   seq, n_exp):
    tm = h_ref.shape[0]
    row0 = pl.program_id(1) * tm
    first = (pl.program_id(0) == 0) & (pl.program_id(1) == 0)

    @pl.when(first)
    def _():
        carry_scr[...] = jnp.zeros_like(carry_scr)

    u = _modulated(h_ref[...], modl_ref, modc_ref, g_ref, 3, 4, row0, seq)
    u_ref[...] = u
    uhi, ulo = _split2(u)
    logits = _dot(uhi, whi_ref[...]) + _dot(ulo, whi_ref[...]) + _dot(uhi, wlo_ref[...])
    lane = lax.broadcasted_iota(jnp.int32, (tm, LANE), 1)
    neg = jnp.float32(-jnp.inf)
    logits = jnp.where(lane < n_exp, logits, neg)
    v0 = jnp.max(logits, axis=-1, keepdims=True)
    lane_f = lane.astype(F32)
    i0 = jnp.min(jnp.where(logits == v0, lane_f, float(LANE)), axis=-1, keepdims=True)
    rest = jnp.where(lane_f == i0, neg, logits)
    v1 = jnp.max(rest, axis=-1, keepdims=True)
    i1 = jnp.min(jnp.where(rest == v1, lane_f, float(LANE)), axis=-1, keepdims=True)
    g0 = 1.0 / (1.0 + jnp.exp(v1 - v0))
    g1 = 1.0 - g0
    sel0 = lane_f == i0
    sel1 = lane_f == i1
    onehot = (sel0 | sel1).astype(BF16)
    rr = lax.broadcasted_iota(jnp.int32, (tm, tm), 0)
    cc = lax.broadcasted_iota(jnp.int32, (tm, tm), 1)
    before = (cc < rr).astype(BF16)
    prefix = _dot(before, onehot) + carry_scr[...]
    r0 = jnp.sum(jnp.where(sel0, prefix, 0.0), axis=-1, keepdims=True)
    r1 = jnp.sum(jnp.where(sel1, prefix, 0.0), axis=-1, keepdims=True)
    carry_scr[...] += jnp.sum(onehot.astype(F32), axis=0, keepdims=True)
    cnt_ref[...] = carry_scr[...]
    meta = jnp.where(lane == 0, i0, 0.0)
    meta = jnp.where(lane == 1, i1, meta)
    meta = jnp.where(lane == 2, g0, meta)
    meta = jnp.where(lane == 3, g1, meta)
    meta = jnp.where(lane == 4, r0, meta)
    meta = jnp.where(lane == 5, r1, meta)
    meta_ref[...] = meta


def _router(hh, modl, modc, g, whi, wlo, *, rows, tm, seq, n_exp):
    b, nt, d = hh.shape
    return pl.pallas_call(
        functools.partial(_router_kernel, seq=seq, n_exp=n_exp),
        grid=(b, rows // tm),
        in_specs=[pl.BlockSpec((None, tm, d), lambda bb, i: (bb, i, 0)),
                  pl.BlockSpec((None, 6, d), lambda bb, i: (bb, 0, 0)),
                  pl.BlockSpec((None, 6, d), lambda bb, i: (0, 0, 0)),
                  pl.BlockSpec((1, d), lambda bb, i: (0, 0)),
                  pl.BlockSpec((d, LANE), lambda bb, i: (0, 0)),
                  pl.BlockSpec((d, LANE), lambda bb, i: (0, 0))],
        out_specs=[pl.BlockSpec((None, tm, d), lambda bb, i: (bb, i, 0)),
                   pl.BlockSpec((None, tm, LANE), lambda bb, i: (bb, i, 0)),
                   pl.BlockSpec((1, LANE), lambda bb, i: (0, 0))],
        out_shape=[jax.ShapeDtypeStruct((b, rows, d), F32),
                   jax.ShapeDtypeStruct((b, rows, LANE), F32),
                   jax.ShapeDtypeStruct((1, LANE), F32)],
        scratch_shapes=[pltpu.VMEM((1, LANE), F32)],
        compiler_params=_cparams("arbitrary", "arbitrary"),
        name="moe_router",
    )(hh, modl, modc, g, whi, wlo)


def _dispatch_kernel(dest_ref, fill_ref, u_ref, xg_hbm, zero_scr, sem, *, tm, rows_per_b):
    base = (pl.program_id(0) * rows_per_b + pl.program_id(1) * tm) * TOP_K

    @pl.when((pl.program_id(0) == 0) & (pl.program_id(1) == 0))
    def _():
        zero_scr[...] = jnp.zeros_like(zero_scr)

        def fill(blk, c):
            @pl.when(fill_ref[blk] != 0)
            def _():
                row0 = pl.multiple_of(blk * MOE_BLOCK, MOE_BLOCK)
                cp = pltpu.make_async_copy(zero_scr, xg_hbm.at[pl.ds(row0, MOE_BLOCK)], sem)
                cp.start()
                cp.wait()
            return c

        lax.fori_loop(0, fill_ref.shape[0], fill, 0)

    def issue(r, c):
        for k in range(TOP_K):
            dst = xg_hbm.at[pl.ds(dest_ref[base + r * TOP_K + k], 1)]
            pltpu.make_async_copy(u_ref.at[pl.ds(r, 1)], dst, sem).start()
        return c

    lax.fori_loop(0, tm, issue, 0, unroll=8)
    for k in range(TOP_K):
        pltpu.make_async_copy(u_ref, xg_hbm.at[pl.ds(0, tm)], sem).wait()


def _dispatch(dest, fill, u, *, tm):
    b, rows, d = u.shape
    grid_spec = pltpu.PrefetchScalarGridSpec(
        num_scalar_prefetch=2, grid=(b, rows // tm),
        in_specs=[pl.BlockSpec((tm, d), lambda bb, i, ds, fl: (bb * (rows // tm) + i, 0))],
        out_specs=pl.BlockSpec(memory_space=pl.ANY),
        scratch_shapes=[pltpu.VMEM((MOE_BLOCK, d), F32), pltpu.SemaphoreType.DMA(())])
    return pl.pallas_call(
        functools.partial(_dispatch_kernel, tm=tm, rows_per_b=rows),
        grid_spec=grid_spec,
        out_shape=jax.ShapeDtypeStruct((fill.shape[0] * MOE_BLOCK, d), F32),
        compiler_params=_cparams("arbitrary", "arbitrary"),
        name="moe_dispatch",
    )(dest, fill, u.reshape(b * rows, d))


def _expert_kernel(be_ref, nu_ref, x_ref, wg_ref, wu_ref, wd_ref, y_ref, xb_scr):
    blk = pl.program_id(0)
    j = pl.program_id(1)
    used = blk < nu_ref[0]

    @pl.when(j == 0)
    def _():
        y_ref[...] = jnp.zeros_like(y_ref)

    @pl.when(used & (j == 0))
    def _():
        xb_scr[...] = x_ref[...].astype(BF16)

    @pl.when(used)
    def _():
        y_ref[...] += _swiglu_tile(xb_scr[...], wg_ref, wu_ref, wd_ref)


def _experts(block_expert, n_used, xg, wg, wu, wd, *, th, layer):
    p, d = xg.shape
    hid = wg.shape[3]
    nb = p // MOE_BLOCK
    nj = hid // th

    def blk_eff(blk, nu):
        return jnp.minimum(blk, nu[0] - 1)

    def j_eff(blk, j, nu):
        return jnp.where(blk < nu[0], j, nj - 1)

    def w_in_map(blk, j, be, nu):
        return (layer, be[blk_eff(blk, nu)], 0, j_eff(blk, j, nu))

    def w_out_map(blk, j, be, nu):
        return (layer, be[blk_eff(blk, nu)], j_eff(blk, j, nu), 0)

    grid_spec = pltpu.PrefetchScalarGridSpec(
        num_scalar_prefetch=2, grid=(nb, nj),
        in_specs=[pl.BlockSpec((MOE_BLOCK, d), lambda blk, j, be, nu: (blk_eff(blk, nu), 0)),
                  pl.BlockSpec((None, None, d, th), w_in_map),
                  pl.BlockSpec((None, None, d, th), w_in_map),
                  pl.BlockSpec((None, None, th, d), w_out_map)],
        out_specs=pl.BlockSpec((MOE_BLOCK, d), lambda blk, j, be, nu: (blk, 0)),
        scratch_shapes=[pltpu.VMEM((MOE_BLOCK, d), BF16)])
    return pl.pallas_call(
        _expert_kernel, grid_spec=grid_spec,
        out_shape=jax.ShapeDtypeStruct((p, d), F32),
        compiler_params=_cparams("arbitrary", "arbitrary"),
        name="moe_experts",
    )(block_expert, n_used, xg, wg, wu, wd)


def _combine_kernel(dest_ref, yg_hbm, meta_ref, h_ref, modl_ref, modc_ref, o_ref, y0_scr, y1_scr, sem, *, tm,
                    rows_per_b, seq):
    bb = pl.program_id(0)
    i = pl.program_id(1)
    base = (bb * rows_per_b + i * tm) * TOP_K
    bufs = (y0_scr, y1_scr)

    def issue(r, c):
        for k in range(TOP_K):
            src = yg_hbm.at[pl.ds(dest_ref[base + r * TOP_K + k], 1)]
            pltpu.make_async_copy(src, bufs[k].at[pl.ds(r, 1)], sem).start()
        return c

    lax.fori_loop(0, tm, issue, 0, unroll=8)
    for k in range(TOP_K):
        pltpu.make_async_copy(yg_hbm.at[pl.ds(0, tm)], bufs[k], sem).wait()
    meta = meta_ref[...]
    y = meta[:, 2:3] * y0_scr[...] + meta[:, 3:4] * y1_scr[...]
    gate = _gate_rows(modl_ref, modc_ref, 5, tm, i * tm, seq)
    o_ref[...] = h_ref[...] + gate * y


def _combine(dest, yg, meta, hh, modl, modc, *, tm, seq):
    b, rows, _ = meta.shape
    d = hh.shape[2]
    grid_spec = pltpu.PrefetchScalarGridSpec(
        num_scalar_prefetch=1, grid=(b, rows // tm),
        in_specs=[pl.BlockSpec(memory_space=pl.ANY),
                  pl.BlockSpec((None, tm, LANE), lambda bb, i, ds: (bb, i, 0)),
                  pl.BlockSpec((None, tm, d), lambda bb, i, ds: (bb, i, 0)),
                  pl.BlockSpec((None, 6, d), lambda bb, i, ds: (bb, 0, 0)),
                  pl.BlockSpec((None, 6, d), lambda bb, i, ds: (0, 0, 0))],
        out_specs=pl.BlockSpec((None, tm, d), lambda bb, i, ds: (bb, i, 0)),
        scratch_shapes=[pltpu.VMEM((tm, d), F32), pltpu.VMEM((tm, d), F32), pltpu.SemaphoreType.DMA(())])
    return pl.pallas_call(
        functools.partial(_combine_kernel, tm=tm, rows_per_b=rows, seq=seq),
        grid_spec=grid_spec,
        out_shape=jax.ShapeDtypeStruct((b, rows, d), F32),
        compiler_params=_cparams("arbitrary", "arbitrary"),
        name="moe_combine",
    )(dest, yg, meta, hh, modl, modc)


def _moe(hh, modl, modc, g, w_router, wg, wu, wd, *, rows, seq, layer):
    b, nt, d = hh.shape
    n_exp = w_router.shape[1]
    wr = jnp.pad(w_router, ((0, 0), (0, LANE - n_exp)))
    whi, wlo = _split2(wr)
    tm = _pick(rows, (768, 512, 256))
    u, meta, counts = _router(hh, modl, modc, g, whi, wlo, rows=rows, tm=tm, seq=seq, n_exp=n_exp)

    cnt = counts[0, :n_exp].astype(jnp.int32)
    padded = (cnt + MOE_BLOCK - 1) // MOE_BLOCK * MOE_BLOCK
    pend = jnp.cumsum(padded)
    pstart = pend - padded
    n_assign = b * rows * TOP_K
    nb = -(-(n_assign + n_exp * (MOE_BLOCK - 1)) // MOE_BLOCK)
    idx = meta[:, :, 0:TOP_K].astype(jnp.int32)
    rank = meta[:, :, 4:4 + TOP_K].astype(jnp.int32)
    dest = (sum(jnp.where(idx == e, pstart[e], 0) for e in range(n_exp)) + rank).reshape(-1)
    blk_start = jnp.arange(nb, dtype=jnp.int32) * MOE_BLOCK
    block_expert = jnp.minimum(jnp.sum(pend[None, :] <= blk_start[:, None], axis=1), n_exp - 1).astype(jnp.int32)
    n_used = (pend[-1:] // MOE_BLOCK).astype(jnp.int32)

    blk_id = jnp.arange(nb, dtype=jnp.int32)
    holds_padding = jnp.any((pend[None, :] // MOE_BLOCK - 1 == blk_id[:, None]) & (padded[None, :] > 0), axis=1)
    fill = (holds_padding | (blk_id >= n_used[0])).astype(jnp.int32)
    td = tm
    xg = _dispatch(dest, fill, u, tm=td)
    yg = _experts(block_expert, n_used, xg, wg, wu, wd, th=_pick(wg.shape[3], (512, 256, 128)), layer=layer)
    return _combine(dest, yg, meta, hh, modl, modc, tm=td, seq=seq)


def _rope_tables(seq, nt):
    rows = seq // GRID_W
    row = jnp.repeat(jnp.arange(rows), GRID_W).astype(F32)
    col = jnp.tile(jnp.arange(GRID_W), rows).astype(F32)
    n_freq = ATT_HEAD_DIM // 4
    inv_freq = ROPE_THETA ** (-jnp.arange(n_freq, dtype=F32) / n_freq)
    ang = jnp.concatenate([row[:, None] * inv_freq, col[:, None] * inv_freq], axis=-1)
    cos, sin = jnp.cos(ang), jnp.sin(ang)
    cos_h = jnp.concatenate([cos, cos], axis=-1)
    sin_h = jnp.concatenate([-sin, sin], axis=-1)
    reps = LANE // ATT_HEAD_DIM
    cos_t = jnp.tile(cos_h, (1, reps))
    sin_t = jnp.tile(sin_h, (1, reps))
    pad = nt - seq
    cos_t = jnp.concatenate([cos_t, jnp.ones((pad, LANE), F32)], axis=0)
    sin_t = jnp.concatenate([sin_t, jnp.zeros((pad, LANE), F32)], axis=0)
    return cos_t, sin_t


def _head_mean_matrix(n):
    r = jnp.arange(n) // ATT_HEAD_DIM
    return (r[:, None] == r[None, :]).astype(F32).astype(BF16) * jnp.asarray(1.0 / ATT_HEAD_DIM, BF16)


def kernel(x, c, ctx, c_ctx, w_mod, b_mod, norm_g, ssm_w_in, ssm_conv_w, ssm_conv_b, ssm_dt_bias, ssm_a_log, ssm_d, ssm_norm_g, ssm_w_out, att_w_qkv, att_q_norm, att_k_norm, att_w_out, pool_w_in, pool_w_grp, pool_scale, lru_w_in, lru_conv_w, lru_conv_b, lru_gate_w, lru_gate_b, lru_lambda, lru_w_out, ffn_w_gate, ffn_w_up, ffn_w_down, moe_w_router, moe_w_gate, moe_w_up, moe_w_down):
    b, seq, d = x.shape
    nctx = ctx.shape[1]
    nt = seq + nctx
    depth = w_mod.shape[0]
    n_mixers = 4

    bp = -(-(b + 1) // 8) * 8
    s_rows = jnp.concatenate([c, c_ctx[None, :], jnp.zeros((bp - b - 1, d), F32)], axis=0)
    mods = _mod_vectors(s_rows, w_mod, b_mod)
    modl_all = mods[:, :b].reshape(depth, b, 6, d)
    modc_all = mods[:, b:b + 1].reshape(depth, 1, 6, d)

    hh = jnp.concatenate([x, ctx], axis=1)
    tm_row = _pick(nt, (768, 512, 256))
    cos_t, sin_t = _rope_tables(seq, nt)
    out = None

    for i in range(depth):
        last = i == depth - 1
        kind, j = i % n_mixers, i // n_mixers
        modl, modc = modl_all[i], modc_all[i]
        g1 = norm_g[i, 0][None, :]
        g2 = norm_g[i, 1][None, :]
        proj = functools.partial(_proj, hh, modl, modc, g1, seq=seq, k_shift=0, k_scale=1)

        if kind == 0:
            inner = ssm_d.shape[1] * SSM_HEAD_DIM
            heads = ssm_d.shape[1]
            conv_dim = ssm_conv_w.shape[2]
            w_in = ssm_w_in[j].astype(BF16)
            w_z = w_in[:, :inner]
            w_xbc = w_in[:, inner:inner + conv_dim]
            w_dt = jnp.pad(w_in[:, inner + conv_dim:], ((0, 0), (0, LANE - 2 * heads)))
            dt_bias = jnp.pad(ssm_dt_bias[j].reshape(1, 2 * heads), ((0, 0), (0, LANE - 2 * heads)))
            z = proj(w_z, epi=_epi_plain, tm=tm_row, tn=inner, out_dtype=BF16, name="ssd_proj_z")
            xbc = proj(w_xbc, epi=_epi_conv_silu, extra=(ssm_conv_w[j], ssm_conv_b[j][None, :]),
                       extra_specs=(_col_spec(ssm_conv_w.shape[1], 512), _col_spec(1, 512)),
                       tm=nt, tn=512, out_dtype=BF16, name="ssd_proj_xbc")
            dt = proj(w_dt, epi=_epi_softplus, extra=(dt_bias,), extra_specs=(_col_spec(1, LANE),),
                      tm=tm_row, tn=LANE, out_dtype=F32, name="ssd_proj_dt")
            nega = jnp.pad(-jnp.exp(ssm_a_log[j].astype(F32)).reshape(1, 2 * heads), ((0, 0), (0, LANE - 2 * heads)))
            head_of = jnp.arange(inner)[None, :] // SSM_HEAD_DIM
            lanes = jnp.arange(LANE)[:, None]
            expand = jnp.stack([(lanes == head_of + dd * heads) for dd in range(2)]).astype(F32).astype(BF16)
            dskip = jnp.repeat(ssm_d[j].astype(F32), SSM_HEAD_DIM)[None, :]
            yf, yb = _ssd(xbc, dt, nega, dskip, expand, seq=seq, heads=heads)
            hh = _mmres((yf, yb, z), (inner, inner, inner), ssm_w_out[j].astype(BF16), hh, modl, modc,
                        pro=_pro_ssd, tm=tm_row, k_gate=2, seq=seq, name="ssd_out",
                        row_ins=(ssm_norm_g[j][None, :],))
        elif kind == 1:
            nq = att_w_out.shape[1]
            nk = ATT_KV_HEADS * ATT_HEAD_DIM
            w_qkv = att_w_qkv[j].astype(BF16)
            tab_specs = (pl.BlockSpec((tm_row, LANE), lambda bb, ii, jj: (ii, 0)),) * 2

            def qk_extra(n, gain):
                return ((_head_mean_matrix(n), jnp.tile(gain, n // ATT_HEAD_DIM)[None, :], cos_t, sin_t),
                        (pl.BlockSpec((n, n), lambda bb, ii, jj: (0, 0)),
                         pl.BlockSpec((1, n), lambda bb, ii, jj: (0, 0))) + tab_specs)

            ex, sp = qk_extra(nq, att_q_norm[j])
            q = proj(w_qkv[:, :nq], epi=functools.partial(_epi_qk, out_scale=ATT_HEAD_DIM ** -0.5 * math.log2(math.e)),
                     extra=ex,
                     extra_specs=sp, tm=tm_row, tn=nq, out_dtype=BF16, name="att_proj_q")
            ex, sp = qk_extra(nk, att_k_norm[j])
            k = proj(w_qkv[:, nq:nq + nk], epi=functools.partial(_epi_qk, out_scale=1.0), extra=ex,
                     extra_specs=sp, tm=tm_row, tn=nk, out_dtype=BF16, name="att_proj_k")
            v = proj(w_qkv[:, nq + nk:], epi=_epi_plain, tm=tm_row, tn=nk, out_dtype=BF16, name="att_proj_v")
            o = _attention(q, k, v, seq=seq, tq=_pick(nctx, (128,)))
            hh = _mmres((o,), (nq,), att_w_out[j].astype(BF16), hh, modl, modc, pro=_pro_plain, tm=tm_row,
                        k_gate=2, seq=seq, name="att_out")
        elif kind == 2:
            v = proj(pool_w_in[j].astype(BF16), epi=_epi_plain, tm=tm_row, tn=d, out_dtype=F32,
                     name="pool_proj")
            hh = _pool(v, hh, modl, modc, pool_w_grp[j].astype(BF16), pool_scale[j][None, :], seq=seq,
                       tm=_pick(nctx, (256, 128)))
        else:
            width = lru_w_out.shape[1]
            w_in = lru_w_in[j].astype(BF16)
            gg = proj(w_in[:, :width], epi=_epi_gelu, tm=tm_row, tn=width, out_dtype=BF16, name="lru_proj_gate")
            tn_x = _pick(width, (640, 256, 128))
            xr = proj(w_in[:, width:], epi=_epi_conv_bias, extra=(lru_conv_w[j], lru_conv_b[j][None, :]),
                      extra_specs=(_col_spec(lru_conv_w.shape[1], tn_x), _col_spec(1, tn_x)),
                      tm=nt, tn=tn_x, out_dtype=BF16, name="lru_proj_x")
            gw = lru_gate_w[j]
            w_cat = jnp.concatenate([gw[:, 0], gw[:, 1]], axis=-1).astype(BF16)
            hs = _lru(xr, w_cat, lru_gate_b[j], lru_lambda[j][:, None, :], seq=seq, tc=_pick(nctx, (256, 128)))
            rows = seq if last else nt
            hh = _mmres(((hs, 0), (hs, 1), gg), (width, width, width), lru_w_out[j].astype(BF16), hh, modl, modc,
                        pro=_pro_lru, tm=_pick(rows, (1024, 768, 512, 256)), k_gate=2, seq=seq, name="lru_out",
                        rows=rows)

        kf = i // 2
        if i % 2 == 0:
            hh = _dense_ffn(hh, modl, modc, g2, ffn_w_gate[kf].astype(BF16), ffn_w_up[kf].astype(BF16),
                            ffn_w_down[kf].astype(BF16), tm=tm_row,
                            th=_pick(ffn_w_gate.shape[2], (512, 256, 128)), seq=seq)
        else:
            rows = seq if last else nt
            hh = _moe(hh, modl, modc, g2, moe_w_router[kf], moe_w_gate, moe_w_up, moe_w_down, rows=rows, seq=seq,
                      layer=kf)
        if last:
            out = hh[:, :seq] if hh.shape[1] != seq else hh
    return out
```

```python
import functools
import math

import jax
import jax.numpy as jnp
from jax import lax
from jax.experimental import pallas as pl
from jax.experimental.pallas import tpu as pltpu

F32 = jnp.float32
BF16 = jnp.bfloat16
EPS = 1e-6

GRID_W = 64
SSM_HEAD_DIM = 64
SSM_GROUPS = 4
SSM_STATE = 128
SSM_CHUNK = 128
ATT_HEAD_DIM = 64
ATT_KV_HEADS = 4
ROPE_THETA = 10000.0
POOL_WINDOWS = (2, 4, 8, 16)
LRU_BLOCKS = 10
LRU_C = 8.0
TOP_K = 2
MOE_BLOCK = 1024

LANE = 128
SUBLANES = 8
POOL_HALO = 64
VMEM_LIMIT = 56 * 1024 * 1024


def _cparams(*sem):
    return pltpu.CompilerParams(dimension_semantics=sem, vmem_limit_bytes=VMEM_LIMIT)


def _pick(n, candidates):
    for c in candidates:
        if n % c == 0:
            return c
    raise ValueError(f"no tile in {candidates} divides {n}")


def _sigmoid(x):
    return 0.5 * jnp.tanh(0.5 * x) + 0.5


def _silu(x):
    return x * _sigmoid(x)


def _split2(x):
    hi = x.astype(BF16)
    lo = (x - hi.astype(F32)).astype(BF16)
    return hi, lo


def _split3(x):
    x1 = x.astype(BF16)
    r = x - x1.astype(F32)
    x2 = r.astype(BF16)
    x3 = (r - x2.astype(F32)).astype(BF16)
    return x1, x2, x3


def _dot(a, b):
    return jnp.dot(a, b, preferred_element_type=F32)


def _dot_nt(a, b):
    return lax.dot_general(a, b, (((1,), (1,)), ((), ())), preferred_element_type=F32)


def _dot_tn(a, b):
    return lax.dot_general(a, b, (((0,), (0,)), ((), ())), preferred_element_type=F32)


def _dot_exact_lhs(a_bf16, x_f32):
    x1, x2, x3 = _split3(x_f32)
    return _dot(a_bf16, x1) + _dot(a_bf16, x2) + _dot(a_bf16, x3)


def _modulated(h, modl_ref, modc_ref, g_ref, k_shift, k_scale, row0, seq):
    tm = h.shape[0]
    row = row0 + lax.broadcasted_iota(jnp.int32, (tm, 1), 0)
    is_ctx = row >= seq
    shift = jnp.where(is_ctx, modc_ref[k_shift:k_shift + 1, :], modl_ref[k_shift:k_shift + 1, :])
    scale = jnp.where(is_ctx, modc_ref[k_scale:k_scale + 1, :], modl_ref[k_scale:k_scale + 1, :])
    ms = jnp.mean(h * h, axis=-1, keepdims=True)
    y = h * lax.rsqrt(ms + EPS) * g_ref[...]
    return y * (1.0 + scale) + shift


def _gate_rows(modl_ref, modc_ref, k_gate, tm, row0, seq):
    row = row0 + lax.broadcasted_iota(jnp.int32, (tm, 1), 0)
    return jnp.where(row >= seq, modc_ref[k_gate:k_gate + 1, :], modl_ref[k_gate:k_gate + 1, :])


def _mod_kernel(s_ref, w_ref, b_ref, o_ref):
    s = _silu(s_ref[...])
    o_ref[...] = _dot(s.astype(BF16), w_ref[...].astype(BF16)) + b_ref[...]


def _mod_vectors(s_rows, w_mod, b_mod):
    depth, d, n6 = w_mod.shape
    bp = s_rows.shape[0]
    tn = _pick(n6, (1536, 1024, 512, 256, 128))
    return pl.pallas_call(
        _mod_kernel,
        grid=(depth, n6 // tn),
        in_specs=[pl.BlockSpec((bp, d), lambda l, j: (0, 0)),
                  pl.BlockSpec((None, d, tn), lambda l, j: (l, 0, j)),
                  pl.BlockSpec((None, 1, tn), lambda l, j: (l, 0, j))],
        out_specs=pl.BlockSpec((None, bp, tn), lambda l, j: (l, 0, j)),
        out_shape=jax.ShapeDtypeStruct((depth, bp, n6), F32),
        compiler_params=_cparams("parallel", "parallel"),
        name="mod_vectors",
    )(s_rows, w_mod, b_mod.reshape(depth, 1, n6))


def _seg_conv(x, w_ref, seq):
    nt = x.shape[0]
    row = lax.broadcasted_iota(jnp.int32, (nt, 1), 0)
    in_ctx = row >= seq
    pos = jnp.where(in_ctx, row - seq, row)
    seglen = jnp.where(in_ctx, nt - seq, seq)
    out = x * w_ref[2:3, :]
    for k, off in ((0, -2), (1, -1), (3, 1)):
        shifted = pltpu.roll(x, (-off) % nt, axis=0)
        valid = (pos + off >= 0) & (pos + off < seglen)
        out = out + jnp.where(valid, shifted, 0.0) * w_ref[k:k + 1, :]
    return out


def _epi_plain(acc, extra, row0, seq):
    return acc


def _epi_softplus(acc, extra, row0, seq):
    (b_ref,) = extra
    x = acc + b_ref[...]
    return jnp.maximum(x, 0.0) + jnp.log1p(jnp.exp(-jnp.abs(x)))


def _epi_conv_silu(acc, extra, row0, seq):
    w_ref, b_ref = extra
    return _silu(_seg_conv(acc, w_ref, seq) + b_ref[...])


def _epi_conv_bias(acc, extra, row0, seq):
    w_ref, b_ref = extra
    return _seg_conv(acc, w_ref, seq) + b_ref[...]


def _epi_gelu(acc, extra, row0, seq):
    return jax.nn.gelu(acc)


def _epi_qk(acc, extra, row0, seq, *, out_scale):
    g_ref, gain_ref, cos_ref, sin_ref = extra
    tn = acc.shape[1]
    ms = _dot((acc * acc).astype(BF16), g_ref[...])
    xn = acc * lax.rsqrt(ms + EPS) * gain_ref[...]
    lane = lax.broadcasted_iota(jnp.int32, (1, tn), 1)
    half = ATT_HEAD_DIM // 2
    second = (lane % ATT_HEAD_DIM) >= half
    partner = jnp.where(second, pltpu.roll(xn, half, axis=1), pltpu.roll(xn, tn - half, axis=1))
    reps = tn // LANE
    cos = jnp.concatenate([cos_ref[...]] * reps, axis=1) if reps > 1 else cos_ref[...]
    sin = jnp.concatenate([sin_ref[...]] * reps, axis=1) if reps > 1 else sin_ref[...]
    return (xn * cos + partner * sin) * out_scale


def _proj_kernel(*refs, epi, n_extra, k_shift, k_scale, seq):
    h_ref, modl_ref, modc_ref, g_ref, w_ref = refs[:5]
    extra = refs[5:5 + n_extra]
    o_ref = refs[5 + n_extra]
    u_scr = refs[6 + n_extra]
    tm = h_ref.shape[0]
    row0 = pl.program_id(1) * tm

    @pl.when(pl.program_id(2) == 0)
    def _():
        u_scr[...] = _modulated(h_ref[...], modl_ref, modc_ref, g_ref, k_shift, k_scale, row0, seq).astype(BF16)

    acc = _dot(u_scr[...], w_ref[...])
    o_ref[...] = epi(acc, extra, row0, seq).astype(o_ref.dtype)


def _proj(hh, modl, modc, g, w, *, epi, extra=(), extra_specs=(), tm, tn, out_dtype, seq, k_shift, k_scale,
          name):
    b, nt, d = hh.shape
    n = w.shape[1]
    grid = (b, nt // tm, n // tn)
    in_specs = [pl.BlockSpec((None, tm, d), lambda bb, i, j: (bb, i, 0)),
                pl.BlockSpec((None, 6, d), lambda bb, i, j: (bb, 0, 0)),
                pl.BlockSpec((None, 6, d), lambda bb, i, j: (0, 0, 0)),
                pl.BlockSpec((1, d), lambda bb, i, j: (0, 0)),
                pl.BlockSpec((d, tn), lambda bb, i, j: (0, j))] + list(extra_specs)
    kern = functools.partial(_proj_kernel, epi=epi, n_extra=len(extra), k_shift=k_shift, k_scale=k_scale, seq=seq)
    return pl.pallas_call(
        kern, grid=grid, in_specs=in_specs,
        out_specs=pl.BlockSpec((None, tm, tn), lambda bb, i, j: (bb, i, j)),
        out_shape=jax.ShapeDtypeStruct((b, nt, n), out_dtype),
        scratch_shapes=[pltpu.VMEM((tm, d), BF16)],
        compiler_params=_cparams("parallel", "parallel", "arbitrary"),
        name=name,
    )(hh, modl, modc, g, w, *extra)


def _col_spec(rows, tn):
    return pl.BlockSpec((rows, tn), lambda bb, i, j: (0, j))


def _pro_plain(ins):
    (a_ref,) = ins
    return a_ref[...]


def _pro_ssd(ins):
    yf_ref, yb_ref, z_ref, ng_ref = ins
    y = (yf_ref[...].astype(F32) + yb_ref[...].astype(F32)) * _silu(z_ref[...].astype(F32))
    ms = jnp.mean(y * y, axis=-1, keepdims=True)
    return (y * lax.rsqrt(ms + EPS) * ng_ref[...]).astype(BF16)


def _pro_lru(ins):
    hf_ref, hb_ref, gg_ref = ins
    hsum = (hf_ref[...].astype(F32) + hb_ref[...].astype(F32))
    return (hsum * gg_ref[...].astype(F32)).astype(BF16)


def _mmres_kernel(*refs, pro, n_in, k_gate, seq):
    ins = refs[:n_in]
    w_ref, h_ref, modl_ref, modc_ref, o_ref = refs[n_in:n_in + 5]
    tm = h_ref.shape[0]
    row0 = pl.program_id(1) * tm
    y = _dot(pro(ins), w_ref[...])
    gate = _gate_rows(modl_ref, modc_ref, k_gate, tm, row0, seq)
    o_ref[...] = h_ref[...] + gate * y


def _mmres(ins, in_widths, w, hh, modl, modc, *, pro, tm, k_gate, seq, name, rows=None, row_ins=()):
    b, nt, d = hh.shape
    rows = nt if rows is None else rows
    k = w.shape[0]
    grid = (b, rows // tm)
    in_specs = []
    for item, wd in zip(ins, in_widths):
        if isinstance(item, tuple):
            in_specs.append(pl.BlockSpec((None, None, tm, wd), functools.partial(lambda bb, i, s: (s, bb, i, 0), s=item[1])))
        else:
            in_specs.append(pl.BlockSpec((None, tm, wd), lambda bb, i: (bb, i, 0)))
    ins = [item[0] if isinstance(item, tuple) else item for item in ins]
    in_specs += [pl.BlockSpec((1, r.shape[1]), lambda bb, i: (0, 0)) for r in row_ins]
    in_specs += [pl.BlockSpec((k, d), lambda bb, i: (0, 0)),
                 pl.BlockSpec((None, tm, d), lambda bb, i: (bb, i, 0)),
                 pl.BlockSpec((None, 6, d), lambda bb, i: (bb, 0, 0)),
                 pl.BlockSpec((None, 6, d), lambda bb, i: (0, 0, 0))]
    kern = functools.partial(_mmres_kernel, pro=pro, n_in=len(ins) + len(row_ins), k_gate=k_gate, seq=seq)
    return pl.pallas_call(
        kern, grid=grid, in_specs=in_specs,
        out_specs=pl.BlockSpec((None, tm, d), lambda bb, i: (bb, i, 0)),
        out_shape=jax.ShapeDtypeStruct((b, rows, d), F32),
        compiler_params=_cparams("parallel", "parallel"),
        name=name,
    )(*ins, *row_ins, w, hh, modl, modc)


def _swiglu_tile(x, wg_ref, wu_ref, wd_ref):
    gate = _dot(x, wg_ref[...].astype(BF16))
    up = _dot(x, wu_ref[...].astype(BF16))
    return _dot((_silu(gate) * up).astype(BF16), wd_ref[...].astype(BF16))


def _ffn_kernel(h_ref, modl_ref, modc_ref, g_ref, wg_ref, wu_ref, wd_ref, o_ref, u_scr, acc_scr, *, seq):
    tm = h_ref.shape[0]
    row0 = pl.program_id(1) * tm
    j = pl.program_id(2)

    @pl.when(j == 0)
    def _():
        u_scr[...] = _modulated(h_ref[...], modl_ref, modc_ref, g_ref, 3, 4, row0, seq).astype(BF16)
        acc_scr[...] = jnp.zeros_like(acc_scr)

    acc_scr[...] += _swiglu_tile(u_scr[...], wg_ref, wu_ref, wd_ref)

    @pl.when(j == pl.num_programs(2) - 1)
    def _():
        gate = _gate_rows(modl_ref, modc_ref, 5, tm, row0, seq)
        o_ref[...] = h_ref[...] + gate * acc_scr[...]


def _dense_ffn(hh, modl, modc, g, wg, wu, wd, *, tm, th, seq):
    b, nt, d = hh.shape
    hid = wg.shape[1]
    grid = (b, nt // tm, hid // th)
    return pl.pallas_call(
        functools.partial(_ffn_kernel, seq=seq),
        grid=grid,
        in_specs=[pl.BlockSpec((None, tm, d), lambda bb, i, j: (bb, i, 0)),
                  pl.BlockSpec((None, 6, d), lambda bb, i, j: (bb, 0, 0)),
                  pl.BlockSpec((None, 6, d), lambda bb, i, j: (0, 0, 0)),
                  pl.BlockSpec((1, d), lambda bb, i, j: (0, 0)),
                  pl.BlockSpec((d, th), lambda bb, i, j: (0, j)),
                  pl.BlockSpec((d, th), lambda bb, i, j: (0, j)),
                  pl.BlockSpec((th, d), lambda bb, i, j: (j, 0))],
        out_specs=pl.BlockSpec((None, tm, d), lambda bb, i, j: (bb, i, 0)),
        out_shape=jax.ShapeDtypeStruct((b, nt, d), F32),
        scratch_shapes=[pltpu.VMEM((tm, d), BF16), pltpu.VMEM((tm, d), F32)],
        compiler_params=_cparams("parallel", "parallel", "arbitrary"),
        name="dense_ffn",
    )(hh, modl, modc, g, wg, wu, wd)


def _sel01(x_f32, e_bf16):
    x1, x2, x3 = _split3(x_f32)
    return _dot(x1, e_bf16) + _dot(x2, e_bf16) + _dot(x3, e_bf16)


def _ssd_chunk(xbc, dt, nega, dskip, expand, lane0, state_ref, o_ref, *, reverse, add_skip):
    q = xbc.shape[0]
    g_n = SSM_GROUPS * SSM_STATE
    inner = xbc.shape[1] - 2 * g_n
    heads = inner // SSM_HEAD_DIM
    rp = inner // SSM_GROUPS
    r_heads = heads // SSM_GROUPS

    rows = lax.broadcasted_iota(jnp.int32, (q, q), 0)
    cols = lax.broadcasted_iota(jnp.int32, (q, q), 1)
    keep = (rows <= cols) if reverse else (rows >= cols)
    tri = keep.astype(BF16)

    la = dt * nega
    ac = _dot_exact_lhs(tri, la)
    ac_t = jnp.transpose(ac)
    last = 0 if reverse else q - 1
    ac_end = ac[last:last + 1, :]

    x = xbc[:, :inner].astype(F32)
    decay_in = jnp.exp(ac)
    xdt = x * _dot(dt.astype(BF16), expand)
    to_end = _dot(jnp.exp(ac_end - ac).astype(BF16), expand)
    from_start = _dot(decay_in.astype(BF16), expand)
    total = jnp.broadcast_to(decay_in[last:last + 1, :], (2 * SUBLANES, LANE))
    chunk_decay = _sel01(total, expand)[0:1, :]
    xdt_b = xdt.astype(BF16)
    xend_b = (xdt * to_end).astype(BF16)

    for g in range(SSM_GROUPS):
        bg = xbc[:, inner + g * SSM_STATE: inner + (g + 1) * SSM_STATE]
        cg = xbc[:, inner + g_n + g * SSM_STATE: inner + g_n + (g + 1) * SSM_STATE]
        cb = _dot_nt(cg, bg).astype(BF16)
        pieces = []
        for r in range(r_heads):
            hd = g * r_heads + r
            ln = lane0 + hd
            diff = ac[:, ln:ln + 1] - ac_t[ln:ln + 1, :]
            dec = jnp.exp(jnp.minimum(diff, 0.0).astype(BF16))
            m = jnp.where(keep, cb * dec, jnp.zeros_like(dec))
            pieces.append(_dot(m, xdt_b[:, hd * SSM_HEAD_DIM:(hd + 1) * SSM_HEAD_DIM]))
        y_diag = jnp.concatenate(pieces, axis=1)
        sl = slice(g * rp, (g + 1) * rp)
        h_in = state_ref[g]
        y_off = _dot(cg, h_in.astype(BF16)) * from_start[:, sl]
        states = _dot_tn(bg, xend_b[:, sl])
        state_ref[g] = h_in * chunk_decay[:, sl] + states
        y = y_diag + y_off
        if add_skip:
            y = y + x[:, sl] * dskip[:, sl]
        o_ref[:, sl] = y.astype(o_ref.dtype)


def _ssd_kernel(xf_ref, xb_ref, dtf_ref, dtb_ref, nega_ref, dskip_ref, exp_ref, yf_ref, yb_ref, st_ref, *, heads):
    @pl.when(pl.program_id(1) == 0)
    def _():
        st_ref[...] = jnp.zeros_like(st_ref)

    _ssd_chunk(xf_ref[...], dtf_ref[...], nega_ref[...], dskip_ref[...], exp_ref[0], 0, st_ref.at[0], yf_ref,
               reverse=False, add_skip=True)
    _ssd_chunk(xb_ref[...], dtb_ref[...], nega_ref[...], dskip_ref[...], exp_ref[1], heads, st_ref.at[1], yb_ref,
               reverse=True, add_skip=False)


def _ssd(xbc, dt, nega, dskip, expand, *, seq, heads):
    b, nt, width = xbc.shape
    q = SSM_CHUNK
    inner = heads * SSM_HEAD_DIM
    nch = nt // q
    ncl = seq // q

    def fwd(bb, j):
        return (bb, (j + ncl) % nch, 0)

    def bwd(bb, j):
        return (bb, nch - 1 - j, 0)

    return pl.pallas_call(
        functools.partial(_ssd_kernel, heads=heads),
        grid=(b, nch),
        in_specs=[pl.BlockSpec((None, q, width), fwd),
                  pl.BlockSpec((None, q, width), bwd),
                  pl.BlockSpec((None, q, LANE), fwd),
                  pl.BlockSpec((None, q, LANE), bwd),
                  pl.BlockSpec((1, LANE), lambda bb, j: (0, 0)),
                  pl.BlockSpec((1, inner), lambda bb, j: (0, 0)),
                  pl.BlockSpec((2, LANE, inner), lambda bb, j: (0, 0, 0))],
        out_specs=[pl.BlockSpec((None, q, inner), fwd),
                   pl.BlockSpec((None, q, inner), bwd)],
        out_shape=[jax.ShapeDtypeStruct((b, nt, inner), BF16)] * 2,
        scratch_shapes=[pltpu.VMEM((2, SSM_GROUPS, SSM_STATE, inner // SSM_GROUPS), F32)],
        compiler_params=_cparams("parallel", "arbitrary"),
        name="ssd_scan",
    )(xbc, xbc, dt, dt, nega, dskip, expand)


def _attn_heads(q, k, v, o_ref):
    hd = ATT_HEAD_DIM
    n_q = q.shape[1] // hd
    rep = n_q // ATT_KV_HEADS
    tq = q.shape[0]
    for g in range(ATT_KV_HEADS):
        kg = k[:, g * hd:(g + 1) * hd]
        vg = v[:, g * hd:(g + 1) * hd]
        v_aug = jnp.concatenate([vg, jnp.ones_like(vg)], axis=1)
        qg = jnp.concatenate([q[:, (g * rep + r) * hd:(g * rep + r + 1) * hd] for r in range(rep)], axis=0)
        s_t = _dot_nt(kg, qg).astype(BF16)
        m = jnp.max(s_t, axis=0, keepdims=True)
        p_t = jnp.exp2(s_t - m)
        o_aug = _dot_tn(v_aug, p_t)
        o = jnp.transpose(o_aug[:hd, :] / o_aug[hd:hd + 1, :])
        og = jnp.concatenate([o[r * tq:(r + 1) * tq, :] for r in range(rep)], axis=1)
        o_ref[:, g * rep * hd:(g + 1) * rep * hd] = og.astype(o_ref.dtype)


def _attn_kernel(q_ref, k_ref, v_ref, o_ref, *, seq):
    tq = q_ref.shape[0]
    is_lat = pl.program_id(1) * tq < seq

    @pl.when(is_lat)
    def _():
        _attn_heads(q_ref[...], k_ref[...], v_ref[...], o_ref)

    @pl.when(jnp.logical_not(is_lat))
    def _():
        _attn_heads(q_ref[...], k_ref[seq:, :], v_ref[seq:, :], o_ref)


def _attention(q, k, v, *, seq, tq):
    b, nt, dq = q.shape
    dk = k.shape[2]
    return pl.pallas_call(
        functools.partial(_attn_kernel, seq=seq),
        grid=(b, nt // tq),
        in_specs=[pl.BlockSpec((None, tq, dq), lambda bb, i: (bb, i, 0)),
                  pl.BlockSpec((None, nt, dk), lambda bb, i: (bb, 0, 0)),
                  pl.BlockSpec((None, nt, dk), lambda bb, i: (bb, 0, 0))],
        out_specs=pl.BlockSpec((None, tq, dq), lambda bb, i: (bb, i, 0)),
        out_shape=jax.ShapeDtypeStruct((b, nt, dq), BF16),
        compiler_params=_cparams("parallel", "parallel"),
        name="gqa_attention",
    )(q, k, v)


def _pool_kernel(v_ref, vp_ref, vn_ref, h_ref, modl_ref, modc_ref, wg_ref, sc_ref, o_ref, *, seq, nt):
    tm = v_ref.shape[0]
    halo = vp_ref.shape[0]
    i = pl.program_id(1)
    row0 = i * tm
    v = v_ref[...]
    vext = jnp.concatenate([vp_ref[...], v, vn_ref[...]], axis=0)
    vhi, vlo = _split2(vext)
    t = row0 + lax.broadcasted_iota(jnp.int32, (tm, 1), 0)
    s = row0 - halo + lax.broadcasted_iota(jnp.int32, (1, tm + 2 * halo), 1)
    in_ctx = t >= seq
    seg_lo = jnp.where(in_ctx, seq, 0)
    seg_hi = jnp.where(in_ctx, nt, seq)
    grp = v.shape[1] // len(POOL_WINDOWS)
    outs = []
    for gi, win in enumerate(POOL_WINDOWS):
        lo = jnp.maximum(t - win // 2, seg_lo)
        hi = jnp.minimum(t + win // 2, seg_hi)
        band = ((s >= lo) & (s < hi)).astype(BF16)
        sl = slice(gi * grp, (gi + 1) * grp)
        summed = _dot(band, vhi[:, sl]) + _dot(band, vlo[:, sl])
        pooled = summed / (hi - lo).astype(F32) - v[:, sl]
        outs.append(_dot(pooled.astype(BF16), wg_ref[gi]))
    y = jnp.concatenate(outs, axis=1) * sc_ref[...]
    gate = _gate_rows(modl_ref, modc_ref, 2, tm, row0, seq)
    o_ref[...] = h_ref[...] + gate * y


def _pool(v, hh, modl, modc, w_grp, scale, *, seq, tm):
    b, nt, d = hh.shape
    halo = POOL_HALO
    per = tm // halo
    nhb = nt // halo
    grp = w_grp.shape[1]
    return pl.pallas_call(
        functools.partial(_pool_kernel, seq=seq, nt=nt),
        grid=(b, nt // tm),
        in_specs=[pl.BlockSpec((None, tm, d), lambda bb, i: (bb, i, 0)),
                  pl.BlockSpec((None, halo, d), lambda bb, i: (bb, jnp.maximum(i * per - 1, 0), 0)),
                  pl.BlockSpec((None, halo, d), lambda bb, i: (bb, jnp.minimum((i + 1) * per, nhb - 1), 0)),
                  pl.BlockSpec((None, tm, d), lambda bb, i: (bb, i, 0)),
                  pl.BlockSpec((None, 6, d), lambda bb, i: (bb, 0, 0)),
                  pl.BlockSpec((None, 6, d), lambda bb, i: (0, 0, 0)),
                  pl.BlockSpec((len(POOL_WINDOWS), grp, grp), lambda bb, i: (0, 0, 0)),
                  pl.BlockSpec((1, d), lambda bb, i: (0, 0))],
        out_specs=pl.BlockSpec((None, tm, d), lambda bb, i: (bb, i, 0)),
        out_shape=jax.ShapeDtypeStruct((b, nt, d), F32),
        compiler_params=_cparams("parallel", "parallel"),
        name="pool_mixer",
    )(v, v, v, hh, modl, modc, w_grp, scale)


def _lru_scan_chunk(a_scr, bx_scr, h_scr, carry_scr, *, reverse):
    tc = a_scr.shape[0]

    def step(tt, h):
        t = tc - 1 - tt if reverse else tt
        h = a_scr[pl.ds(t, 1), :] * h + bx_scr[pl.ds(t, 1), :]
        h_scr[pl.ds(t, 1), :] = h
        return h

    carry_scr[...] = lax.fori_loop(0, tc, step, carry_scr[...], unroll=8)


def _lru_kernel(x_ref, w_ref, b_ref, lam_ref, o_ref, a_scr, bx_scr, h_scr, carry_scr):
    d_dir = pl.program_id(1)
    tc, width = x_ref.shape
    bw = width // LRU_BLOCKS

    @pl.when(pl.program_id(2) == 0)
    def _():
        carry_scr[...] = jnp.zeros_like(carry_scr)

    xb = x_ref[...]
    for k in range(LRU_BLOCKS):
        sl = slice(k * bw, (k + 1) * bw)
        pre = _dot(xb[:, sl], w_ref[k])
        r_gate = _sigmoid(pre[:, :bw] + b_ref[0:1, sl])
        i_gate = _sigmoid(pre[:, bw:] + b_ref[1:2, sl])
        neg_lam = -lam_ref[:, sl]
        softplus = jnp.maximum(neg_lam, 0.0) + jnp.log1p(jnp.exp(-jnp.abs(neg_lam)))
        log_a = r_gate * ((-LRU_C) * softplus)
        a = jnp.exp(log_a)
        a_scr[:, sl] = a
        v = -jnp.tanh(log_a) * (a * a + 1.0)
        root = jnp.where(v > 0.0, v * lax.rsqrt(v), 0.0)
        bx_scr[:, sl] = root * (i_gate * xb[:, sl].astype(F32))

    @pl.when(d_dir == 0)
    def _():
        _lru_scan_chunk(a_scr, bx_scr, h_scr, carry_scr, reverse=False)

    @pl.when(d_dir == 1)
    def _():
        _lru_scan_chunk(a_scr, bx_scr, h_scr, carry_scr, reverse=True)

    o_ref[...] = h_scr[...].astype(o_ref.dtype)


def _lru(xr, w_cat, bias, lam, *, seq, tc):
    b, nt, width = xr.shape
    nc = nt // tc
    nl = seq // tc

    def chunk(d, j):
        return d * (nc - 1 - j) + (1 - d) * ((j + nl) % nc)

    return pl.pallas_call(
        _lru_kernel,
        grid=(b, 2, nc),
        in_specs=[pl.BlockSpec((None, tc, width), lambda bb, d, j: (bb, chunk(d, j), 0)),
                  pl.BlockSpec((None,) + w_cat.shape[1:], lambda bb, d, j: (d, 0, 0, 0)),
                  pl.BlockSpec((None, 2, width), lambda bb, d, j: (d, 0, 0)),
                  pl.BlockSpec((None, 1, width), lambda bb, d, j: (d, 0, 0))],
        out_specs=pl.BlockSpec((None, None, tc, width), lambda bb, d, j: (d, bb, chunk(d, j), 0)),
        out_shape=jax.ShapeDtypeStruct((2, b, nt, width), BF16),
        scratch_shapes=[pltpu.VMEM((tc, width), F32)] * 3 + [pltpu.VMEM((1, width), F32)],
        compiler_params=_cparams("parallel", "arbitrary", "arbitrary"),
        name="rglru_scan",
    )(xr, w_cat, bias, lam)


def _router_kernel(h_ref, modl_ref, modc_ref, g_ref, whi_ref, wlo_ref, u_ref, meta_ref, cnt_ref, carry_scr, *,
                   seq, n_exp):
    tm = h_ref.shape[0]
    row0 = pl.program_id(1) * tm
    first = (pl.program_id(0) == 0) & (pl.program_id(1) == 0)

    @pl.when(first)
    def _():
        carry_scr[...] = jnp.zeros_like(carry_scr)

    u = _modulated(h_ref[...], modl_ref, modc_ref, g_ref, 3, 4, row0, seq)
    u_ref[...] = u
    uhi, ulo = _split2(u)
    logits = _dot(uhi, whi_ref[...]) + _dot(ulo, whi_ref[...]) + _dot(uhi, wlo_ref[...])
    lane = lax.broadcasted_iota(jnp.int32, (tm, LANE), 1)
    neg = jnp.float32(-jnp.inf)
    logits = jnp.where(lane < n_exp, logits, neg)
    v0 = jnp.max(logits, axis=-1, keepdims=True)
    lane_f = lane.astype(F32)
    i0 = jnp.min(jnp.where(logits == v0, lane_f, float(LANE)), axis=-1, keepdims=True)
    rest = jnp.where(lane_f == i0, neg, logits)
    v1 = jnp.max(rest, axis=-1, keepdims=True)
    i1 = jnp.min(jnp.where(rest == v1, lane_f, float(LANE)), axis=-1, keepdims=True)
    g0 = 1.0 / (1.0 + jnp.exp(v1 - v0))
    g1 = 1.0 - g0
    sel0 = lane_f == i0
    sel1 = lane_f == i1
    onehot = (sel0 | sel1).astype(BF16)
    rr = lax.broadcasted_iota(jnp.int32, (tm, tm), 0)
    cc = lax.broadcasted_iota(jnp.int32, (tm, tm), 1)
    before = (cc < rr).astype(BF16)
    prefix = _dot(before, onehot) + carry_scr[...]
    r0 = jnp.sum(jnp.where(sel0, prefix, 0.0), axis=-1, keepdims=True)
    r1 = jnp.sum(jnp.where(sel1, prefix, 0.0), axis=-1, keepdims=True)
    carry_scr[...] += jnp.sum(onehot.astype(F32), axis=0, keepdims=True)
    cnt_ref[...] = carry_scr[...]
    meta = jnp.where(lane == 0, i0, 0.0)
    meta = jnp.where(lane == 1, i1, meta)
    meta = jnp.where(lane == 2, g0, meta)
    meta = jnp.where(lane == 3, g1, meta)
    meta = jnp.where(lane == 4, r0, meta)
    meta = jnp.where(lane == 5, r1, meta)
    meta_ref[...] = meta


def _router(hh, modl, modc, g, whi, wlo, *, rows, tm, seq, n_exp):
    b, nt, d = hh.shape
    return pl.pallas_call(
        functools.partial(_router_kernel, seq=seq, n_exp=n_exp),
        grid=(b, rows // tm),
        in_specs=[pl.BlockSpec((None, tm, d), lambda bb, i: (bb, i, 0)),
                  pl.BlockSpec((None, 6, d), lambda bb, i: (bb, 0, 0)),
                  pl.BlockSpec((None, 6, d), lambda bb, i: (0, 0, 0)),
                  pl.BlockSpec((1, d), lambda bb, i: (0, 0)),
                  pl.BlockSpec((d, LANE), lambda bb, i: (0, 0)),
                  pl.BlockSpec((d, LANE), lambda bb, i: (0, 0))],
        out_specs=[pl.BlockSpec((None, tm, d), lambda bb, i: (bb, i, 0)),
                   pl.BlockSpec((None, tm, LANE), lambda bb, i: (bb, i, 0)),
                   pl.BlockSpec((1, LANE), lambda bb, i: (0, 0))],
        out_shape=[jax.ShapeDtypeStruct((b, rows, d), F32),
                   jax.ShapeDtypeStruct((b, rows, LANE), F32),
                   jax.ShapeDtypeStruct((1, LANE), F32)],
        scratch_shapes=[pltpu.VMEM((1, LANE), F32)],
        compiler_params=_cparams("arbitrary", "arbitrary"),
        name="moe_router",
    )(hh, modl, modc, g, whi, wlo)


def _dispatch_kernel(dest_ref, fill_ref, u_ref, xg_hbm, zero_scr, sem, *, tm, rows_per_b):
    base = (pl.program_id(0) * rows_per_b + pl.program_id(1) * tm) * TOP_K

    @pl.when((pl.program_id(0) == 0) & (pl.program_id(1) == 0))
    def _():
        zero_scr[...] = jnp.zeros_like(zero_scr)

        def fill(blk, c):
            @pl.when(fill_ref[blk] != 0)
            def _():
                row0 = pl.multiple_of(blk * MOE_BLOCK, MOE_BLOCK)
                cp = pltpu.make_async_copy(zero_scr, xg_hbm.at[pl.ds(row0, MOE_BLOCK)], sem)
                cp.start()
                cp.wait()
            return c

        lax.fori_loop(0, fill_ref.shape[0], fill, 0)

    def issue(r, c):
        for k in range(TOP_K):
            dst = xg_hbm.at[pl.ds(dest_ref[base + r * TOP_K + k], 1)]
            pltpu.make_async_copy(u_ref.at[pl.ds(r, 1)], dst, sem).start()
        return c

    lax.fori_loop(0, tm, issue, 0, unroll=8)
    for k in range(TOP_K):
        pltpu.make_async_copy(u_ref, xg_hbm.at[pl.ds(0, tm)], sem).wait()


def _dispatch(dest, fill, u, *, tm):
    b, rows, d = u.shape
    grid_spec = pltpu.PrefetchScalarGridSpec(
        num_scalar_prefetch=2, grid=(b, rows // tm),
        in_specs=[pl.BlockSpec((tm, d), lambda bb, i, ds, fl: (bb * (rows // tm) + i, 0))],
        out_specs=pl.BlockSpec(memory_space=pl.ANY),
        scratch_shapes=[pltpu.VMEM((MOE_BLOCK, d), F32), pltpu.SemaphoreType.DMA(())])
    return pl.pallas_call(
        functools.partial(_dispatch_kernel, tm=tm, rows_per_b=rows),
        grid_spec=grid_spec,
        out_shape=jax.ShapeDtypeStruct((fill.shape[0] * MOE_BLOCK, d), F32),
        compiler_params=_cparams("arbitrary", "arbitrary"),
        name="moe_dispatch",
    )(dest, fill, u.reshape(b * rows, d))


def _expert_kernel(be_ref, nu_ref, x_ref, wg_ref, wu_ref, wd_ref, y_ref, xb_scr):
    blk = pl.program_id(0)
    j = pl.program_id(1)
    used = blk < nu_ref[0]

    @pl.when(j == 0)
    def _():
        y_ref[...] = jnp.zeros_like(y_ref)

    @pl.when(used & (j == 0))
    def _():
        xb_scr[...] = x_ref[...].astype(BF16)

    @pl.when(used)
    def _():
        y_ref[...] += _swiglu_tile(xb_scr[...], wg_ref, wu_ref, wd_ref)


def _experts(block_expert, n_used, xg, wg, wu, wd, *, th, layer):
    p, d = xg.shape
    hid = wg.shape[3]
    nb = p // MOE_BLOCK
    nj = hid // th

    def blk_eff(blk, nu):
        return jnp.minimum(blk, nu[0] - 1)

    def j_eff(blk, j, nu):
        return jnp.where(blk < nu[0], j, nj - 1)

    def w_in_map(blk, j, be, nu):
        return (layer, be[blk_eff(blk, nu)], 0, j_eff(blk, j, nu))

    def w_out_map(blk, j, be, nu):
        return (layer, be[blk_eff(blk, nu)], j_eff(blk, j, nu), 0)

    grid_spec = pltpu.PrefetchScalarGridSpec(
        num_scalar_prefetch=2, grid=(nb, nj),
        in_specs=[pl.BlockSpec((MOE_BLOCK, d), lambda blk, j, be, nu: (blk_eff(blk, nu), 0)),
                  pl.BlockSpec((None, None, d, th), w_in_map),
                  pl.BlockSpec((None, None, d, th), w_in_map),
                  pl.BlockSpec((None, None, th, d), w_out_map)],
        out_specs=pl.BlockSpec((MOE_BLOCK, d), lambda blk, j, be, nu: (blk, 0)),
        scratch_shapes=[pltpu.VMEM((MOE_BLOCK, d), BF16)])
    return pl.pallas_call(
        _expert_kernel, grid_spec=grid_spec,
        out_shape=jax.ShapeDtypeStruct((p, d), F32),
        compiler_params=_cparams("arbitrary", "arbitrary"),
        name="moe_experts",
    )(block_expert, n_used, xg, wg, wu, wd)


def _combine_kernel(dest_ref, yg_hbm, meta_ref, h_ref, modl_ref, modc_ref, o_ref, y0_scr, y1_scr, sem, *, tm,
                    rows_per_b, seq):
    bb = pl.program_id(0)
    i = pl.program_id(1)
    base = (bb * rows_per_b + i * tm) * TOP_K
    bufs = (y0_scr, y1_scr)

    def issue(r, c):
        for k in range(TOP_K):
            src = yg_hbm.at[pl.ds(dest_ref[base + r * TOP_K + k], 1)]
            pltpu.make_async_copy(src, bufs[k].at[pl.ds(r, 1)], sem).start()
        return c

    lax.fori_loop(0, tm, issue, 0, unroll=8)
    for k in range(TOP_K):
        pltpu.make_async_copy(yg_hbm.at[pl.ds(0, tm)], bufs[k], sem).wait()
    meta = meta_ref[...]
    y = meta[:, 2:3] * y0_scr[...] + meta[:, 3:4] * y1_scr[...]
    gate = _gate_rows(modl_ref, modc_ref, 5, tm, i * tm, seq)
    o_ref[...] = h_ref[...] + gate * y


def _combine(dest, yg, meta, hh, modl, modc, *, tm, seq):
    b, rows, _ = meta.shape
    d = hh.shape[2]
    grid_spec = pltpu.PrefetchScalarGridSpec(
        num_scalar_prefetch=1, grid=(b, rows // tm),
        in_specs=[pl.BlockSpec(memory_space=pl.ANY),
                  pl.BlockSpec((None, tm, LANE), lambda bb, i, ds: (bb, i, 0)),
                  pl.BlockSpec((None, tm, d), lambda bb, i, ds: (bb, i, 0)),
                  pl.BlockSpec((None, 6, d), lambda bb, i, ds: (bb, 0, 0)),
                  pl.BlockSpec((None, 6, d), lambda bb, i, ds: (0, 0, 0))],
        out_specs=pl.BlockSpec((None, tm, d), lambda bb, i, ds: (bb, i, 0)),
        scratch_shapes=[pltpu.VMEM((tm, d), F32), pltpu.VMEM((tm, d), F32), pltpu.SemaphoreType.DMA(())])
    return pl.pallas_call(
        functools.partial(_combine_kernel, tm=tm, rows_per_b=rows, seq=seq),
        grid_spec=grid_spec,
        out_shape=jax.ShapeDtypeStruct((b, rows, d), F32),
        compiler_params=_cparams("arbitrary", "arbitrary"),
        name="moe_combine",
    )(dest, yg, meta, hh, modl, modc)


def _moe(hh, modl, modc, g, w_router, wg, wu, wd, *, rows, seq, layer):
    b, nt, d = hh.shape
    n_exp = w_router.shape[1]
    wr = jnp.pad(w_router, ((0, 0), (0, LANE - n_exp)))
    whi, wlo = _split2(wr)
    tm = _pick(rows, (768, 512, 256))
    u, meta, counts = _router(hh, modl, modc, g, whi, wlo, rows=rows, tm=tm, seq=seq, n_exp=n_exp)

    cnt = counts[0, :n_exp].astype(jnp.int32)
    padded = (cnt + MOE_BLOCK - 1) // MOE_BLOCK * MOE_BLOCK
    pend = jnp.cumsum(padded)
    pstart = pend - padded
    n_assign = b * rows * TOP_K
    nb = -(-(n_assign + n_exp * (MOE_BLOCK - 1)) // MOE_BLOCK)
    idx = meta[:, :, 0:TOP_K].astype(jnp.int32)
    rank = meta[:, :, 4:4 + TOP_K].astype(jnp.int32)
    dest = (sum(jnp.where(idx == e, pstart[e], 0) for e in range(n_exp)) + rank).reshape(-1)
    blk_start = jnp.arange(nb, dtype=jnp.int32) * MOE_BLOCK
    block_expert = jnp.minimum(jnp.sum(pend[None, :] <= blk_start[:, None], axis=1), n_exp - 1).astype(jnp.int32)
    n_used = (pend[-1:] // MOE_BLOCK).astype(jnp.int32)

    blk_id = jnp.arange(nb, dtype=jnp.int32)
    holds_padding = jnp.any((pend[None, :] // MOE_BLOCK - 1 == blk_id[:, None]) & (padded[None, :] > 0), axis=1)
    fill = (holds_padding | (blk_id >= n_used[0])).astype(jnp.int32)
    td = tm
    xg = _dispatch(dest, fill, u, tm=td)
    yg = _experts(block_expert, n_used, xg, wg, wu, wd, th=_pick(wg.shape[3], (512, 256, 128)), layer=layer)
    return _combine(dest, yg, meta, hh, modl, modc, tm=td, seq=seq)


def _rope_tables(seq, nt):
    rows = seq // GRID_W
    row = jnp.repeat(jnp.arange(rows), GRID_W).astype(F32)
    col = jnp.tile(jnp.arange(GRID_W), rows).astype(F32)
    n_freq = ATT_HEAD_DIM // 4
    inv_freq = ROPE_THETA ** (-jnp.arange(n_freq, dtype=F32) / n_freq)
    ang = jnp.concatenate([row[:, None] * inv_freq, col[:, None] * inv_freq], axis=-1)
    cos, sin = jnp.cos(ang), jnp.sin(ang)
    cos_h = jnp.concatenate([cos, cos], axis=-1)
    sin_h = jnp.concatenate([-sin, sin], axis=-1)
    reps = LANE // ATT_HEAD_DIM
    cos_t = jnp.tile(cos_h, (1, reps))
    sin_t = jnp.tile(sin_h, (1, reps))
    pad = nt - seq
    cos_t = jnp.concatenate([cos_t, jnp.ones((pad, LANE), F32)], axis=0)
    sin_t = jnp.concatenate([sin_t, jnp.zeros((pad, LANE), F32)], axis=0)
    return cos_t, sin_t


def _head_mean_matrix(n):
    r = jnp.arange(n) // ATT_HEAD_DIM
    return (r[:, None] == r[None, :]).astype(F32).astype(BF16) * jnp.asarray(1.0 / ATT_HEAD_DIM, BF16)


def kernel(x, c, ctx, c_ctx, w_mod, b_mod, norm_g, ssm_w_in, ssm_conv_w, ssm_conv_b, ssm_dt_bias, ssm_a_log, ssm_d, ssm_norm_g, ssm_w_out, att_w_qkv, att_q_norm, att_k_norm, att_w_out, pool_w_in, pool_w_grp, pool_scale, lru_w_in, lru_conv_w, lru_conv_b, lru_gate_w, lru_gate_b, lru_lambda, lru_w_out, ffn_w_gate, ffn_w_up, ffn_w_down, moe_w_router, moe_w_gate, moe_w_up, moe_w_down):
    b, seq, d = x.shape
    nctx = ctx.shape[1]
    nt = seq + nctx
    depth = w_mod.shape[0]
    n_mixers = 4

    bp = -(-(b + 1) // 8) * 8
    s_rows = jnp.concatenate([c, c_ctx[None, :], jnp.zeros((bp - b - 1, d), F32)], axis=0)
    mods = _mod_vectors(s_rows, w_mod, b_mod)
    modl_all = mods[:, :b].reshape(depth, b, 6, d)
    modc_all = mods[:, b:b + 1].reshape(depth, 1, 6, d)

    hh = jnp.concatenate([x, ctx], axis=1)
    tm_row = _pick(nt, (768, 512, 256))
    cos_t, sin_t = _rope_tables(seq, nt)
    out = None

    for i in range(depth):
        last = i == depth - 1
        kind, j = i % n_mixers, i // n_mixers
        modl, modc = modl_all[i], modc_all[i]
        g1 = norm_g[i, 0][None, :]
        g2 = norm_g[i, 1][None, :]
        proj = functools.partial(_proj, hh, modl, modc, g1, seq=seq, k_shift=0, k_scale=1)

        if kind == 0:
            inner = ssm_d.shape[1] * SSM_HEAD_DIM
            heads = ssm_d.shape[1]
            conv_dim = ssm_conv_w.shape[2]
            w_in = ssm_w_in[j].astype(BF16)
            w_z = w_in[:, :inner]
            w_xbc = w_in[:, inner:inner + conv_dim]
            w_dt = jnp.pad(w_in[:, inner + conv_dim:], ((0, 0), (0, LANE - 2 * heads)))
            dt_bias = jnp.pad(ssm_dt_bias[j].reshape(1, 2 * heads), ((0, 0), (0, LANE - 2 * heads)))
            z = proj(w_z, epi=_epi_plain, tm=tm_row, tn=inner, out_dtype=BF16, name="ssd_proj_z")
            xbc = proj(w_xbc, epi=_epi_conv_silu, extra=(ssm_conv_w[j], ssm_conv_b[j][None, :]),
                       extra_specs=(_col_spec(ssm_conv_w.shape[1], 512), _col_spec(1, 512)),
                       tm=nt, tn=512, out_dtype=BF16, name="ssd_proj_xbc")
            dt = proj(w_dt, epi=_epi_softplus, extra=(dt_bias,), extra_specs=(_col_spec(1, LANE),),
                      tm=tm_row, tn=LANE, out_dtype=F32, name="ssd_proj_dt")
            nega = jnp.pad(-jnp.exp(ssm_a_log[j].astype(F32)).reshape(1, 2 * heads), ((0, 0), (0, LANE - 2 * heads)))
            head_of = jnp.arange(inner)[None, :] // SSM_HEAD_DIM
            lanes = jnp.arange(LANE)[:, None]
            expand = jnp.stack([(lanes == head_of + dd * heads) for dd in range(2)]).astype(F32).astype(BF16)
            dskip = jnp.repeat(ssm_d[j].astype(F32), SSM_HEAD_DIM)[None, :]
            yf, yb = _ssd(xbc, dt, nega, dskip, expand, seq=seq, heads=heads)
            hh = _mmres((yf, yb, z), (inner, inner, inner), ssm_w_out[j].astype(BF16), hh, modl, modc,
                        pro=_pro_ssd, tm=tm_row, k_gate=2, seq=seq, name="ssd_out",
                        row_ins=(ssm_norm_g[j][None, :],))
        elif kind == 1:
            nq = att_w_out.shape[1]
            nk = ATT_KV_HEADS * ATT_HEAD_DIM
            w_qkv = att_w_qkv[j].astype(BF16)
            tab_specs = (pl.BlockSpec((tm_row, LANE), lambda bb, ii, jj: (ii, 0)),) * 2

            def qk_extra(n, gain):
                return ((_head_mean_matrix(n), jnp.tile(gain, n // ATT_HEAD_DIM)[None, :], cos_t, sin_t),
                        (pl.BlockSpec((n, n), lambda bb, ii, jj: (0, 0)),
                         pl.BlockSpec((1, n), lambda bb, ii, jj: (0, 0))) + tab_specs)

            ex, sp = qk_extra(nq, att_q_norm[j])
            q = proj(w_qkv[:, :nq], epi=functools.partial(_epi_qk, out_scale=ATT_HEAD_DIM ** -0.5 * math.log2(math.e)),
                     extra=ex,
                     extra_specs=sp, tm=tm_row, tn=nq, out_dtype=BF16, name="att_proj_q")
            ex, sp = qk_extra(nk, att_k_norm[j])
            k = proj(w_qkv[:, nq:nq + nk], epi=functools.partial(_epi_qk, out_scale=1.0), extra=ex,
                     extra_specs=sp, tm=tm_row, tn=nk, out_dtype=BF16, name="att_proj_k")
            v = proj(w_qkv[:, nq + nk:], epi=_epi_plain, tm=tm_row, tn=nk, out_dtype=BF16, name="att_proj_v")
            o = _attention(q, k, v, seq=seq, tq=_pick(nctx, (128,)))
            hh = _mmres((o,), (nq,), att_w_out[j].astype(BF16), hh, modl, modc, pro=_pro_plain, tm=tm_row,
                        k_gate=2, seq=seq, name="att_out")
        elif kind == 2:
            v = proj(pool_w_in[j].astype(BF16), epi=_epi_plain, tm=tm_row, tn=d, out_dtype=F32,
                     name="pool_proj")
            hh = _pool(v, hh, modl, modc, pool_w_grp[j].astype(BF16), pool_scale[j][None, :], seq=seq,
                       tm=_pick(nctx, (256, 128)))
        else:
            width = lru_w_out.shape[1]
            w_in = lru_w_in[j].astype(BF16)
            gg = proj(w_in[:, :width], epi=_epi_gelu, tm=tm_row, tn=width, out_dtype=BF16, name="lru_proj_gate")
            tn_x = _pick(width, (640, 256, 128))
            xr = proj(w_in[:, width:], epi=_epi_conv_bias, extra=(lru_conv_w[j], lru_conv_b[j][None, :]),
                      extra_specs=(_col_spec(lru_conv_w.shape[1], tn_x), _col_spec(1, tn_x)),
                      tm=nt, tn=tn_x, out_dtype=BF16, name="lru_proj_x")
            gw = lru_gate_w[j]
            w_cat = jnp.concatenate([gw[:, 0], gw[:, 1]], axis=-1).astype(BF16)
            hs = _lru(xr, w_cat, lru_gate_b[j], lru_lambda[j][:, None, :], seq=seq, tc=_pick(nctx, (256, 128)))
            rows = seq if last else nt
            hh = _mmres(((hs, 0), (hs, 1), gg), (width, width, width), lru_w_out[j].astype(BF16), hh, modl, modc,
                        pro=_pro_lru, tm=_pick(rows, (1024, 768, 512, 256)), k_gate=2, seq=seq, name="lru_out",
                        rows=rows)

        kf = i // 2
        if i % 2 == 0:
            hh = _dense_ffn(hh, modl, modc, g2, ffn_w_gate[kf].astype(BF16), ffn_w_up[kf].astype(BF16),
                            ffn_w_down[kf].astype(BF16), tm=tm_row,
                            th=_pick(ffn_w_gate.shape[2], (512, 256, 128)), seq=seq)
        else:
            rows = seq if last else nt
            hh = _moe(hh, modl, modc, g2, moe_w_router[kf], moe_w_gate, moe_w_up, moe_w_down, rows=rows, seq=seq,
                      layer=kf)
        if last:
            out = hh[:, :seq] if hh.shape[1] != seq else hh
    return out
```

```python
import functools
import math

import jax
import jax.numpy as jnp
from jax import lax
from jax.experimental import pallas as pl
from jax.experimental.pallas import tpu as pltpu

F32 = jnp.float32
BF16 = jnp.bfloat16
EPS = 1e-6

GRID_W = 64
SSM_HEAD_DIM = 64
SSM_GROUPS = 4
SSM_STATE = 128
SSM_CHUNK = 128
ATT_HEAD_DIM = 64
ATT_KV_HEADS = 4
ROPE_THETA = 10000.0
POOL_WINDOWS = (2, 4, 8, 16)
LRU_BLOCKS = 10
LRU_C = 8.0
TOP_K = 2
MOE_BLOCK = 1024

LANE = 128
SUBLANES = 8
POOL_HALO = 64
VMEM_LIMIT = 56 * 1024 * 1024


def _cparams(*sem):
    return pltpu.CompilerParams(dimension_semantics=sem, vmem_limit_bytes=VMEM_LIMIT)


def _pick(n, candidates):
    for c in candidates:
        if n % c == 0:
            return c
    raise ValueError(f"no tile in {candidates} divides {n}")


def _sigmoid(x):
    return 0.5 * jnp.tanh(0.5 * x) + 0.5


def _silu(x):
    return x * _sigmoid(x)


def _split2(x):
    hi = x.astype(BF16)
    lo = (x - hi.astype(F32)).astype(BF16)
    return hi, lo


def _split3(x):
    x1 = x.astype(BF16)
    r = x - x1.astype(F32)
    x2 = r.astype(BF16)
    x3 = (r - x2.astype(F32)).astype(BF16)
    return x1, x2, x3


def _dot(a, b):
    return jnp.dot(a, b, preferred_element_type=F32)


def _dot_nt(a, b):
    return lax.dot_general(a, b, (((1,), (1,)), ((), ())), preferred_element_type=F32)


def _dot_tn(a, b):
    return lax.dot_general(a, b, (((0,), (0,)), ((), ())), preferred_element_type=F32)


def _dot_exact_lhs(a_bf16, x_f32):
    x1, x2, x3 = _split3(x_f32)
    return _dot(a_bf16, x1) + _dot(a_bf16, x2) + _dot(a_bf16, x3)


def _modulated(h, modl_ref, modc_ref, g_ref, k_shift, k_scale, row0, seq):
    tm = h.shape[0]
    row = row0 + lax.broadcasted_iota(jnp.int32, (tm, 1), 0)
    is_ctx = row >= seq
    shift = jnp.where(is_ctx, modc_ref[k_shift:k_shift + 1, :], modl_ref[k_shift:k_shift + 1, :])
    scale = jnp.where(is_ctx, modc_ref[k_scale:k_scale + 1, :], modl_ref[k_scale:k_scale + 1, :])
    ms = jnp.mean(h * h, axis=-1, keepdims=True)
    y = h * lax.rsqrt(ms + EPS) * g_ref[...]
    return y * (1.0 + scale) + shift


def _gate_rows(modl_ref, modc_ref, k_gate, tm, row0, seq):
    row = row0 + lax.broadcasted_iota(jnp.int32, (tm, 1), 0)
    return jnp.where(row >= seq, modc_ref[k_gate:k_gate + 1, :], modl_ref[k_gate:k_gate + 1, :])


def _mod_kernel(s_ref, w_ref, b_ref, o_ref):
    s = _silu(s_ref[...])
    o_ref[...] = _dot(s.astype(BF16), w_ref[...].astype(BF16)) + b_ref[...]


def _mod_vectors(s_rows, w_mod, b_mod):
    depth, d, n6 = w_mod.shape
    bp = s_rows.shape[0]
    tn = _pick(n6, (1536, 1024, 512, 256, 128))
    return pl.pallas_call(
        _mod_kernel,
        grid=(depth, n6 // tn),
        in_specs=[pl.BlockSpec((bp, d), lambda l, j: (0, 0)),
                  pl.BlockSpec((None, d, tn), lambda l, j: (l, 0, j)),
                  pl.BlockSpec((None, 1, tn), lambda l, j: (l, 0, j))],
        out_specs=pl.BlockSpec((None, bp, tn), lambda l, j: (l, 0, j)),
        out_shape=jax.ShapeDtypeStruct((depth, bp, n6), F32),
        compiler_params=_cparams("parallel", "parallel"),
        name="mod_vectors",
    )(s_rows, w_mod, b_mod.reshape(depth, 1, n6))


def _seg_conv(x, w_ref, seq):
    nt = x.shape[0]
    row = lax.broadcasted_iota(jnp.int32, (nt, 1), 0)
    in_ctx = row >= seq
    pos = jnp.where(in_ctx, row - seq, row)
    seglen = jnp.where(in_ctx, nt - seq, seq)
    out = x * w_ref[2:3, :]
    for k, off in ((0, -2), (1, -1), (3, 1)):
        shifted = pltpu.roll(x, (-off) % nt, axis=0)
        valid = (pos + off >= 0) & (pos + off < seglen)
        out = out + jnp.where(valid, shifted, 0.0) * w_ref[k:k + 1, :]
    return out


def _epi_plain(acc, extra, row0, seq):
    return acc


def _epi_softplus(acc, extra, row0, seq):
    (b_ref,) = extra
    x = acc + b_ref[...]
    return jnp.maximum(x, 0.0) + jnp.log1p(jnp.exp(-jnp.abs(x)))


def _epi_conv_silu(acc, extra, row0, seq):
    w_ref, b_ref = extra
    return _silu(_seg_conv(acc, w_ref, seq) + b_ref[...])


def _epi_conv_bias(acc, extra, row0, seq):
    w_ref, b_ref = extra
    return _seg_conv(acc, w_ref, seq) + b_ref[...]


def _epi_gelu(acc, extra, row0, seq):
    return jax.nn.gelu(acc)


def _epi_qk(acc, extra, row0, seq, *, out_scale):
    g_ref, gain_ref, cos_ref, sin_ref = extra
    tn = acc.shape[1]
    ms = _dot((acc * acc).astype(BF16), g_ref[...])
    xn = acc * lax.rsqrt(ms + EPS) * gain_ref[...]
    lane = lax.broadcasted_iota(jnp.int32, (1, tn), 1)
    half = ATT_HEAD_DIM // 2
    second = (lane % ATT_HEAD_DIM) >= half
    partner = jnp.where(second, pltpu.roll(xn, half, axis=1), pltpu.roll(xn, tn - half, axis=1))
    reps = tn // LANE
    cos = jnp.concatenate([cos_ref[...]] * reps, axis=1) if reps > 1 else cos_ref[...]
    sin = jnp.concatenate([sin_ref[...]] * reps, axis=1) if reps > 1 else sin_ref[...]
    return (xn * cos + partner * sin) * out_scale


def _proj_kernel(*refs, epi, n_extra, k_shift, k_scale, seq):
    h_ref, modl_ref, modc_ref, g_ref, w_ref = refs[:5]
    extra = refs[5:5 + n_extra]
    o_ref = refs[5 + n_extra]
    u_scr = refs[6 + n_extra]
    tm = h_ref.shape[0]
    row0 = pl.program_id(1) * tm

    @pl.when(pl.program_id(2) == 0)
    def _():
        u_scr[...] = _modulated(h_ref[...], modl_ref, modc_ref, g_ref, k_shift, k_scale, row0, seq).astype(BF16)

    acc = _dot(u_scr[...], w_ref[...])
    o_ref[...] = epi(acc, extra, row0, seq).astype(o_ref.dtype)


def _proj(hh, modl, modc, g, w, *, epi, extra=(), extra_specs=(), tm, tn, out_dtype, seq, k_shift, k_scale,
          name):
    b, nt, d = hh.shape
    n = w.shape[1]
    grid = (b, nt // tm, n // tn)
    in_specs = [pl.BlockSpec((None, tm, d), lambda bb, i, j: (bb, i, 0)),
                pl.BlockSpec((None, 6, d), lambda bb, i, j: (bb, 0, 0)),
                pl.BlockSpec((None, 6, d), lambda bb, i, j: (0, 0, 0)),
                pl.BlockSpec((1, d), lambda bb, i, j: (0, 0)),
                pl.BlockSpec((d, tn), lambda bb, i, j: (0, j))] + list(extra_specs)
    kern = functools.partial(_proj_kernel, epi=epi, n_extra=len(extra), k_shift=k_shift, k_scale=k_scale, seq=seq)
    return pl.pallas_call(
        kern, grid=grid, in_specs=in_specs,
        out_specs=pl.BlockSpec((None, tm, tn), lambda bb, i, j: (bb, i, j)),
        out_shape=jax.ShapeDtypeStruct((b, nt, n), out_dtype),
        scratch_shapes=[pltpu.VMEM((tm, d), BF16)],
        compiler_params=_cparams("parallel", "parallel", "arbitrary"),
        name=name,
    )(hh, modl, modc, g, w, *extra)


def _col_spec(rows, tn):
    return pl.BlockSpec((rows, tn), lambda bb, i, j: (0, j))


def _pro_plain(ins):
    (a_ref,) = ins
    return a_ref[...]


def _pro_ssd(ins):
    yf_ref, yb_ref, z_ref, ng_ref = ins
    y = (yf_ref[...].astype(F32) + yb_ref[...].astype(F32)) * _silu(z_ref[...].astype(F32))
    ms = jnp.mean(y * y, axis=-1, keepdims=True)
    return (y * lax.rsqrt(ms + EPS) * ng_ref[...]).astype(BF16)


def _pro_lru(ins):
    hf_ref, hb_ref, gg_ref = ins
    hsum = (hf_ref[...].astype(F32) + hb_ref[...].astype(F32))
    return (hsum * gg_ref[...].astype(F32)).astype(BF16)


def _mmres_kernel(*refs, pro, n_in, k_gate, seq):
    ins = refs[:n_in]
    w_ref, h_ref, modl_ref, modc_ref, o_ref = refs[n_in:n_in + 5]
    tm = h_ref.shape[0]
    row0 = pl.program_id(1) * tm
    y = _dot(pro(ins), w_ref[...])
    gate = _gate_rows(modl_ref, modc_ref, k_gate, tm, row0, seq)
    o_ref[...] = h_ref[...] + gate * y


def _mmres(ins, in_widths, w, hh, modl, modc, *, pro, tm, k_gate, seq, name, rows=None, row_ins=()):
    b, nt, d = hh.shape
    rows = nt if rows is None else rows
    k = w.shape[0]
    grid = (b, rows // tm)
    in_specs = []
    for item, wd in zip(ins, in_widths):
        if isinstance(item, tuple):
            in_specs.append(pl.BlockSpec((None, None, tm, wd), functools.partial(lambda bb, i, s: (s, bb, i, 0), s=item[1])))
        else:
            in_specs.append(pl.BlockSpec((None, tm, wd), lambda bb, i: (bb, i, 0)))
    ins = [item[0] if isinstance(item, tuple) else item for item in ins]
    in_specs += [pl.BlockSpec((1, r.shape[1]), lambda bb, i: (0, 0)) for r in row_ins]
    in_specs += [pl.BlockSpec((k, d), lambda bb, i: (0, 0)),
                 pl.BlockSpec((None, tm, d), lambda bb, i: (bb, i, 0)),
                 pl.BlockSpec((None, 6, d), lambda bb, i: (bb, 0, 0)),
                 pl.BlockSpec((None, 6, d), lambda bb, i: (0, 0, 0))]
    kern = functools.partial(_mmres_kernel, pro=pro, n_in=len(ins) + len(row_ins), k_gate=k_gate, seq=seq)
    return pl.pallas_call(
        kern, grid=grid, in_specs=in_specs,
        out_specs=pl.BlockSpec((None, tm, d), lambda bb, i: (bb, i, 0)),
        out_shape=jax.ShapeDtypeStruct((b, rows, d), F32),
        compiler_params=_cparams("parallel", "parallel"),
        name=name,
    )(*ins, *row_ins, w, hh, modl, modc)


def _swiglu_tile(x, wg_ref, wu_ref, wd_ref):
    gate = _dot(x, wg_ref[...].astype(BF16))
    up = _dot(x, wu_ref[...].astype(BF16))
    return _dot((_silu(gate) * up).astype(BF16), wd_ref[...].astype(BF16))


def _ffn_kernel(h_ref, modl_ref, modc_ref, g_ref, wg_ref, wu_ref, wd_ref, o_ref, u_scr, acc_scr, *, seq):
    tm = h_ref.shape[0]
    row0 = pl.program_id(1) * tm
    j = pl.program_id(2)

    @pl.when(j == 0)
    def _():
        u_scr[...] = _modulated(h_ref[...], modl_ref, modc_ref, g_ref, 3, 4, row0, seq).astype(BF16)
        acc_scr[...] = jnp.zeros_like(acc_scr)

    acc_scr[...] += _swiglu_tile(u_scr[...], wg_ref, wu_ref, wd_ref)

    @pl.when(j == pl.num_programs(2) - 1)
    def _():
        gate = _gate_rows(modl_ref, modc_ref, 5, tm, row0, seq)
        o_ref[...] = h_ref[...] + gate * acc_scr[...]


def _dense_ffn(hh, modl, modc, g, wg, wu, wd, *, tm, th, seq):
    b, nt, d = hh.shape
    hid = wg.shape[1]
    grid = (b, nt // tm, hid // th)
    return pl.pallas_call(
        functools.partial(_ffn_kernel, seq=seq),
        grid=grid,
        in_specs=[pl.BlockSpec((None, tm, d), lambda bb, i, j: (bb, i, 0)),
                  pl.BlockSpec((None, 6, d), lambda bb, i, j: (bb, 0, 0)),
                  pl.BlockSpec((None, 6, d), lambda bb, i, j: (0, 0, 0)),
                  pl.BlockSpec((1, d), lambda bb, i, j: (0, 0)),
                  pl.BlockSpec((d, th), lambda bb, i, j: (0, j)),
                  pl.BlockSpec((d, th), lambda bb, i, j: (0, j)),
                  pl.BlockSpec((th, d), lambda bb, i, j: (j, 0))],
        out_specs=pl.BlockSpec((None, tm, d), lambda bb, i, j: (bb, i, 0)),
        out_shape=jax.ShapeDtypeStruct((b, nt, d), F32),
        scratch_shapes=[pltpu.VMEM((tm, d), BF16), pltpu.VMEM((tm, d), F32)],
        compiler_params=_cparams("parallel", "parallel", "arbitrary"),
        name="dense_ffn",
    )(hh, modl, modc, g, wg, wu, wd)


def _sel01(x_f32, e_bf16):
    x1, x2, x3 = _split3(x_f32)
    return _dot(x1, e_bf16) + _dot(x2, e_bf16) + _dot(x3, e_bf16)


def _ssd_chunk(xbc, dt, nega, dskip, expand, lane0, state_ref, o_ref, *, reverse, add_skip):
    q = xbc.shape[0]
    g_n = SSM_GROUPS * SSM_STATE
    inner = xbc.shape[1] - 2 * g_n
    heads = inner // SSM_HEAD_DIM
    rp = inner // SSM_GROUPS
    r_heads = heads // SSM_GROUPS

    rows = lax.broadcasted_iota(jnp.int32, (q, q), 0)
    cols = lax.broadcasted_iota(jnp.int32, (q, q), 1)
    keep = (rows <= cols) if reverse else (rows >= cols)
    tri = keep.astype(BF16)

    la = dt * nega
    ac = _dot_exact_lhs(tri, la)
    ac_t = jnp.transpose(ac)
    last = 0 if reverse else q - 1
    ac_end = ac[last:last + 1, :]

    x = xbc[:, :inner].astype(F32)
    decay_in = jnp.exp(ac)
    xdt = x * _dot(dt.astype(BF16), expand)
    to_end = _dot(jnp.exp(ac_end - ac).astype(BF16), expand)
    from_start = _dot(decay_in.astype(BF16), expand)
    total = jnp.broadcast_to(decay_in[last:last + 1, :], (2 * SUBLANES, LANE))
    chunk_decay = _sel01(total, expand)[0:1, :]
    xdt_b = xdt.astype(BF16)
    xend_b = (xdt * to_end).astype(BF16)

    for g in range(SSM_GROUPS):
        bg = xbc[:, inner + g * SSM_STATE: inner + (g + 1) * SSM_STATE]
        cg = xbc[:, inner + g_n + g * SSM_STATE: inner + g_n + (g + 1) * SSM_STATE]
        cb = _dot_nt(cg, bg).astype(BF16)
        pieces = []
        for r in range(r_heads):
            hd = g * r_heads + r
            ln = lane0 + hd
            diff = ac[:, ln:ln + 1] - ac_t[ln:ln + 1, :]
            dec = jnp.exp(jnp.minimum(diff, 0.0).astype(BF16))
            m = jnp.where(keep, cb * dec, jnp.zeros_like(dec))
            pieces.append(_dot(m, xdt_b[:, hd * SSM_HEAD_DIM:(hd + 1) * SSM_HEAD_DIM]))
        y_diag = jnp.concatenate(pieces, axis=1)
        sl = slice(g * rp, (g + 1) * rp)
        h_in = state_ref[g]
        y_off = _dot(cg, h_in.astype(BF16)) * from_start[:, sl]
        states = _dot_tn(bg, xend_b[:, sl])
        state_ref[g] = h_in * chunk_decay[:, sl] + states
        y = y_diag + y_off
        if add_skip:
            y = y + x[:, sl] * dskip[:, sl]
        o_ref[:, sl] = y.astype(o_ref.dtype)


def _ssd_kernel(xf_ref, xb_ref, dtf_ref, dtb_ref, nega_ref, dskip_ref, exp_ref, yf_ref, yb_ref, st_ref, *, heads):
    @pl.when(pl.program_id(1) == 0)
    def _():
        st_ref[...] = jnp.zeros_like(st_ref)

    _ssd_chunk(xf_ref[...], dtf_ref[...], nega_ref[...], dskip_ref[...], exp_ref[0], 0, st_ref.at[0], yf_ref,
               reverse=False, add_skip=True)
    _ssd_chunk(xb_ref[...], dtb_ref[...], nega_ref[...], dskip_ref[...], exp_ref[1], heads, st_ref.at[1], yb_ref,
               reverse=True, add_skip=False)


def _ssd(xbc, dt, nega, dskip, expand, *, seq, heads):
    b, nt, width = xbc.shape
    q = SSM_CHUNK
    inner = heads * SSM_HEAD_DIM
    nch = nt // q
    ncl = seq // q

    def fwd(bb, j):
        return (bb, (j + ncl) % nch, 0)

    def bwd(bb, j):
        return (bb, nch - 1 - j, 0)

    return pl.pallas_call(
        functools.partial(_ssd_kernel, heads=heads),
        grid=(b, nch),
        in_specs=[pl.BlockSpec((None, q, width), fwd),
                  pl.BlockSpec((None, q, width), bwd),
                  pl.BlockSpec((None, q, LANE), fwd),
                  pl.BlockSpec((None, q, LANE), bwd),
                  pl.BlockSpec((1, LANE), lambda bb, j: (0, 0)),
                  pl.BlockSpec((1, inner), lambda bb, j: (0, 0)),
                  pl.BlockSpec((2, LANE, inner), lambda bb, j: (0, 0, 0))],
        out_specs=[pl.BlockSpec((None, q, inner), fwd),
                   pl.BlockSpec((None, q, inner), bwd)],
        out_shape=[jax.ShapeDtypeStruct((b, nt, inner), BF16)] * 2,
        scratch_shapes=[pltpu.VMEM((2, SSM_GROUPS, SSM_STATE, inner // SSM_GROUPS), F32)],
        compiler_params=_cparams("parallel", "arbitrary"),
        name="ssd_scan",
    )(xbc, xbc, dt, dt, nega, dskip, expand)


def _attn_heads(q, k, v, o_ref):
    hd = ATT_HEAD_DIM
    n_q = q.shape[1] // hd
    rep = n_q // ATT_KV_HEADS
    tq = q.shape[0]
    for g in range(ATT_KV_HEADS):
        kg = k[:, g * hd:(g + 1) * hd]
        vg = v[:, g * hd:(g + 1) * hd]
        v_aug = jnp.concatenate([vg, jnp.ones_like(vg)], axis=1)
        qg = jnp.concatenate([q[:, (g * rep + r) * hd:(g * rep + r + 1) * hd] for r in range(rep)], axis=0)
        s_t = _dot_nt(kg, qg).astype(BF16)
        m = jnp.max(s_t, axis=0, keepdims=True)
        p_t = jnp.exp2(s_t - m)
        o_aug = _dot_tn(v_aug, p_t)
        o = jnp.transpose(o_aug[:hd, :] / o_aug[hd:hd + 1, :])
        og = jnp.concatenate([o[r * tq:(r + 1) * tq, :] for r in range(rep)], axis=1)
        o_ref[:, g * rep * hd:(g + 1) * rep * hd] = og.astype(o_ref.dtype)


def _attn_kernel(q_ref, k_ref, v_ref, o_ref, *, seq):
    tq = q_ref.shape[0]
    is_lat = pl.program_id(1) * tq < seq

    @pl.when(is_lat)
    def _():
        _attn_heads(q_ref[...], k_ref[...], v_ref[...], o_ref)

    @pl.when(jnp.logical_not(is_lat))
    def _():
        _attn_heads(q_ref[...], k_ref[seq:, :], v_ref[seq:, :], o_ref)


def _attention(q, k, v, *, seq, tq):
    b, nt, dq = q.shape
    dk = k.shape[2]
    return pl.pallas_call(
        functools.partial(_attn_kernel, seq=seq),
        grid=(b, nt // tq),
        in_specs=[pl.BlockSpec((None, tq, dq), lambda bb, i: (bb, i, 0)),
                  pl.BlockSpec((None, nt, dk), lambda bb, i: (bb, 0, 0)),
                  pl.BlockSpec((None, nt, dk), lambda bb, i: (bb, 0, 0))],
        out_specs=pl.BlockSpec((None, tq, dq), lambda bb, i: (bb, i, 0)),
        out_shape=jax.ShapeDtypeStruct((b, nt, dq), BF16),
        compiler_params=_cparams("parallel", "parallel"),
        name="gqa_attention",
    )(q, k, v)


def _pool_kernel(v_ref, vp_ref, vn_ref, h_ref, modl_ref, modc_ref, wg_ref, sc_ref, o_ref, *, seq, nt):
    tm = v_ref.shape[0]
    halo = vp_ref.shape[0]
    i = pl.program_id(1)
    row0 = i * tm
    v = v_ref[...]
    vext = jnp.concatenate([vp_ref[...], v, vn_ref[...]], axis=0)
    vhi, vlo = _split2(vext)
    t = row0 + lax.broadcasted_iota(jnp.int32, (tm, 1), 0)
    s = row0 - halo + lax.broadcasted_iota(jnp.int32, (1, tm + 2 * halo), 1)
    in_ctx = t >= seq
    seg_lo = jnp.where(in_ctx, seq, 0)
    seg_hi = jnp.where(in_ctx, nt, seq)
    grp = v.shape[1] // len(POOL_WINDOWS)
    outs = []
    for gi, win in enumerate(POOL_WINDOWS):
        lo = jnp.maximum(t - win // 2, seg_lo)
        hi = jnp.minimum(t + win // 2, seg_hi)
        band = ((s >= lo) & (s < hi)).astype(BF16)
        sl = slice(gi * grp, (gi + 1) * grp)
        summed = _dot(band, vhi[:, sl]) + _dot(band, vlo[:, sl])
        pooled = summed / (hi - lo).astype(F32) - v[:, sl]
        outs.append(_dot(pooled.astype(BF16), wg_ref[gi]))
    y = jnp.concatenate(outs, axis=1) * sc_ref[...]
    gate = _gate_rows(modl_ref, modc_ref, 2, tm, row0, seq)
    o_ref[...] = h_ref[...] + gate * y


def _pool(v, hh, modl, modc, w_grp, scale, *, seq, tm):
    b, nt, d = hh.shape
    halo = POOL_HALO
    per = tm // halo
    nhb = nt // halo
    grp = w_grp.shape[1]
    return pl.pallas_call(
        functools.partial(_pool_kernel, seq=seq, nt=nt),
        grid=(b, nt // tm),
        in_specs=[pl.BlockSpec((None, tm, d), lambda bb, i: (bb, i, 0)),
                  pl.BlockSpec((None, halo, d), lambda bb, i: (bb, jnp.maximum(i * per - 1, 0), 0)),
                  pl.BlockSpec((None, halo, d), lambda bb, i: (bb, jnp.minimum((i + 1) * per, nhb - 1), 0)),
                  pl.BlockSpec((None, tm, d), lambda bb, i: (bb, i, 0)),
                  pl.BlockSpec((None, 6, d), lambda bb, i: (bb, 0, 0)),
                  pl.BlockSpec((None, 6, d), lambda bb, i: (0, 0, 0)),
                  pl.BlockSpec((len(POOL_WINDOWS), grp, grp), lambda bb, i: (0, 0, 0)),
                  pl.BlockSpec((1, d), lambda bb, i: (0, 0))],
        out_specs=pl.BlockSpec((None, tm, d), lambda bb, i: (bb, i, 0)),
        out_shape=jax.ShapeDtypeStruct((b, nt, d), F32),
        compiler_params=_cparams("parallel", "parallel"),
        name="pool_mixer",
    )(v, v, v, hh, modl, modc, w_grp, scale)


def _lru_scan_chunk(a_scr, bx_scr, h_scr, carry_scr, *, reverse):
    tc = a_scr.shape[0]

    def step(tt, h):
        t = tc - 1 - tt if reverse else tt
        h = a_scr[pl.ds(t, 1), :] * h + bx_scr[pl.ds(t, 1), :]
        h_scr[pl.ds(t, 1), :] = h
        return h

    carry_scr[...] = lax.fori_loop(0, tc, step, carry_scr[...], unroll=8)


def _lru_kernel(x_ref, w_ref, b_ref, lam_ref, o_ref, a_scr, bx_scr, h_scr, carry_scr):
    d_dir = pl.program_id(1)
    tc, width = x_ref.shape
    bw = width // LRU_BLOCKS

    @pl.when(pl.program_id(2) == 0)
    def _():
        carry_scr[...] = jnp.zeros_like(carry_scr)

    xb = x_ref[...]
    for k in range(LRU_BLOCKS):
        sl = slice(k * bw, (k + 1) * bw)
        pre = _dot(xb[:, sl], w_ref[k])
        r_gate = _sigmoid(pre[:, :bw] + b_ref[0:1, sl])
        i_gate = _sigmoid(pre[:, bw:] + b_ref[1:2, sl])
        neg_lam = -lam_ref[:, sl]
        softplus = jnp.maximum(neg_lam, 0.0) + jnp.log1p(jnp.exp(-jnp.abs(neg_lam)))
        log_a = r_gate * ((-LRU_C) * softplus)
        a = jnp.exp(log_a)
        a_scr[:, sl] = a
        v = -jnp.tanh(log_a) * (a * a + 1.0)
        root = jnp.where(v > 0.0, v * lax.rsqrt(v), 0.0)
        bx_scr[:, sl] = root * (i_gate * xb[:, sl].astype(F32))

    @pl.when(d_dir == 0)
    def _():
        _lru_scan_chunk(a_scr, bx_scr, h_scr, carry_scr, reverse=False)

    @pl.when(d_dir == 1)
    def _():
        _lru_scan_chunk(a_scr, bx_scr, h_scr, carry_scr, reverse=True)

    o_ref[...] = h_scr[...].astype(o_ref.dtype)


def _lru(xr, w_cat, bias, lam, *, seq, tc):
    b, nt, width = xr.shape
    nc = nt // tc
    nl = seq // tc

    def chunk(d, j):
        return d * (nc - 1 - j) + (1 - d) * ((j + nl) % nc)

    return pl.pallas_call(
        _lru_kernel,
        grid=(b, 2, nc),
        in_specs=[pl.BlockSpec((None, tc, width), lambda bb, d, j: (bb, chunk(d, j), 0)),
                  pl.BlockSpec((None,) + w_cat.shape[1:], lambda bb, d, j: (d, 0, 0, 0)),
                  pl.BlockSpec((None, 2, width), lambda bb, d, j: (d, 0, 0)),
                  pl.BlockSpec((None, 1, width), lambda bb, d, j: (d, 0, 0))],
        out_specs=pl.BlockSpec((None, None, tc, width), lambda bb, d, j: (d, bb, chunk(d, j), 0)),
        out_shape=jax.ShapeDtypeStruct((2, b, nt, width), BF16),
        scratch_shapes=[pltpu.VMEM((tc, width), F32)] * 3 + [pltpu.VMEM((1, width), F32)],
        compiler_params=_cparams("parallel", "arbitrary", "arbitrary"),
        name="rglru_scan",
    )(xr, w_cat, bias, lam)


def _router_kernel(h_ref, modl_ref, modc_ref, g_ref, whi_ref, wlo_ref, u_ref, meta_ref, cnt_ref, carry_scr, *,
                   seq, n_exp):
    tm = h_ref.shape[0]
    row0 = pl.program_id(1) * tm
    first = (pl.program_id(0) == 0) & (pl.program_id(1) == 0)

    @pl.when(first)
    def _():
        carry_scr[...] = jnp.zeros_like(carry_scr)

    u = _modulated(h_ref[...], modl_ref, modc_ref, g_ref, 3, 4, row0, seq)
    u_ref[...] = u
    uhi, ulo = _split2(u)
    logits = _dot(uhi, whi_ref[...]) + _dot(ulo, whi_ref[...]) + _dot(uhi, wlo_ref[...])
    lane = lax.broadcasted_iota(jnp.int32, (tm, LANE), 1)
    neg = jnp.float32(-jnp.inf)
    logits = jnp.where(lane < n_exp, logits, neg)
    v0 = jnp.max(logits, axis=-1, keepdims=True)
    lane_f = lane.astype(F32)
    i0 = jnp.min(jnp.where(logits == v0, lane_f, float(LANE)), axis=-1, keepdims=True)
    rest = jnp.where(lane_f == i0, neg, logits)
    v1 = jnp.max(rest, axis=-1, keepdims=True)
    i1 = jnp.min(jnp.where(rest == v1, lane_f, float(LANE)), axis=-1, keepdims=True)
    g0 = 1.0 / (1.0 + jnp.exp(v1 - v0))
    g1 = 1.0 - g0
    sel0 = lane_f == i0
    sel1 = lane_f == i1
    onehot = (sel0 | sel1).astype(BF16)
    rr = lax.broadcasted_iota(jnp.int32, (tm, tm), 0)
    cc = lax.broadcasted_iota(jnp.int32, (tm, tm), 1)
    before = (cc < rr).astype(BF16)
    prefix = _dot(before, onehot) + carry_scr[...]
    r0 = jnp.sum(jnp.where(sel0, prefix, 0.0), axis=-1, keepdims=True)
    r1 = jnp.sum(jnp.where(sel1, prefix, 0.0), axis=-1, keepdims=True)
    carry_scr[...] += jnp.sum(onehot.astype(F32), axis=0, keepdims=True)
    cnt_ref[...] = carry_scr[...]
    meta = jnp.where(lane == 0, i0, 0.0)
    meta = jnp.where(lane == 1, i1, meta)
    meta = jnp.where(lane == 2, g0, meta)
    meta = jnp.where(lane == 3, g1, meta)
    meta = jnp.where(lane == 4, r0, meta)
    meta = jnp.where(lane == 5, r1, meta)
    meta_ref[...] = meta


def _router(hh, modl, modc, g, whi, wlo, *, rows, tm, seq, n_exp):
    b, nt, d = hh.shape
    return pl.pallas_call(
        functools.partial(_router_kernel, seq=seq, n_exp=n_exp),
        grid=(b, rows // tm),
        in_specs=[pl.BlockSpec((None, tm, d), lambda bb, i: (bb, i, 0)),
                  pl.BlockSpec((None, 6, d), lambda bb, i: (bb, 0, 0)),
                  pl.BlockSpec((None, 6, d), lambda bb, i: (0, 0, 0)),
                  pl.BlockSpec((1, d), lambda bb, i: (0, 0)),
                  pl.BlockSpec((d, LANE), lambda bb, i: (0, 0)),
                  pl.BlockSpec((d, LANE), lambda bb, i: (0, 0))],
        out_specs=[pl.BlockSpec((None, tm, d), lambda bb, i: (bb, i, 0)),
                   pl.BlockSpec((None, tm, LANE), lambda bb, i: (bb, i, 0)),
                   pl.BlockSpec((1, LANE), lambda bb, i: (0, 0))],
        out_shape=[jax.ShapeDtypeStruct((b, rows, d), F32),
                   jax.ShapeDtypeStruct((b, rows, LANE), F32),
                   jax.ShapeDtypeStruct((1, LANE), F32)],
        scratch_shapes=[pltpu.VMEM((1, LANE), F32)],
        compiler_params=_cparams("arbitrary", "arbitrary"),
        name="moe_router",
    )(hh, modl, modc, g, whi, wlo)


def _dispatch_kernel(dest_ref, fill_ref, u_ref, xg_hbm, zero_scr, sem, *, tm, rows_per_b):
    base = (pl.program_id(0) * rows_per_b + pl.program_id(1) * tm) * TOP_K

    @pl.when((pl.program_id(0) == 0) & (pl.program_id(1) == 0))
    def _():
        zero_scr[...] = jnp.zeros_like(zero_scr)

        def fill(blk, c):
            @pl.when(fill_ref[blk] != 0)
            def _():
                row0 = pl.multiple_of(blk * MOE_BLOCK, MOE_BLOCK)
                cp = pltpu.make_async_copy(zero_scr, xg_hbm.at[pl.ds(row0, MOE_BLOCK)], sem)
                cp.start()
                cp.wait()
            return c

        lax.fori_loop(0, fill_ref.shape[0], fill, 0)

    def issue(r, c):
        for k in range(TOP_K):
            dst = xg_hbm.at[pl.ds(dest_ref[base + r * TOP_K + k], 1)]
            pltpu.make_async_copy(u_ref.at[pl.ds(r, 1)], dst, sem).start(priority=k % 2)
        return c

    lax.fori_loop(0, tm, issue, 0, unroll=8)
    for k in range(TOP_K):
        pltpu.make_async_copy(u_ref, xg_hbm.at[pl.ds(0, tm)], sem).wait()


def _dispatch(dest, fill, u, *, tm):
    b, rows, d = u.shape
    grid_spec = pltpu.PrefetchScalarGridSpec(
        num_scalar_prefetch=2, grid=(b, rows // tm),
        in_specs=[pl.BlockSpec((tm, d), lambda bb, i, ds, fl: (bb * (rows // tm) + i, 0))],
        out_specs=pl.BlockSpec(memory_space=pl.ANY),
        scratch_shapes=[pltpu.VMEM((MOE_BLOCK, d), F32), pltpu.SemaphoreType.DMA(())])
    return pl.pallas_call(
        functools.partial(_dispatch_kernel, tm=tm, rows_per_b=rows),
        grid_spec=grid_spec,
        out_shape=jax.ShapeDtypeStruct((fill.shape[0] * MOE_BLOCK, d), F32),
        compiler_params=_cparams("arbitrary", "arbitrary"),
        name="moe_dispatch",
    )(dest, fill, u.reshape(b * rows, d))


def _expert_kernel(be_ref, nu_ref, x_ref, wg_ref, wu_ref, wd_ref, y_ref, xb_scr):
    blk = pl.program_id(0)
    j = pl.program_id(1)
    used = blk < nu_ref[0]

    @pl.when(j == 0)
    def _():
        y_ref[...] = jnp.zeros_like(y_ref)

    @pl.when(used & (j == 0))
    def _():
        xb_scr[...] = x_ref[...].astype(BF16)

    @pl.when(used)
    def _():
        y_ref[...] += _swiglu_tile(xb_scr[...], wg_ref, wu_ref, wd_ref)


def _experts(block_expert, n_used, xg, wg, wu, wd, *, th, layer):
    p, d = xg.shape
    hid = wg.shape[3]
    nb = p // MOE_BLOCK
    nj = hid // th

    def blk_eff(blk, nu):
        return jnp.minimum(blk, nu[0] - 1)

    def j_eff(blk, j, nu):
        return jnp.where(blk < nu[0], j, nj - 1)

    def w_in_map(blk, j, be, nu):
        return (layer, be[blk_eff(blk, nu)], 0, j_eff(blk, j, nu))

    def w_out_map(blk, j, be, nu):
        return (layer, be[blk_eff(blk, nu)], j_eff(blk, j, nu), 0)

    grid_spec = pltpu.PrefetchScalarGridSpec(
        num_scalar_prefetch=2, grid=(nb, nj),
        in_specs=[pl.BlockSpec((MOE_BLOCK, d), lambda blk, j, be, nu: (blk_eff(blk, nu), 0)),
                  pl.BlockSpec((None, None, d, th), w_in_map),
                  pl.BlockSpec((None, None, d, th), w_in_map),
                  pl.BlockSpec((None, None, th, d), w_out_map)],
        out_specs=pl.BlockSpec((MOE_BLOCK, d), lambda blk, j, be, nu: (blk, 0)),
        scratch_shapes=[pltpu.VMEM((MOE_BLOCK, d), BF16)])
    return pl.pallas_call(
        _expert_kernel, grid_spec=grid_spec,
        out_shape=jax.ShapeDtypeStruct((p, d), F32),
        compiler_params=_cparams("arbitrary", "arbitrary"),
        name="moe_experts",
    )(block_expert, n_used, xg, wg, wu, wd)


def _combine_kernel(dest_ref, yg_hbm, meta_ref, h_ref, modl_ref, modc_ref, o_ref, y0_scr, y1_scr, sem, *, tm,
                    rows_per_b, seq):
    bb = pl.program_id(0)
    i = pl.program_id(1)
    base = (bb * rows_per_b + i * tm) * TOP_K
    bufs = (y0_scr, y1_scr)

    def issue(r, c):
        for k in range(TOP_K):
            src = yg_hbm.at[pl.ds(dest_ref[base + r * TOP_K + k], 1)]
            pltpu.make_async_copy(src, bufs[k].at[pl.ds(r, 1)], sem).start(priority=k % 2)
        return c

    lax.fori_loop(0, tm, issue, 0, unroll=8)
    for k in range(TOP_K):
        pltpu.make_async_copy(yg_hbm.at[pl.ds(0, tm)], bufs[k], sem).wait()
    meta = meta_ref[...]
    y = meta[:, 2:3] * y0_scr[...] + meta[:, 3:4] * y1_scr[...]
    gate = _gate_rows(modl_ref, modc_ref, 5, tm, i * tm, seq)
    o_ref[...] = h_ref[...] + gate * y


def _combine(dest, yg, meta, hh, modl, modc, *, tm, seq):
    b, rows, _ = meta.shape
    d = hh.shape[2]
    grid_spec = pltpu.PrefetchScalarGridSpec(
        num_scalar_prefetch=1, grid=(b, rows // tm),
        in_specs=[pl.BlockSpec(memory_space=pl.ANY),
                  pl.BlockSpec((None, tm, LANE), lambda bb, i, ds: (bb, i, 0)),
                  pl.BlockSpec((None, tm, d), lambda bb, i, ds: (bb, i, 0)),
                  pl.BlockSpec((None, 6, d), lambda bb, i, ds: (bb, 0, 0)),
                  pl.BlockSpec((None, 6, d), lambda bb, i, ds: (0, 0, 0))],
        out_specs=pl.BlockSpec((None, tm, d), lambda bb, i, ds: (bb, i, 0)),
        scratch_shapes=[pltpu.VMEM((tm, d), F32), pltpu.VMEM((tm, d), F32), pltpu.SemaphoreType.DMA(())])
    return pl.pallas_call(
        functools.partial(_combine_kernel, tm=tm, rows_per_b=rows, seq=seq),
        grid_spec=grid_spec,
        out_shape=jax.ShapeDtypeStruct((b, rows, d), F32),
        compiler_params=_cparams("arbitrary", "arbitrary"),
        name="moe_combine",
    )(dest, yg, meta, hh, modl, modc)


def _moe(hh, modl, modc, g, w_router, wg, wu, wd, *, rows, seq, layer):
    b, nt, d = hh.shape
    n_exp = w_router.shape[1]
    wr = jnp.pad(w_router, ((0, 0), (0, LANE - n_exp)))
    whi, wlo = _split2(wr)
    tm = _pick(rows, (768, 512, 256))
    u, meta, counts = _router(hh, modl, modc, g, whi, wlo, rows=rows, tm=tm, seq=seq, n_exp=n_exp)

    cnt = counts[0, :n_exp].astype(jnp.int32)
    padded = (cnt + MOE_BLOCK - 1) // MOE_BLOCK * MOE_BLOCK
    pend = jnp.cumsum(padded)
    pstart = pend - padded
    n_assign = b * rows * TOP_K
    nb = -(-(n_assign + n_exp * (MOE_BLOCK - 1)) // MOE_BLOCK)
    idx = meta[:, :, 0:TOP_K].astype(jnp.int32)
    rank = meta[:, :, 4:4 + TOP_K].astype(jnp.int32)
    dest = (sum(jnp.where(idx == e, pstart[e], 0) for e in range(n_exp)) + rank).reshape(-1)
    blk_start = jnp.arange(nb, dtype=jnp.int32) * MOE_BLOCK
    block_expert = jnp.minimum(jnp.sum(pend[None, :] <= blk_start[:, None], axis=1), n_exp - 1).astype(jnp.int32)
    n_used = (pend[-1:] // MOE_BLOCK).astype(jnp.int32)

    blk_id = jnp.arange(nb, dtype=jnp.int32)
    holds_padding = jnp.any((pend[None, :] // MOE_BLOCK - 1 == blk_id[:, None]) & (padded[None, :] > 0), axis=1)
    fill = (holds_padding | (blk_id >= n_used[0])).astype(jnp.int32)
    td = tm
    xg = _dispatch(dest, fill, u, tm=td)
    yg = _experts(block_expert, n_used, xg, wg, wu, wd, th=_pick(wg.shape[3], (512, 256, 128)), layer=layer)
    return _combine(dest, yg, meta, hh, modl, modc, tm=td, seq=seq)


def _rope_tables(seq, nt):
    rows = seq // GRID_W
    row = jnp.repeat(jnp.arange(rows), GRID_W).astype(F32)
    col = jnp.tile(jnp.arange(GRID_W), rows).astype(F32)
    n_freq = ATT_HEAD_DIM // 4
    inv_freq = ROPE_THETA ** (-jnp.arange(n_freq, dtype=F32) / n_freq)
    ang = jnp.concatenate([row[:, None] * inv_freq, col[:, None] * inv_freq], axis=-1)
    cos, sin = jnp.cos(ang), jnp.sin(ang)
    cos_h = jnp.concatenate([cos, cos], axis=-1)
    sin_h = jnp.concatenate([-sin, sin], axis=-1)
    reps = LANE // ATT_HEAD_DIM
    cos_t = jnp.tile(cos_h, (1, reps))
    sin_t = jnp.tile(sin_h, (1, reps))
    pad = nt - seq
    cos_t = jnp.concatenate([cos_t, jnp.ones((pad, LANE), F32)], axis=0)
    sin_t = jnp.concatenate([sin_t, jnp.zeros((pad, LANE), F32)], axis=0)
    return cos_t, sin_t


def _head_mean_matrix(n):
    r = jnp.arange(n) // ATT_HEAD_DIM
    return (r[:, None] == r[None, :]).astype(F32).astype(BF16) * jnp.asarray(1.0 / ATT_HEAD_DIM, BF16)


def kernel(x, c, ctx, c_ctx, w_mod, b_mod, norm_g, ssm_w_in, ssm_conv_w, ssm_conv_b, ssm_dt_bias, ssm_a_log, ssm_d, ssm_norm_g, ssm_w_out, att_w_qkv, att_q_norm, att_k_norm, att_w_out, pool_w_in, pool_w_grp, pool_scale, lru_w_in, lru_conv_w, lru_conv_b, lru_gate_w, lru_gate_b, lru_lambda, lru_w_out, ffn_w_gate, ffn_w_up, ffn_w_down, moe_w_router, moe_w_gate, moe_w_up, moe_w_down):
    b, seq, d = x.shape
    nctx = ctx.shape[1]
    nt = seq + nctx
    depth = w_mod.shape[0]
    n_mixers = 4

    bp = -(-(b + 1) // 8) * 8
    s_rows = jnp.concatenate([c, c_ctx[None, :], jnp.zeros((bp - b - 1, d), F32)], axis=0)
    mods = _mod_vectors(s_rows, w_mod, b_mod)
    modl_all = mods[:, :b].reshape(depth, b, 6, d)
    modc_all = mods[:, b:b + 1].reshape(depth, 1, 6, d)

    hh = jnp.concatenate([x, ctx], axis=1)
    tm_row = _pick(nt, (768, 512, 256))
    cos_t, sin_t = _rope_tables(seq, nt)
    out = None

    for i in range(depth):
        last = i == depth - 1
        kind, j = i % n_mixers, i // n_mixers
        modl, modc = modl_all[i], modc_all[i]
        g1 = norm_g[i, 0][None, :]
        g2 = norm_g[i, 1][None, :]
        proj = functools.partial(_proj, hh, modl, modc, g1, seq=seq, k_shift=0, k_scale=1)

        if kind == 0:
            inner = ssm_d.shape[1] * SSM_HEAD_DIM
            heads = ssm_d.shape[1]
            conv_dim = ssm_conv_w.shape[2]
            w_in = ssm_w_in[j].astype(BF16)
            w_z = w_in[:, :inner]
            w_xbc = w_in[:, inner:inner + conv_dim]
            w_dt = jnp.pad(w_in[:, inner + conv_dim:], ((0, 0), (0, LANE - 2 * heads)))
            dt_bias = jnp.pad(ssm_dt_bias[j].reshape(1, 2 * heads), ((0, 0), (0, LANE - 2 * heads)))
            z = proj(w_z, epi=_epi_plain, tm=tm_row, tn=inner, out_dtype=BF16, name="ssd_proj_z")
            xbc = proj(w_xbc, epi=_epi_conv_silu, extra=(ssm_conv_w[j], ssm_conv_b[j][None, :]),
                       extra_specs=(_col_spec(ssm_conv_w.shape[1], 512), _col_spec(1, 512)),
                       tm=nt, tn=512, out_dtype=BF16, name="ssd_proj_xbc")
            dt = proj(w_dt, epi=_epi_softplus, extra=(dt_bias,), extra_specs=(_col_spec(1, LANE),),
                      tm=tm_row, tn=LANE, out_dtype=F32, name="ssd_proj_dt")
            nega = jnp.pad(-jnp.exp(ssm_a_log[j].astype(F32)).reshape(1, 2 * heads), ((0, 0), (0, LANE - 2 * heads)))
            head_of = jnp.arange(inner)[None, :] // SSM_HEAD_DIM
            lanes = jnp.arange(LANE)[:, None]
            expand = jnp.stack([(lanes == head_of + dd * heads) for dd in range(2)]).astype(F32).astype(BF16)
            dskip = jnp.repeat(ssm_d[j].astype(F32), SSM_HEAD_DIM)[None, :]
            yf, yb = _ssd(xbc, dt, nega, dskip, expand, seq=seq, heads=heads)
            hh = _mmres((yf, yb, z), (inner, inner, inner), ssm_w_out[j].astype(BF16), hh, modl, modc,
                        pro=_pro_ssd, tm=tm_row, k_gate=2, seq=seq, name="ssd_out",
                        row_ins=(ssm_norm_g[j][None, :],))
        elif kind == 1:
            nq = att_w_out.shape[1]
            nk = ATT_KV_HEADS * ATT_HEAD_DIM
            w_qkv = att_w_qkv[j].astype(BF16)
            tab_specs = (pl.BlockSpec((tm_row, LANE), lambda bb, ii, jj: (ii, 0)),) * 2

            def qk_extra(n, gain):
                return ((_head_mean_matrix(n), jnp.tile(gain, n // ATT_HEAD_DIM)[None, :], cos_t, sin_t),
                        (pl.BlockSpec((n, n), lambda bb, ii, jj: (0, 0)),
                         pl.BlockSpec((1, n), lambda bb, ii, jj: (0, 0))) + tab_specs)

            ex, sp = qk_extra(nq, att_q_norm[j])
            q = proj(w_qkv[:, :nq], epi=functools.partial(_epi_qk, out_scale=ATT_HEAD_DIM ** -0.5 * math.log2(math.e)),
                     extra=ex,
                     extra_specs=sp, tm=tm_row, tn=nq, out_dtype=BF16, name="att_proj_q")
            ex, sp = qk_extra(nk, att_k_norm[j])
            k = proj(w_qkv[:, nq:nq + nk], epi=functools.partial(_epi_qk, out_scale=1.0), extra=ex,
                     extra_specs=sp, tm=tm_row, tn=nk, out_dtype=BF16, name="att_proj_k")
            v = proj(w_qkv[:, nq + nk:], epi=_epi_plain, tm=tm_row, tn=nk, out_dtype=BF16, name="att_proj_v")
            o = _attention(q, k, v, seq=seq, tq=_pick(nctx, (128,)))
            hh = _mmres((o,), (nq,), att_w_out[j].astype(BF16), hh, modl, modc, pro=_pro_plain, tm=tm_row,
                        k_gate=2, seq=seq, name="att_out")
        elif kind == 2:
            v = proj(pool_w_in[j].astype(BF16), epi=_epi_plain, tm=tm_row, tn=d, out_dtype=F32,
                     name="pool_proj")
            hh = _pool(v, hh, modl, modc, pool_w_grp[j].astype(BF16), pool_scale[j][None, :], seq=seq,
                       tm=_pick(nctx, (256, 128)))
        else:
            width = lru_w_out.shape[1]
            w_in = lru_w_in[j].astype(BF16)
            gg = proj(w_in[:, :width], epi=_epi_gelu, tm=tm_row, tn=width, out_dtype=BF16, name="lru_proj_gate")
            tn_x = _pick(width, (640, 256, 128))
            xr = proj(w_in[:, width:], epi=_epi_conv_bias, extra=(lru_conv_w[j], lru_conv_b[j][None, :]),
                      extra_specs=(_col_spec(lru_conv_w.shape[1], tn_x), _col_spec(1, tn_x)),
                      tm=nt, tn=tn_x, out_dtype=BF16, name="lru_proj_x")
            gw = lru_gate_w[j]
            w_cat = jnp.concatenate([gw[:, 0], gw[:, 1]], axis=-1).astype(BF16)
            hs = _lru(xr, w_cat, lru_gate_b[j], lru_lambda[j][:, None, :], seq=seq, tc=_pick(nctx, (256, 128)))
            rows = seq if last else nt
            hh = _mmres(((hs, 0), (hs, 1), gg), (width, width, width), lru_w_out[j].astype(BF16), hh, modl, modc,
                        pro=_pro_lru, tm=_pick(rows, (1024, 768, 512, 256)), k_gate=2, seq=seq, name="lru_out",
                        rows=rows)

        kf = i // 2
        if i % 2 == 0:
            hh = _dense_ffn(hh, modl, modc, g2, ffn_w_gate[kf].astype(BF16), ffn_w_up[kf].astype(BF16),
                            ffn_w_down[kf].astype(BF16), tm=tm_row,
                            th=_pick(ffn_w_gate.shape[2], (512, 256, 128)), seq=seq)
        else:
            rows = seq if last else nt
            hh = _moe(hh, modl, modc, g2, moe_w_router[kf], moe_w_gate, moe_w_up, moe_w_down, rows=rows, seq=seq,
                      layer=kf)
        if last:
            out = hh[:, :seq] if hh.shape[1] != seq else hh
    return out
```
